```python
import math
import jax, jax.numpy as jnp
from jax import lax
import numpy as np

D_MODEL = 1024
BATCH = 2
SEQ = 8192
DEPTH = 2
DEC_BATCH = 32
DEC_SEQ = 8
PAST_LEN = 8192
PAGE_SIZE = 128

N_SELF = DEPTH // 2
N_CROSS = DEPTH - N_SELF
HEAD_DIM = 64
N_HEADS = D_MODEL // HEAD_DIM
DIL_GROUPS = ((128, 1), (512, 4), (2048, 16))
N_GROUPS = len(DIL_GROUPS)
MOBA_BLOCK = 256
MOBA_TOPK = 3
NUM_BUCKETS = 32
REL_MAX_DIST = 4096
D_FF = ((8 * D_MODEL // 3 + 127) // 128) * 128
CONV_W = 3
EPS = 1e-6
SCALE = HEAD_DIM ** -0.5
PROMPT_Q_CHUNK = 64

kernel_name = "yoco_dilated_moba_decoder_step"


def rmsnorm(x, g):
    xf = x.astype(jnp.float32)
    y = xf * lax.rsqrt(jnp.mean(xf * xf, axis=-1, keepdims=True) + EPS)
    return (y * g.astype(jnp.float32)).astype(x.dtype)


def rel_bucket(dist):
    n = jnp.maximum(dist, 0)
    exact = NUM_BUCKETS // 2
    nf = jnp.maximum(n, exact).astype(jnp.float32)
    large = exact + (jnp.log(nf / exact) / math.log(REL_MAX_DIST / exact) * (NUM_BUCKETS - exact)).astype(jnp.int32)
    return jnp.where(n < exact, n, jnp.minimum(large, NUM_BUCKETS - 1))


def a_project(h, w_qkv, qg, kg):
    B, S, _ = h.shape
    qkv = (h @ w_qkv).reshape(B, S, 3, N_GROUPS, N_HEADS, HEAD_DIM)
    return rmsnorm(qkv[:, :, 0], qg), rmsnorm(qkv[:, :, 1], kg), qkv[:, :, 2]


def dilated_prompt(q, k, v, bias_k, dil, n_back):
    B, S, H, Dh = q.shape
    span = dil * n_back
    Lp = -(-S // span) * span
    nb = Lp // span

    def to_blocks(t):
        t = jnp.pad(t, ((0, 0), (0, Lp - S), (0, 0), (0, 0)))
        t = jnp.swapaxes(t.reshape(B, Lp // dil, dil, H, Dh), 1, 2)
        return t.reshape(B, dil, nb, n_back, H, Dh)

    def with_prev(t):
        prev = jnp.pad(t[:, :, :-1], ((0, 0), (0, 0), (1, 0), (0, 0), (0, 0), (0, 0)))
        return jnp.concatenate([prev, t], axis=3)

    qb = to_blocks(q)
    kk = with_prev(to_blocks(k))
    vv = with_prev(to_blocks(v))
    steps = jnp.arange(n_back)[:, None] + n_back - jnp.arange(2 * n_back)[None, :]
    band = (steps >= 0) & (steps <= n_back)
    ok = band[None] & ((jnp.arange(nb) > 0)[:, None, None] | (jnp.arange(2 * n_back) >= n_back)[None, None, :])
    bias = jnp.transpose(bias_k[jnp.clip(steps, 0, n_back)], (2, 0, 1)).astype(jnp.float32)
    logits = jnp.einsum('brcihd,brcjhd->brchij', qb, kk).astype(jnp.float32) * SCALE + bias
    logits = jnp.where(ok[None, None, :, None], logits, -jnp.inf)
    m = jnp.max(logits, axis=-1, keepdims=True)
    e = jnp.exp(logits - m)
    den = jnp.sum(e, axis=-1, keepdims=True)
    o = jnp.einsum('brchij,brcjhd->brcihd', (e / den).astype(v.dtype), vv)
    lse = jnp.swapaxes((m + jnp.log(den))[..., 0], 3, 4)

    def unblock(t):
        t = t.reshape((B, dil, Lp // dil) + t.shape[4:])
        return jnp.swapaxes(t, 1, 2).reshape((B, Lp) + t.shape[3:])[:, :S]

    return unblock(o), unblock(lse)


def dilated_sample(q, k, v, buf, bias_k, dil, n_back):
    L = buf.shape[1]
    T = q.shape[1]
    full = jnp.concatenate([buf, jnp.stack([k, v], axis=2).astype(buf.dtype)], axis=1)
    idx = L + jnp.arange(T)[:, None] - dil * jnp.arange(n_back + 1)[None, :]
    ok = idx >= 0
    g = full[:, jnp.maximum(idx, 0)]
    logits = jnp.einsum('bthd,btkhd->bthk', q, g[:, :, :, 0]).astype(jnp.float32) * SCALE + bias_k.T.astype(jnp.float32)
    logits = jnp.where(ok[None, :, None, :], logits, -jnp.inf)
    m = jnp.max(logits, axis=-1, keepdims=True)
    e = jnp.exp(logits - m)
    den = jnp.sum(e, axis=-1, keepdims=True)
    o = jnp.einsum('bthk,btkhd->bthd', (e / den).astype(v.dtype), g[:, :, :, 1])
    lse = (m + jnp.log(den))[..., 0]
    return o, lse, full[:, T:]


def merge_groups(outs, lses):
    w = jax.nn.softmax(jnp.stack(lses, 0), axis=0)
    return jnp.einsum('gbsh,gbshd->bshd', w.astype(outs[0].dtype), jnp.stack(outs, 0))


def moba_chunk(q, q_pos, k_blocks, v_blocks, k_means, rel_bias):
    B, Q, H, Dh = q.shape
    NB = k_blocks.shape[1]
    n_sel = min(MOBA_TOPK, NB)
    own = q_pos // MOBA_BLOCK
    score = jnp.einsum('bqhd,bnhd->bqhn', q.astype(jnp.float32), k_means)
    past = jnp.arange(NB)[None, :] < own[:, None]
    score = jnp.where(past[None, :, None, :], score, -jnp.inf)
    _, top = lax.top_k(score, n_sel)
    sel = jnp.concatenate([top, jnp.broadcast_to(own[None, :, None, None], (B, Q, H, 1))], axis=-1)
    sel_ok = jnp.concatenate([jnp.arange(n_sel)[None, :] < own[:, None], jnp.ones((Q, 1), bool)], axis=-1)
    bi = jnp.arange(B)[:, None, None, None]
    hi = jnp.arange(H)[None, None, :, None]
    kg = k_blocks[bi, sel, :, hi, :]
    vg = v_blocks[bi, sel, :, hi, :]
    key_pos = sel[..., None] * MOBA_BLOCK + jnp.arange(MOBA_BLOCK)
    dist = q_pos[None, :, None, None, None] - key_pos
    ok = (dist >= 0) & sel_ok[None, :, None, :, None]
    bias = rel_bias[rel_bucket(dist), hi[..., None]].astype(jnp.float32)
    logits = jnp.einsum('bqhd,bqhnkd->bqhnk', q, kg).astype(jnp.float32) * SCALE + bias
    logits = jnp.where(ok, logits, -jnp.inf)
    p = jax.nn.softmax(logits.reshape(B, Q, H, -1), axis=-1).reshape(logits.shape)
    return jnp.einsum('bqhnk,bqhnkd->bqhd', p.astype(vg.dtype), vg)


def moba_attend(q, q_pos, k_seq, v_seq, rel_bias, chunk):
    B, L, H, Dh = k_seq.shape
    NB = -(-L // MOBA_BLOCK)
    pad = ((0, 0), (0, NB * MOBA_BLOCK - L), (0, 0), (0, 0))
    kb = jnp.pad(k_seq, pad).reshape(B, NB, MOBA_BLOCK, H, Dh)
    vb = jnp.pad(v_seq, pad).reshape(B, NB, MOBA_BLOCK, H, Dh)
    k_means = jnp.mean(kb.astype(jnp.float32), axis=2)
    Q = q.shape[1]
    nc = Q // chunk
    qc = jnp.swapaxes(q.reshape(B, nc, chunk, H, Dh), 0, 1)
    pc = q_pos.reshape(nc, chunk)
    out = lax.map(lambda a: moba_chunk(a[0], a[1], kb, vb, k_means, rel_bias), (qc, pc))
    return jnp.swapaxes(out, 0, 1).reshape(B, Q, H, Dh)


def conv_ffn(x, prev, g, w_up, cw, cb, w_down):
    S = x.shape[1]
    u = rmsnorm(x, g) @ w_up
    ue = jnp.concatenate([prev.astype(u.dtype), u], axis=1)
    c = cb
    for j in range(CONV_W):
        c = c + cw[j] * ue[:, j:j + S]
    a, b = jnp.split(c, 2, axis=-1)
    y = (jax.nn.silu(a) * b) @ w_down
    return x + y, ue[:, -(CONV_W - 1):]


def setup_inputs(seed: int = 0) -> dict:
    key = jax.random.key(seed)
    ks = jax.random.split(key, 32)

    def nrm(i, shape, scale=1.0):
        return jax.random.normal(ks[i], shape, jnp.float32) * scale

    def gain(i, shape):
        return 1.0 + 0.05 * nrm(i, shape)

    hd = N_HEADS * HEAD_DIM
    n_pages = PAST_LEN // PAGE_SIZE
    n_used = DEC_BATCH * n_pages
    n_pool = n_used + max(1, n_used // 4)

    def a_shape(win):
        return (N_SELF, DEC_BATCH, min(win, PAST_LEN), 2, N_HEADS, HEAD_DIM)

    page_table = jax.random.permutation(ks[6], n_pool)[:n_used].reshape(DEC_BATCH, n_pages).astype(jnp.int32)
    return {
        'x_prompt': nrm(0, (BATCH, SEQ, D_MODEL)),
        'x_sample': nrm(1, (DEC_BATCH, DEC_SEQ, D_MODEL)),
        'cache_a_kv_g0': nrm(2, a_shape(DIL_GROUPS[0][0])),
        'cache_a_kv_g1': nrm(3, a_shape(DIL_GROUPS[1][0])),
        'cache_a_kv_g2': nrm(4, a_shape(DIL_GROUPS[2][0])),
        'cache_b_kv_pool': nrm(5, (n_pool, PAGE_SIZE, 2, N_HEADS, HEAD_DIM)),
        'page_table': page_table,
        'state_ffn_conv': nrm(7, (DEPTH, DEC_BATCH, CONV_W - 1, 2 * D_FF)),
        'rel_bias': nrm(8, (NUM_BUCKETS, N_HEADS), 0.5),
        'norm_attn': gain(9, (DEPTH, D_MODEL)),
        'norm_ffn': gain(10, (DEPTH, D_MODEL)),
        'w_qkv_a': nrm(11, (N_SELF, D_MODEL, 3 * N_GROUPS * hd), D_MODEL ** -0.5),
        'q_norm_a': gain(12, (N_SELF, HEAD_DIM)),
        'k_norm_a': gain(13, (N_SELF, HEAD_DIM)),
        'w_o_a': nrm(14, (N_SELF, hd, D_MODEL), hd ** -0.5),
        'norm_kv': gain(15, (D_MODEL,)),
        'w_kv_b': nrm(16, (D_MODEL, 2 * hd), D_MODEL ** -0.5),
        'k_norm_b': gain(17, (HEAD_DIM,)),
        'w_q_b': nrm(18, (N_CROSS, D_MODEL, hd), D_MODEL ** -0.5),
        'q_norm_b': gain(19, (N_CROSS, HEAD_DIM)),
        'w_o_b': nrm(20, (N_CROSS, hd, D_MODEL), hd ** -0.5),
        'w_up': nrm(21, (DEPTH, D_MODEL, 2 * D_FF), D_MODEL ** -0.5),
        'conv_w': nrm(22, (DEPTH, CONV_W, 2 * D_FF), CONV_W ** -0.5),
        'conv_b': nrm(23, (DEPTH, 2 * D_FF), 0.01),
        'w_down': nrm(24, (DEPTH, D_FF, D_MODEL), D_FF ** -0.5),
    }


def reference(x_prompt, x_sample, cache_a_kv_g0, cache_a_kv_g1, cache_a_kv_g2,
              cache_b_kv_pool, page_table, state_ffn_conv,
              rel_bias, norm_attn, norm_ffn, w_qkv_a, q_norm_a, k_norm_a, w_o_a,
              norm_kv, w_kv_b, k_norm_b, w_q_b, q_norm_b, w_o_b,
              w_up, conv_w, conv_b, w_down):
    group_bias = [rel_bias[rel_bucket(dil * jnp.arange(win // dil + 1))] for (win, dil) in DIL_GROUPS]

    def trunk(x, prompt, a_bufs, past_k, past_v, conv_prev):
        B, S, _ = x.shape
        pos0 = 0 if prompt else past_k.shape[1]
        q_pos = pos0 + jnp.arange(S, dtype=jnp.int32)
        a_new = [[] for _ in range(N_GROUPS)]
        conv_new = []
        kv_new = None
        k_seq = None
        v_seq = None
        for l in range(DEPTH):
            if l < N_SELF:
                h = rmsnorm(x, norm_attn[l])
                q, k, v = a_project(h, w_qkv_a[l], q_norm_a[l], k_norm_a[l])
                outs, lses = [], []
                for gi, (win, dil) in enumerate(DIL_GROUPS):
                    n_back = win // dil
                    if prompt:
                        o, lse = dilated_prompt(q[:, :, gi], k[:, :, gi], v[:, :, gi], group_bias[gi], dil, n_back)
                        keep = min(win, S)
                        buf = jnp.stack([k[:, S - keep:, gi], v[:, S - keep:, gi]], axis=2)
                    else:
                        o, lse, buf = dilated_sample(q[:, :, gi], k[:, :, gi], v[:, :, gi], a_bufs[gi][l], group_bias[gi], dil, n_back)
                    outs.append(o)
                    lses.append(lse)
                    a_new[gi].append(buf)
                x = x + merge_groups(outs, lses).reshape(B, S, -1) @ w_o_a[l]
            else:
                if l == N_SELF:
                    kv = (rmsnorm(x, norm_kv) @ w_kv_b).reshape(B, S, 2, N_HEADS, HEAD_DIM)
                    k_new = rmsnorm(kv[:, :, 0], k_norm_b)
                    v_new = kv[:, :, 1]
                    kv_new = jnp.stack([k_new, v_new], axis=2)
                    if prompt:
                        k_seq, v_seq = k_new, v_new
                    else:
                        k_seq = jnp.concatenate([past_k, k_new.astype(past_k.dtype)], axis=1)
                        v_seq = jnp.concatenate([past_v, v_new.astype(past_v.dtype)], axis=1)
                lb = l - N_SELF
                h = rmsnorm(x, norm_attn[l])
                q = rmsnorm((h @ w_q_b[lb]).reshape(B, S, N_HEADS, HEAD_DIM), q_norm_b[lb])
                o = moba_attend(q, q_pos, k_seq, v_seq, rel_bias, PROMPT_Q_CHUNK if prompt else 1)
                x = x + o.reshape(B, S, -1) @ w_o_b[lb]
            prev = jnp.zeros((B, CONV_W - 1, 2 * D_FF), x.dtype) if prompt else conv_prev[l]
            x, cs = conv_ffn(x, prev, norm_ffn[l], w_up[l], conv_w[l], conv_b[l], w_down[l])
            conv_new.append(cs)
        return x, [jnp.stack(a, 0) for a in a_new], kv_new, jnp.stack(conv_new, 0)

    dec_b = x_sample.shape[0]
    past_len = page_table.shape[1] * cache_b_kv_pool.shape[1]
    past_k = cache_b_kv_pool[page_table, :, 0].reshape(dec_b, past_len, N_HEADS, HEAD_DIM)
    past_v = cache_b_kv_pool[page_table, :, 1].reshape(dec_b, past_len, N_HEADS, HEAD_DIM)

    y_p, a_p, kv_p, conv_p = trunk(x_prompt, True, None, None, None, None)
    y_s, a_s, kv_s, conv_s = trunk(x_sample, False, [cache_a_kv_g0, cache_a_kv_g1, cache_a_kv_g2],
                                   past_k, past_v, state_ffn_conv)
    return (y_p, y_s, a_p[0], a_p[1], a_p[2], a_s[0], a_s[1], a_s[2], kv_p, kv_s, conv_p, conv_s)
```

```python
import functools
import math

import jax
import jax.numpy as jnp
from jax import lax
from jax.experimental import pallas as pl
from jax.experimental.pallas import tpu as pltpu

HEAD_DIM = 64
N_HEADS = 16
HD = N_HEADS * HEAD_DIM
DIL_GROUPS = ((128, 1), (512, 4), (2048, 16))
N_GROUPS = len(DIL_GROUPS)
N_BACK = 128
MOBA_BLOCK = 256
MOBA_TOPK = 3
NUM_BUCKETS = 32
REL_MAX_DIST = 4096
CONV_W = 3
EPS = 1e-6
SCALE = HEAD_DIM ** -0.5
NEG = -1e30
LANES = 128
HEADS_PER_LANE_TILE = LANES // HEAD_DIM
VMEM_LIMIT = 56 * 1024 * 1024

F32 = jnp.float32
BF16 = jnp.bfloat16
HIGHEST = lax.Precision.HIGHEST


def _params(sem, vmem=VMEM_LIMIT):
    return pltpu.CompilerParams(dimension_semantics=sem, vmem_limit_bytes=vmem)


def _const_spec(shape):
    nd = len(shape)
    return pl.BlockSpec(shape, lambda *_: (0,) * nd, pipeline_mode=pl.Buffered(1))


def _rel_bucket(dist):
    n = jnp.maximum(dist, 0)
    exact = NUM_BUCKETS // 2
    nf = jnp.maximum(n, exact).astype(F32)
    large = exact + (jnp.log(nf / exact) / math.log(REL_MAX_DIST / exact) * (NUM_BUCKETS - exact)).astype(jnp.int32)
    return jnp.where(n < exact, n, jnp.minimum(large, NUM_BUCKETS - 1))


def _norm_matmul_kernel(x_ref, g_ref, w_ref, hg_ref, bd_ref, o_ref, h_scr, *, n_norm_tiles):
    j = pl.program_id(1)

    @pl.when(j == 0)
    def _():
        x = x_ref[...]
        r = lax.rsqrt(jnp.mean(x * x, axis=-1, keepdims=True) + EPS)
        h_scr[...] = (x * r * g_ref[...]).astype(BF16)

    y = jnp.dot(h_scr[...], w_ref[...], preferred_element_type=F32)

    @pl.when(j < n_norm_tiles)
    def _():
        ss = jnp.dot((y * y).astype(BF16), bd_ref[...], preferred_element_type=F32)
        o_ref[...] = y * lax.rsqrt(ss * (1.0 / HEAD_DIM) + EPS) * hg_ref[...]

    @pl.when(j >= n_norm_tiles)
    def _():
        o_ref[...] = y


def norm_matmul(x, g, w_bf16, head_gain, n_norm_cols, *, tn=512):
    M, K = x.shape
    N = w_bf16.shape[1]
    tm = min(512, M)
    assert M % tm == 0 and N % tn == 0 and n_norm_cols % tn == 0 and tn % HEAD_DIM == 0
    hid = jnp.arange(tn) // HEAD_DIM
    bd = (hid[:, None] == hid[None, :]).astype(BF16)
    return pl.pallas_call(
        functools.partial(_norm_matmul_kernel, n_norm_tiles=n_norm_cols // tn),
        out_shape=jax.ShapeDtypeStruct((M, N), F32),
        grid=(M // tm, N // tn),
        in_specs=[
            pl.BlockSpec((tm, K), lambda i, j: (i, 0)),
            pl.BlockSpec((1, K), lambda i, j: (0, 0)),
            pl.BlockSpec((K, tn), lambda i, j: (0, j)),
            pl.BlockSpec((1, tn), lambda i, j: (0, j)),
            pl.BlockSpec((tn, tn), lambda i, j: (0, 0)),
        ],
        out_specs=pl.BlockSpec((tm, tn), lambda i, j: (i, j)),
        scratch_shapes=[pltpu.VMEM((tm, K), BF16)],
        compiler_params=_params(("parallel", "arbitrary")),
        name="norm_matmul",
    )(x, g.reshape(1, K), w_bf16, head_gain.reshape(1, N), bd)


def _merge_proj_kernel(*refs, n_g):
    o_refs = refs[:n_g]
    lse_refs = refs[n_g:2 * n_g] if n_g > 1 else ()
    w_ref, x_ref, out_ref = refs[-3:]
    if n_g == 1:
        a = o_refs[0][...]
    else:
        ls = [r[...] for r in lse_refs]
        mx = functools.reduce(jnp.maximum, ls)
        ws = [jnp.exp(l - mx) for l in ls]
        den = functools.reduce(lambda p, q: p + q, ws)
        num = functools.reduce(lambda p, q: p + q, [w * r[...] for w, r in zip(ws, o_refs)])
        a = num / den
    out_ref[...] = x_ref[...] + jnp.dot(a.astype(BF16), w_ref[...], preferred_element_type=F32)


def merge_proj(os_, lses, w_bf16, x):
    M, D = x.shape
    n_g = len(os_)
    tm = min(256, M)
    assert M % tm == 0
    row = pl.BlockSpec((tm, D), lambda i: (i, 0))
    ins = list(os_) + (list(lses) if n_g > 1 else [])
    return pl.pallas_call(
        functools.partial(_merge_proj_kernel, n_g=n_g),
        out_shape=jax.ShapeDtypeStruct((M, D), F32),
        grid=(M // tm,),
        in_specs=[row] * len(ins) + [_const_spec(w_bf16.shape), row],
        out_specs=row,
        compiler_params=_params(("parallel",)),
        name="merge_proj",
    )(*ins, w_bf16, x)


def _silu_gate(ca, cb):
    return (ca / (1.0 + jnp.exp(-ca))) * cb


def _ffn_prompt_kernel(x_ref, xp_ref, g_ref, wup_ref, cw_ref, cb_ref, wdn_ref, y_ref, ut_ref,
                       ua_scr, ub_scr, acc_scr, *, tm, ck, d_ff, tiles_per_seq):
    i = pl.program_id(0)
    x = x_ref[...]
    keep = jnp.where(i % tiles_per_seq == 0, 0.0, 1.0)
    xc = jnp.concatenate([xp_ref[...] * keep, x], axis=0)
    r = lax.rsqrt(jnp.mean(xc * xc, axis=-1, keepdims=True) + EPS)
    h = (xc * r * g_ref[...]).astype(BF16)
    acc_scr[...] = jnp.zeros_like(acc_scr)
    for c in range(d_ff // ck):
        a0, b0 = c * ck, d_ff + c * ck
        ua_scr[...] = jnp.dot(h, wup_ref[:, a0:a0 + ck], preferred_element_type=F32)
        ub_scr[...] = jnp.dot(h, wup_ref[:, b0:b0 + ck], preferred_element_type=F32)
        ut_ref[:, a0:a0 + ck] = ua_scr[tm:tm + 8, :]
        ut_ref[:, b0:b0 + ck] = ub_scr[tm:tm + 8, :]
        ca = cb_ref[:, a0:a0 + ck]
        cb = cb_ref[:, b0:b0 + ck]
        for j in range(CONV_W):
            ca = ca + cw_ref[j:j + 1, a0:a0 + ck] * ua_scr[6 + j:6 + j + tm, :]
            cb = cb + cw_ref[j:j + 1, b0:b0 + ck] * ub_scr[6 + j:6 + j + tm, :]
        gt = _silu_gate(ca, cb).astype(BF16)
        acc_scr[...] += jnp.dot(gt, wdn_ref[a0:a0 + ck, :], preferred_element_type=F32)
    y_ref[...] = x + acc_scr[...]


def ffn_prompt(x, seq_len, g, wup_bf16, cw, cb, wdn_bf16, *, tm=256, ck=256):
    M, D = x.shape
    d_ff = wdn_bf16.shape[0]
    assert seq_len % tm == 0 and M % seq_len == 0 and d_ff % ck == 0 and CONV_W - 1 <= 8
    n_tiles = M // tm
    y, ut = pl.pallas_call(
        functools.partial(_ffn_prompt_kernel, tm=tm, ck=ck, d_ff=d_ff, tiles_per_seq=seq_len // tm),
        out_shape=(jax.ShapeDtypeStruct((M, D), F32), jax.ShapeDtypeStruct((n_tiles * 8, 2 * d_ff), F32)),
        grid=(n_tiles,),
        in_specs=[
            pl.BlockSpec((tm, D), lambda i: (i, 0)),
            pl.BlockSpec((8, D), lambda i: (jnp.maximum(i * (tm // 8) - 1, 0), 0)),
            _const_spec((1, D)),
            _const_spec(wup_bf16.shape),
            _const_spec(cw.shape),
            _const_spec((1, 2 * d_ff)),
            _const_spec(wdn_bf16.shape),
        ],
        out_specs=(pl.BlockSpec((tm, D), lambda i: (i, 0)), pl.BlockSpec((8, 2 * d_ff), lambda i: (i, 0))),
        scratch_shapes=[pltpu.VMEM((tm + 8, ck), F32), pltpu.VMEM((tm + 8, ck), F32), pltpu.VMEM((tm, D), F32)],
        compiler_params=_params(("parallel",)),
        name="ffn_prompt",
    )(x, x, g.reshape(1, D), wup_bf16, cw, cb.reshape(1, 2 * d_ff), wdn_bf16)
    return y, ut


def _ffn_sample_kernel(x_ref, g_ref, wa_ref, wb_ref, cwa_ref, cwb_ref, cba_ref, cbb_ref,
                       e1a_ref, e1b_ref, e0a_ref, e0b_ref, wdn_ref, y_ref, ua_ref, ub_ref,
                       h_scr, ua_scr, ub_scr, *, m, t_len):
    c = pl.program_id(0)

    @pl.when(c == 0)
    def _():
        x = x_ref[...]
        r = lax.rsqrt(jnp.mean(x * x, axis=-1, keepdims=True) + EPS)
        h_scr[...] = (x * r * g_ref[...]).astype(BF16)
        y_ref[...] = x
        ua_scr[0:8, :] = jnp.zeros((8, ua_scr.shape[1]), F32)
        ub_scr[0:8, :] = jnp.zeros((8, ub_scr.shape[1]), F32)

    h = h_scr[...]
    ua = jnp.dot(h, wa_ref[...], preferred_element_type=F32)
    ub = jnp.dot(h, wb_ref[...], preferred_element_type=F32)
    ua_ref[...] = ua
    ub_ref[...] = ub
    ua_scr[8:8 + m, :] = ua
    ub_scr[8:8 + m, :] = ub
    t = lax.broadcasted_iota(jnp.int32, ua.shape, 0) % t_len

    def conv(u, u_scr, cw_ref, cb_ref, e1_ref, e0_ref):
        um1 = jnp.where(t >= 1, u_scr[7:7 + m, :], e1_ref[...])
        um2 = jnp.where(t >= 2, u_scr[6:6 + m, :], jnp.where(t == 1, e1_ref[...], e0_ref[...]))
        return cb_ref[...] + cw_ref[0:1, :] * um2 + cw_ref[1:2, :] * um1 + cw_ref[2:3, :] * u

    ca = conv(ua, ua_scr, cwa_ref, cba_ref, e1a_ref, e0a_ref)
    cb = conv(ub, ub_scr, cwb_ref, cbb_ref, e1b_ref, e0b_ref)
    gt = _silu_gate(ca, cb).astype(BF16)
    y_ref[...] += jnp.dot(gt, wdn_ref[...], preferred_element_type=F32)


def ffn_sample(x, t_len, prev, g, wup_bf16, cw, cb, wdn_bf16, *, ck=256):
    M, D = x.shape
    d_ff = wdn_bf16.shape[0]
    assert d_ff % ck == 0 and CONV_W == 3 and t_len >= 2
    nc = d_ff // ck
    e1 = jnp.repeat(prev[:, 1], t_len, axis=0)
    e0 = jnp.repeat(prev[:, 0], t_len, axis=0)
    cb2 = cb.reshape(1, 2 * d_ff)
    a_col = lambda c: (0, c)
    b_col = lambda c: (0, nc + c)
    return pl.pallas_call(
        functools.partial(_ffn_sample_kernel, m=M, t_len=t_len),
        out_shape=(jax.ShapeDtypeStruct((M, D), F32), jax.ShapeDtypeStruct((M, d_ff), F32),
                   jax.ShapeDtypeStruct((M, d_ff), F32)),
        grid=(nc,),
        in_specs=[
            pl.BlockSpec((M, D), lambda c: (0, 0)),
            pl.BlockSpec((1, D), lambda c: (0, 0)),
            pl.BlockSpec((D, ck), a_col), pl.BlockSpec((D, ck), b_col),
            pl.BlockSpec((CONV_W, ck), a_col), pl.BlockSpec((CONV_W, ck), b_col),
            pl.BlockSpec((1, ck), a_col), pl.BlockSpec((1, ck), b_col),
            pl.BlockSpec((M, ck), a_col), pl.BlockSpec((M, ck), b_col),
            pl.BlockSpec((M, ck), a_col), pl.BlockSpec((M, ck), b_col),
            pl.BlockSpec((ck, D), lambda c: (c, 0)),
        ],
        out_specs=(pl.BlockSpec((M, D), lambda c: (0, 0)),
                   pl.BlockSpec((M, ck), a_col), pl.BlockSpec((M, ck), a_col)),
        scratch_shapes=[pltpu.VMEM((M, D), BF16), pltpu.VMEM((M + 8, ck), F32), pltpu.VMEM((M + 8, ck), F32)],
        compiler_params=_params(("arbitrary",)),
        name="ffn_sample",
    )(x, g.reshape(1, D), wup_bf16, wup_bf16, cw, cw, cb2, cb2, e1, e1, e0, e0, wdn_bf16)


def _softmax_pv(s, v):
    m = jnp.max(s, axis=-1, keepdims=True)
    e = jnp.exp(s - m)
    l = jnp.sum(e, axis=-1, keepdims=True)
    o = jnp.dot(e.astype(BF16), v, preferred_element_type=F32) / l
    return o, m + jnp.log(l)


def _dil_prompt_kernel(q_ref, kc_ref, kp_ref, vc_ref, vp_ref, bias_ref, o_ref, lse_ref, *, tq):
    i = pl.program_id(3)
    q = q_ref[...]
    k = jnp.concatenate([kp_ref[...], kc_ref[...]], axis=0).astype(BF16)
    v = jnp.concatenate([vp_ref[...], vc_ref[...]], axis=0).astype(BF16)
    lane = lax.broadcasted_iota(jnp.int32, (tq, LANES), 1)
    col = lax.broadcasted_iota(jnp.int32, (tq, tq + N_BACK), 1)
    n_dead = jnp.where(i == 0, N_BACK, 0)
    outs, lses = [], []
    for h in range(HEADS_PER_LANE_TILE):
        hm = (lane >= h * HEAD_DIM) & (lane < (h + 1) * HEAD_DIM)
        qh = jnp.where(hm, q, 0.0).astype(BF16)
        s = lax.dot_general(qh, k, (((1,), (1,)), ((), ())), preferred_element_type=F32) * SCALE + bias_ref[h]
        s = jnp.where(col < n_dead, NEG, s)
        o, lse = _softmax_pv(s, v)
        outs.append(o)
        lses.append(lse)
    first = lane < HEAD_DIM
    o_ref[...] = jnp.where(first, outs[0], outs[1])
    lse_ref[...] = jnp.where(first, lses[0], lses[1])


def dil_prompt(qkv, batch, seq_len, gi, bias_k):
    win, d = DIL_GROUPS[gi]
    assert win // d == N_BACK and seq_len % (d * N_BACK) == 0
    sub = seq_len // d
    tq = min(512, sub)
    assert sub % tq == 0 and tq % N_BACK == 0
    n_col = qkv.shape[1] // LANES
    hp_tiles = HD // LANES
    qkv3 = qkv.reshape(batch, sub, d * qkv.shape[1])
    step = jnp.arange(tq)[:, None] + N_BACK - jnp.arange(tq + N_BACK)[None, :]
    band = (step >= 0) & (step <= N_BACK)
    bias = jnp.where(band[:, :, None], bias_k[jnp.clip(step, 0, N_BACK)].astype(F32), NEG)
    bias = jnp.transpose(bias, (2, 0, 1))

    def col(which):
        return lambda hp, b, r, i: (b, i, r * n_col + (which * N_GROUPS + gi) * hp_tiles + hp)

    def col_prev(which):
        return lambda hp, b, r, i: (b, jnp.maximum(i * (tq // N_BACK) - 1, 0),
                                    r * n_col + (which * N_GROUPS + gi) * hp_tiles + hp)

    cur = lambda which: pl.BlockSpec((None, tq, LANES), col(which))
    prev = lambda which: pl.BlockSpec((None, N_BACK, LANES), col_prev(which))
    out_spec = pl.BlockSpec((None, tq, LANES), lambda hp, b, r, i: (b, i, r * hp_tiles + hp))
    o, lse = pl.pallas_call(
        functools.partial(_dil_prompt_kernel, tq=tq),
        out_shape=(jax.ShapeDtypeStruct((batch, sub, d * HD), F32),) * 2,
        grid=(hp_tiles, batch, d, sub // tq),
        in_specs=[cur(0), cur(1), prev(1), cur(2), prev(2),
                  pl.BlockSpec((HEADS_PER_LANE_TILE, tq, tq + N_BACK), lambda hp, b, r, i: (hp, 0, 0))],
        out_specs=(out_spec, out_spec),
        compiler_params=_params(("parallel",) * 4),
        name=f"dil_prompt_g{gi}",
    )(qkv3, qkv3, qkv3, qkv3, qkv3, bias)
    return o.reshape(batch * seq_len, HD), lse.reshape(batch * seq_len, HD)


def _dil_sample_kernel(cache_ref, new_ref, bias_ref, eh_ref, eht_ref, o_ref, lse_ref, *, gi, d, n_q, t_len):
    r = pl.program_id(1)

    @pl.when(r == 0)
    def _():
        o_ref[...] = jnp.zeros_like(o_ref)
        lse_ref[...] = jnp.zeros_like(lse_ref)

    q_all = new_ref[:, gi * HD:(gi + 1) * HD]
    k_all = jnp.concatenate([cache_ref[:, :HD], new_ref[:, (N_GROUPS + gi) * HD:(N_GROUPS + gi + 1) * HD]], axis=0)
    v_all = jnp.concatenate([cache_ref[:, HD:], new_ref[:, (2 * N_GROUPS + gi) * HD:(2 * N_GROUPS + gi + 1) * HD]],
                            axis=0)
    rows = lax.broadcasted_iota(jnp.int32, (t_len, HD), 0)
    for m in range(n_q):
        t = r + m * d
        q_row = jnp.sum(jnp.where(rows == t, q_all, 0.0), axis=0, keepdims=True)
        s = jnp.dot(k_all * q_row, eh_ref[...], precision=HIGHEST, preferred_element_type=F32) * SCALE + bias_ref[m]
        mx = jnp.max(s, axis=0, keepdims=True)
        e = jnp.exp(s - mx)
        l = jnp.sum(e, axis=0, keepdims=True)
        pe = jnp.dot(e / l, eht_ref[...], precision=HIGHEST, preferred_element_type=F32)
        o_row = jnp.sum(pe * v_all, axis=0, keepdims=True)
        lse8 = jnp.broadcast_to(mx + jnp.log(l), (8, LANES))
        lse_row = jnp.dot(lse8, eht_ref[...], precision=HIGHEST, preferred_element_type=F32)[0:1, :]
        o_ref[...] = jnp.where(rows == t, o_row, o_ref[...])
        lse_ref[...] = jnp.where(rows == t, lse_row, lse_ref[...])


def dil_sample(cache, qkv_s, dec_batch, t_len, gi, bias_k, eh, eht):
    win, d = DIL_GROUPS[gi]
    L = cache.shape[1]
    assert L == win and win // d == N_BACK, "cached window must hold exactly the group's window"
    n_r = min(d, t_len)
    n_q = -(-t_len // d)
    assert n_r * n_q == t_len
    n_keys = N_BACK + t_len
    rr = jnp.arange(n_r)[:, None, None]
    mm = jnp.arange(n_q)[None, :, None]
    kk = jnp.arange(n_keys)[None, None, :]
    t_q = rr + mm * d
    t2 = kk - N_BACK
    step_cache = N_BACK + mm - kk
    ok_cache = (kk < N_BACK) & (kk >= mm)
    ok_new = (kk >= N_BACK) & (t2 <= t_q) & ((t_q - t2) % d == 0)
    step = jnp.where(kk < N_BACK, step_cache, (t_q - t2) // d)
    ok = ok_cache | ok_new
    bias = jnp.where(ok[..., None], bias_k[jnp.clip(step, 0, N_BACK)].astype(F32), NEG)
    bias = jnp.pad(bias, ((0, 0), (0, 0), (0, 0), (0, LANES - N_HEADS)))
    cache3 = cache.reshape(dec_batch, N_BACK, d * 2 * HD)
    new3 = qkv_s.reshape(dec_batch, t_len, qkv_s.shape[1])
    out_spec = pl.BlockSpec((None, t_len, HD), lambda b, r: (b, 0, 0))
    o, lse = pl.pallas_call(
        functools.partial(_dil_sample_kernel, gi=gi, d=d, n_q=n_q, t_len=t_len),
        out_shape=(jax.ShapeDtypeStruct((dec_batch, t_len, HD), F32),) * 2,
        grid=(dec_batch, n_r),
        in_specs=[
            pl.BlockSpec((None, N_BACK, 2 * HD), lambda b, r: (b, 0, r)),
            pl.BlockSpec((None, t_len, new3.shape[2]), lambda b, r: (b, 0, 0)),
            pl.BlockSpec((None, n_q, n_keys, LANES), lambda b, r: (r, 0, 0, 0)),
            pl.BlockSpec(eh.shape, lambda b, r: (0, 0)),
            pl.BlockSpec(eht.shape, lambda b, r: (0, 0)),
        ],
        out_specs=(out_spec, out_spec),
        compiler_params=_params(("parallel", "arbitrary")),
        name=f"dil_sample_g{gi}",
    )(cache3, new3, bias, eh, eht)
    return o.reshape(dec_batch * t_len, HD), lse.reshape(dec_batch * t_len, HD)


def _cache_shift_kernel(buf_ref, new_ref, out_ref, sem, *, gi, t_len, L):
    keep = L - t_len
    copies = [
        pltpu.make_async_copy(buf_ref.at[:, pl.ds(t_len, keep), :], out_ref.at[:, pl.ds(0, keep), :], sem.at[0]),
        pltpu.make_async_copy(new_ref.at[:, :, pl.ds((N_GROUPS + gi) * HD, HD)],
                              out_ref.at[:, pl.ds(keep, t_len), pl.ds(0, HD)], sem.at[1]),
        pltpu.make_async_copy(new_ref.at[:, :, pl.ds((2 * N_GROUPS + gi) * HD, HD)],
                              out_ref.at[:, pl.ds(keep, t_len), pl.ds(HD, HD)], sem.at[2]),
    ]
    for cp in copies:
        cp.start()
    for cp in copies:
        cp.wait()


def cache_shift(cache, qkv_s, dec_batch, t_len, gi):
    L = cache.shape[1]
    assert L > t_len
    buf = cache.reshape(dec_batch, L, 2 * HD)
    new3 = qkv_s.reshape(dec_batch, t_len, qkv_s.shape[1])
    out = pl.pallas_call(
        functools.partial(_cache_shift_kernel, gi=gi, t_len=t_len, L=L),
        out_shape=jax.ShapeDtypeStruct(buf.shape, F32),
        in_specs=[pl.BlockSpec(memory_space=pl.ANY), pl.BlockSpec(memory_space=pl.ANY)],
        out_specs=pl.BlockSpec(memory_space=pl.ANY),
        scratch_shapes=[pltpu.SemaphoreType.DMA((3,))],
        name=f"cache_shift_g{gi}",
    )(buf, new3)
    return out.reshape(cache.shape)


def _block_mean_kernel(k_ref, o_ref):
    o_ref[...] = jnp.broadcast_to(jnp.sum(k_ref[...], axis=0, keepdims=True) * (1.0 / MOBA_BLOCK), o_ref.shape)


def block_means(kvb, n_blocks_total):
    out = pl.pallas_call(
        _block_mean_kernel,
        out_shape=jax.ShapeDtypeStruct((n_blocks_total, 8, HD), F32),
        grid=(n_blocks_total,),
        in_specs=[pl.BlockSpec((MOBA_BLOCK, HD), lambda i: (i, 0))],
        out_specs=pl.BlockSpec((None, 8, HD), lambda i: (i, 0, 0)),
        compiler_params=_params(("parallel",)),
        name="block_means",
    )(kvb)
    return out[:, 0]


def _top_blocks(sc, blk, n_valid, n_sel, axis):
    sel = jnp.zeros(sc.shape, F32)
    big = float(sc.shape[axis])
    for it in range(n_sel):
        mx = jnp.max(sc, axis=axis, keepdims=True)
        idx = jnp.min(jnp.where(sc == mx, blk, big), axis=axis, keepdims=True)
        hit = blk == idx
        sel = jnp.maximum(sel, jnp.where(hit, jnp.where(it < n_valid, 1.0, 0.0), 0.0))
        sc = jnp.where(hit, -jnp.inf, sc)
    return sel


def _moba_prompt_kernel(q_ref, k_ref, v_ref, km_ref, bias_ref, o_ref, m_scr, l_scr, acc_scr, sel_scr, *, nb, n_sel):
    i = pl.program_id(2)
    j = pl.program_id(3)
    tb = MOBA_BLOCK
    lane = lax.broadcasted_iota(jnp.int32, (tb, LANES), 1)
    blk = lax.broadcasted_iota(jnp.int32, (tb, nb), 1).astype(F32)
    head_masks = [(lane >= h * HEAD_DIM) & (lane < (h + 1) * HEAD_DIM) for h in range(HEADS_PER_LANE_TILE)]

    @pl.when(j == 0)
    def _():
        m_scr[...] = jnp.full_like(m_scr, NEG)
        l_scr[...] = jnp.zeros_like(l_scr)
        acc_scr[...] = jnp.zeros_like(acc_scr)
        q = q_ref[...]
        km = km_ref[...]
        i_f = i.astype(F32)
        for h, hm in enumerate(head_masks):
            qh = jnp.where(hm, q, 0.0)
            sc = lax.dot_general(qh, km, (((1,), (1,)), ((), ())), precision=HIGHEST, preferred_element_type=F32)
            sc = jnp.where(blk < i_f, sc, -jnp.inf)
            sel_scr[h] = _top_blocks(sc, blk, i, n_sel, axis=1)

    @pl.when(j <= i)
    def _():
        q = q_ref[...]
        k = k_ref[...].astype(BF16)
        v = v_ref[...].astype(BF16)
        delta = i - j
        row = lax.broadcasted_iota(jnp.int32, (tb, tb), 0)
        colm = lax.broadcasted_iota(jnp.int32, (tb, tb), 1)
        causal = (row - colm + jnp.where(delta == 0, 0, tb)) >= 0
        own = jnp.where(delta == 0, 1.0, 0.0)
        j_f = j.astype(F32)
        for h, hm in enumerate(head_masks):
            qh = jnp.where(hm, q, 0.0).astype(BF16)
            s = lax.dot_general(qh, k, (((1,), (1,)), ((), ())), preferred_element_type=F32) * SCALE
            s = s + bias_ref[h, delta]
            picked = jnp.sum(jnp.where(blk == j_f, sel_scr[h], 0.0), axis=1, keepdims=True) + own
            ok = causal & (picked > 0.5)
            s = jnp.where(ok, s, NEG)
            m_old = m_scr[h]
            m_new = jnp.maximum(m_old, jnp.max(s, axis=1, keepdims=True))
            e = jnp.where(ok, jnp.exp(s - m_new), 0.0)
            alpha = jnp.exp(m_old - m_new)
            l_scr[h] = alpha * l_scr[h] + jnp.sum(e, axis=1, keepdims=True)
            acc_scr[h] = alpha * acc_scr[h] + jnp.dot(e.astype(BF16), v, preferred_element_type=F32)
            m_scr[h] = m_new

    @pl.when(j == i)
    def _():
        o_ref[...] = jnp.where(head_masks[0], acc_scr[0] / l_scr[0], acc_scr[1] / l_scr[1])


def moba_prompt(q, kvb, k_means, tab, batch, seq_len):
    tb = MOBA_BLOCK
    assert seq_len % tb == 0
    nb = seq_len // tb
    n_sel = min(MOBA_TOPK, nb)
    hp_tiles = HD // LANES
    dist = jnp.arange(nb)[:, None, None] * tb + jnp.arange(tb)[None, :, None] - jnp.arange(tb)[None, None, :]
    bias = tab.T.astype(F32)[:, jnp.maximum(dist, 0)]
    km3 = k_means.reshape(batch, nb, HD)
    return pl.pallas_call(
        functools.partial(_moba_prompt_kernel, nb=nb, n_sel=n_sel),
        out_shape=jax.ShapeDtypeStruct(q.shape, F32),
        grid=(hp_tiles, batch, nb, nb),
        in_specs=[
            pl.BlockSpec((tb, LANES), lambda hp, b, i, j: (b * nb + i, hp)),
            pl.BlockSpec((tb, LANES), lambda hp, b, i, j: (b * nb + jnp.minimum(j, i), hp)),
            pl.BlockSpec((tb, LANES), lambda hp, b, i, j: (b * nb + jnp.minimum(j, i), hp_tiles + hp)),
            pl.BlockSpec((None, nb, LANES), lambda hp, b, i, j: (b, 0, hp)),
            pl.BlockSpec((HEADS_PER_LANE_TILE, nb, tb, tb), lambda hp, b, i, j: (hp, 0, 0, 0),
                         pipeline_mode=pl.Buffered(1)),
        ],
        out_specs=pl.BlockSpec((tb, LANES), lambda hp, b, i, j: (b * nb + i, hp)),
        scratch_shapes=[pltpu.VMEM((HEADS_PER_LANE_TILE, tb, 1), F32), pltpu.VMEM((HEADS_PER_LANE_TILE, tb, 1), F32),
                        pltpu.VMEM((HEADS_PER_LANE_TILE, tb, LANES), F32),
                        pltpu.VMEM((HEADS_PER_LANE_TILE, tb, nb), F32)],
        compiler_params=_params(("parallel", "parallel", "parallel", "arbitrary")),
        name="moba_prompt",
    )(q, kvb, kvb, km3, bias)


def _moba_page_kernel(pt_ref, pool_ref, qbd_ref, bias_ref, dmask_ref, ksum_ref, m_ref, l_ref, acc_ref, *, t_len):
    del pt_ref
    kv = pool_ref[...]
    k = kv[:, :HD]
    v = kv[:, HD:].astype(BF16)
    ksum_ref[...] = jnp.broadcast_to(jnp.sum(k, axis=0, keepdims=True), ksum_ref.shape)
    s = jnp.dot(k.astype(BF16), qbd_ref[...].astype(BF16), preferred_element_type=F32) * SCALE + bias_ref[...]
    m = jnp.max(s, axis=0, keepdims=True)
    e = jnp.exp(s - m)
    m_ref[...] = jnp.broadcast_to(m, m_ref.shape)
    l_ref[...] = jnp.broadcast_to(jnp.sum(e, axis=0, keepdims=True), l_ref.shape)
    pv = jnp.dot(e.T.astype(BF16), v, preferred_element_type=F32)
    dmask = dmask_ref[...]
    rows = [jnp.sum(pv[t * N_HEADS:(t + 1) * N_HEADS, :] * dmask, axis=0, keepdims=True) for t in range(t_len)]
    acc_ref[...] = jnp.concatenate(rows, axis=0)


def _moba_combine_kernel(ks_ref, m_ref, l_ref, acc_ref, qbd_ref, new_ref, bias_ref, ex_ref, o_ref,
                         *, n_pages, ppb, n_blocks, n_sel, t_len):
    qbd = qbd_ref[...]
    ks = ks_ref[...]
    kmean = functools.reduce(lambda a, b: a + b, [ks[:, p * HD:(p + 1) * HD] for p in range(ppb)]) * (1.0 / MOBA_BLOCK)
    sc = jnp.dot(kmean, qbd, precision=HIGHEST, preferred_element_type=F32)
    n_col = sc.shape[1]
    blk = lax.broadcasted_iota(jnp.int32, (n_blocks, n_col), 0).astype(F32)
    page_blk = (lax.broadcasted_iota(jnp.int32, (n_pages, n_col), 0) // ppb).astype(F32)
    sel = jnp.zeros((n_pages, n_col), F32)
    for _ in range(n_sel):
        mx = jnp.max(sc, axis=0, keepdims=True)
        idx = jnp.min(jnp.where(sc == mx, blk, float(n_blocks)), axis=0, keepdims=True)
        sel = jnp.where(page_blk == idx, 1.0, sel)
        sc = jnp.where(blk == idx, -jnp.inf, sc)
    picked = sel > 0.5
    m_p = m_ref[...]
    knew = new_ref[:, :HD]
    vnew = new_ref[:, HD:]
    s_own = jnp.dot(knew.astype(BF16), qbd.astype(BF16), preferred_element_type=F32) * SCALE + bias_ref[...]
    m_tot = jnp.maximum(jnp.max(jnp.where(picked, m_p, NEG), axis=0, keepdims=True),
                        jnp.max(s_own, axis=0, keepdims=True))
    w = jnp.where(picked, jnp.exp(m_p - m_tot), 0.0)
    e_own = jnp.exp(s_own - m_tot)
    l_tot = jnp.sum(w * l_ref[...], axis=0, keepdims=True) + jnp.sum(e_own, axis=0, keepdims=True)
    l8 = jnp.broadcast_to(l_tot, (8, n_col))
    rows = []
    for t in range(t_len):
        ex = ex_ref[t]
        w_e = jnp.dot(w, ex, precision=HIGHEST, preferred_element_type=F32)
        num = jnp.sum(w_e * acc_ref[t], axis=0, keepdims=True)
        own_e = jnp.dot(e_own, ex, precision=HIGHEST, preferred_element_type=F32)
        num = num + jnp.sum(own_e * vnew, axis=0, keepdims=True)
        den = jnp.dot(l8, ex, precision=HIGHEST, preferred_element_type=F32)[0:1, :]
        rows.append(num / den)
    o_ref[...] = jnp.concatenate(rows, axis=0)


def moba_sample(q_s, kvb_s, pool, page_table, tab, dec_batch, t_len):
    n_pool, page = pool.shape[0], pool.shape[1]
    n_pages = page_table.shape[1]
    past = n_pages * page
    assert MOBA_BLOCK % page == 0 and past % MOBA_BLOCK == 0 and t_len <= MOBA_BLOCK and t_len == 8
    ppb = MOBA_BLOCK // page
    n_blocks = past // MOBA_BLOCK
    n_sel = min(MOBA_TOPK, n_blocks + 1)
    assert n_blocks >= n_sel, "fewer cached blocks than top-k picks is not supported"
    n_col = t_len * N_HEADS
    pool3 = pool.reshape(n_pool, page, 2 * HD)
    q4 = q_s.reshape(dec_batch, t_len, N_HEADS, HEAD_DIM)
    qbd = jnp.einsum('bthx,hg->bhxtg', q4, jnp.eye(N_HEADS, dtype=F32)).reshape(dec_batch, HD, n_col)
    dist = past + jnp.arange(t_len)[None, None, :] - (jnp.arange(n_pages)[:, None, None] * page
                                                      + jnp.arange(page)[None, :, None])
    bias_pages = tab.astype(F32)[dist].reshape(n_pages, page, n_col)
    d_own = jnp.arange(t_len)[None, :] - jnp.arange(t_len)[:, None]
    bias_own = jnp.where((d_own >= 0)[:, :, None], tab.astype(F32)[jnp.maximum(d_own, 0)], NEG).reshape(t_len, n_col)
    dmask = (jnp.arange(N_HEADS)[:, None] == (jnp.arange(HD) // HEAD_DIM)[None, :]).astype(F32)
    col_t = jnp.arange(n_col) // N_HEADS
    col_h = jnp.arange(n_col) % N_HEADS
    expand = ((col_t[None, :, None] == jnp.arange(t_len)[:, None, None])
              & (col_h[None, :, None] == (jnp.arange(HD) // HEAD_DIM)[None, None, :])).astype(F32)

    ksum, m_p, l_p, acc = pl.pallas_call(
        functools.partial(_moba_page_kernel, t_len=t_len),
        out_shape=(jax.ShapeDtypeStruct((dec_batch, n_pages, 8, HD), F32),
                   jax.ShapeDtypeStruct((dec_batch, n_pages, 8, n_col), F32),
                   jax.ShapeDtypeStruct((dec_batch, n_pages, 8, n_col), F32),
                   jax.ShapeDtypeStruct((dec_batch, n_pages * t_len, HD), F32)),
        grid_spec=pltpu.PrefetchScalarGridSpec(
            num_scalar_prefetch=1,
            grid=(dec_batch, n_pages),
            in_specs=[
                pl.BlockSpec((None, page, 2 * HD), lambda b, p, pt: (pt[b, p], 0, 0)),
                pl.BlockSpec((None, HD, n_col), lambda b, p, pt: (b, 0, 0)),
                pl.BlockSpec((None, page, n_col), lambda b, p, pt: (p, 0, 0)),
                pl.BlockSpec((N_HEADS, HD), lambda b, p, pt: (0, 0)),
            ],
            out_specs=(pl.BlockSpec((None, None, 8, HD), lambda b, p, pt: (b, p, 0, 0)),
                       pl.BlockSpec((None, None, 8, n_col), lambda b, p, pt: (b, p, 0, 0)),
                       pl.BlockSpec((None, None, 8, n_col), lambda b, p, pt: (b, p, 0, 0)),
                       pl.BlockSpec((None, t_len, HD), lambda b, p, pt: (b, p, 0))),
        ),
        compiler_params=_params(("parallel", "parallel")),
        name="moba_page",
    )(page_table, pool3, qbd, bias_pages, dmask)

    ks = ksum[:, :, 0].reshape(dec_batch, n_blocks, ppb * HD)
    acc = jnp.swapaxes(acc.reshape(dec_batch, n_pages, t_len, HD), 1, 2)
    new3 = kvb_s.reshape(dec_batch, t_len, 2 * HD)
    per_b = lambda shape: pl.BlockSpec((None,) + shape, lambda b: (b,) + (0,) * len(shape))
    out = pl.pallas_call(
        functools.partial(_moba_combine_kernel, n_pages=n_pages, ppb=ppb, n_blocks=n_blocks, n_sel=n_sel,
                          t_len=t_len),
        out_shape=jax.ShapeDtypeStruct((dec_batch, t_len, HD), F32),
        grid=(dec_batch,),
        in_specs=[per_b((n_blocks, ppb * HD)), per_b((n_pages, n_col)), per_b((n_pages, n_col)),
                  per_b((t_len, n_pages, HD)), per_b((HD, n_col)), per_b((t_len, 2 * HD)),
                  pl.BlockSpec((t_len, n_col), lambda b: (0, 0)),
                  pl.BlockSpec((t_len, n_col, HD), lambda b: (0, 0, 0))],
        out_specs=per_b((t_len, HD)),
        compiler_params=_params(("parallel",)),
        name="moba_combine",
    )(ks, m_p[:, :, 0], l_p[:, :, 0], acc, qbd, new3, bias_own, expand)
    return out.reshape(dec_batch * t_len, HD)


def kernel(x_prompt, x_sample, cache_a_kv_g0, cache_a_kv_g1, cache_a_kv_g2, cache_b_kv_pool, page_table, state_ffn_conv, rel_bias, norm_attn, norm_ffn, w_qkv_a, q_norm_a, k_norm_a, w_o_a, norm_kv, w_kv_b, k_norm_b, w_q_b, q_norm_b, w_o_b, w_up, conv_w, conv_b, w_down):
    B, S, D = x_prompt.shape
    DB, T, _ = x_sample.shape
    depth = norm_attn.shape[0]
    n_self = w_qkv_a.shape[0]
    caches = (cache_a_kv_g0, cache_a_kv_g1, cache_a_kv_g2)
    past = page_table.shape[1] * cache_b_kv_pool.shape[1]
    assert D == HD

    group_bias = [rel_bias[_rel_bucket(dil * jnp.arange(win // dil + 1))] for (win, dil) in DIL_GROUPS]
    tab = rel_bias[_rel_bucket(jnp.arange(max(S, past + T)))]
    eht = (jnp.arange(LANES)[:, None] == (jnp.arange(HD) // HEAD_DIM)[None, :]).astype(F32)
    eh = eht.T

    ones = jnp.ones((HD,), F32)
    tile_h = lambda gvec, n: jnp.tile(gvec, n * N_HEADS)

    def self_layer(x, l, prompt):
        hg = jnp.concatenate([tile_h(q_norm_a[l], N_GROUPS), tile_h(k_norm_a[l], N_GROUPS), jnp.tile(ones, N_GROUPS)])
        qkv = norm_matmul(x, norm_attn[l], w_qkv_a[l].astype(BF16), hg, 2 * N_GROUPS * HD)
        outs, lses, bufs = [], [], []
        for gi, (win, dil) in enumerate(DIL_GROUPS):
            if prompt:
                o, lse = dil_prompt(qkv, B, S, gi, group_bias[gi])
                keep = min(win, S)
                kv = qkv.reshape(B, S, 3, N_GROUPS, N_HEADS, HEAD_DIM)[:, S - keep:, 1:, gi]
                bufs.append(kv)
            else:
                o, lse = dil_sample(caches[gi][l], qkv, DB, T, gi, group_bias[gi], eh, eht)
                bufs.append(cache_shift(caches[gi][l], qkv, DB, T, gi))
            outs.append(o)
            lses.append(lse)
        return merge_proj(outs, lses, w_o_a[l].astype(BF16), x), bufs

    def ffn(x, l, prompt):
        wup, wdn = w_up[l].astype(BF16), w_down[l].astype(BF16)
        if prompt:
            tm = 256
            y, ut = ffn_prompt(x, S, norm_ffn[l], wup, conv_w[l], conv_b[l], wdn, tm=tm)
            cs = ut.reshape(B, S // tm, 8, ut.shape[1])[:, -1, 8 - (CONV_W - 1):]
        else:
            y, ua, ub = ffn_sample(x, T, state_ffn_conv[l], norm_ffn[l], wup, conv_w[l], conv_b[l], wdn)
            u = jnp.concatenate([ua, ub], axis=1)
            cs = u.reshape(DB, T, u.shape[1])[:, T - (CONV_W - 1):]
        return y, cs

    def trunk(x, prompt):
        nb_rows = B if prompt else DB
        a_new = [[] for _ in range(N_GROUPS)]
        conv_new = []
        kvb = None
        k_means = None
        for l in range(depth):
            if l < n_self:
                x, bufs = self_layer(x, l, prompt)
                for gi in range(N_GROUPS):
                    a_new[gi].append(bufs[gi])
            else:
                if l == n_self:
                    hg = jnp.concatenate([tile_h(k_norm_b, 1), ones])
                    kvb = norm_matmul(x, norm_kv, w_kv_b.astype(BF16), hg, HD)
                    if prompt:
                        k_means = block_means(kvb, B * S // MOBA_BLOCK)
                lb = l - n_self
                q = norm_matmul(x, norm_attn[l], w_q_b[lb].astype(BF16), tile_h(q_norm_b[lb], 1), HD)
                if prompt:
                    o = moba_prompt(q, kvb, k_means, tab, B, S)
                else:
                    o = moba_sample(q, kvb, cache_b_kv_pool, page_table, tab, DB, T)
                x = merge_proj([o], None, w_o_b[lb].astype(BF16), x)
            x, cs = ffn(x, l, prompt)
            conv_new.append(cs)
        seq = S if prompt else T
        kv_new = kvb.reshape(nb_rows, seq, 2, N_HEADS, HEAD_DIM)
        return x.reshape(nb_rows, seq, D), [jnp.stack(a, 0) for a in a_new], kv_new, jnp.stack(conv_new, 0)

    y_p, a_p, kv_p, conv_p = trunk(x_prompt.reshape(B * S, D), True)
    y_s, a_s, kv_s, conv_s = trunk(x_sample.reshape(DB * T, D), False)
    return (y_p, y_s, a_p[0], a_p[1], a_p[2], a_s[0], a_s[1], a_s[2], kv_p, kv_s, conv_p, conv_s)
```

```python
import functools
import math

import jax
import jax.numpy as jnp
from jax import lax
from jax.experimental import pallas as pl
from jax.experimental.pallas import tpu as pltpu

HEAD_DIM = 64
N_HEADS = 16
HD = N_HEADS * HEAD_DIM
DIL_GROUPS = ((128, 1), (512, 4), (2048, 16))
N_GROUPS = len(DIL_GROUPS)
N_BACK = 128
MOBA_BLOCK = 256
MOBA_TOPK = 3
NUM_BUCKETS = 32
REL_MAX_DIST = 4096
CONV_W = 3
EPS = 1e-6
SCALE = HEAD_DIM ** -0.5
NEG = -1e30
LANES = 128
HEADS_PER_LANE_TILE = LANES // HEAD_DIM
VMEM_LIMIT = 56 * 1024 * 1024

F32 = jnp.float32
BF16 = jnp.bfloat16
HIGHEST = lax.Precision.HIGHEST


def _params(sem, vmem=VMEM_LIMIT):
    return pltpu.CompilerParams(dimension_semantics=sem, vmem_limit_bytes=vmem)


def _const_spec(shape):
    nd = len(shape)
    return pl.BlockSpec(shape, lambda *_: (0,) * nd, pipeline_mode=pl.Buffered(1))


def _rel_bucket(dist):
    n = jnp.maximum(dist, 0)
    exact = NUM_BUCKETS // 2
    nf = jnp.maximum(n, exact).astype(F32)
    large = exact + (jnp.log(nf / exact) / math.log(REL_MAX_DIST / exact) * (NUM_BUCKETS - exact)).astype(jnp.int32)
    return jnp.where(n < exact, n, jnp.minimum(large, NUM_BUCKETS - 1))


def _norm_matmul_kernel(x_ref, g_ref, w_ref, hg_ref, bd_ref, o_ref, h_scr, *, n_norm_tiles):
    j = pl.program_id(1)

    @pl.when(j == 0)
    def _():
        x = x_ref[...]
        r = lax.rsqrt(jnp.mean(x * x, axis=-1, keepdims=True) + EPS)
        h_scr[...] = (x * r * g_ref[...]).astype(BF16)

    y = jnp.dot(h_scr[...], w_ref[...], preferred_element_type=F32)

    @pl.when(j < n_norm_tiles)
    def _():
        ss = jnp.dot((y * y).astype(BF16), bd_ref[...], preferred_element_type=F32)
        o_ref[...] = y * lax.rsqrt(ss * (1.0 / HEAD_DIM) + EPS) * hg_ref[...]

    @pl.when(j >= n_norm_tiles)
    def _():
        o_ref[...] = y


def norm_matmul(x, g, w_bf16, head_gain, n_norm_cols, *, tn=512):
    M, K = x.shape
    N = w_bf16.shape[1]
    tm = min(512, M)
    assert M % tm == 0 and N % tn == 0 and n_norm_cols % tn == 0 and tn % HEAD_DIM == 0
    hid = jnp.arange(tn) // HEAD_DIM
    bd = (hid[:, None] == hid[None, :]).astype(BF16)
    return pl.pallas_call(
        functools.partial(_norm_matmul_kernel, n_norm_tiles=n_norm_cols // tn),
        out_shape=jax.ShapeDtypeStruct((M, N), F32),
        grid=(M // tm, N // tn),
        in_specs=[
            pl.BlockSpec((tm, K), lambda i, j: (i, 0)),
            pl.BlockSpec((1, K), lambda i, j: (0, 0)),
            pl.BlockSpec((K, tn), lambda i, j: (0, j)),
            pl.BlockSpec((1, tn), lambda i, j: (0, j)),
            pl.BlockSpec((tn, tn), lambda i, j: (0, 0)),
        ],
        out_specs=pl.BlockSpec((tm, tn), lambda i, j: (i, j)),
        scratch_shapes=[pltpu.VMEM((tm, K), BF16)],
        compiler_params=_params(("parallel", "arbitrary")),
        name="norm_matmul",
    )(x, g.reshape(1, K), w_bf16, head_gain.reshape(1, N), bd)


def _merge_proj_kernel(*refs, n_g):
    o_refs = refs[:n_g]
    lse_refs = refs[n_g:2 * n_g] if n_g > 1 else ()
    w_ref, x_ref, out_ref = refs[-3:]
    if n_g == 1:
        a = o_refs[0][...]
    else:
        ls = [r[...] for r in lse_refs]
        mx = functools.reduce(jnp.maximum, ls)
        ws = [jnp.exp(l - mx) for l in ls]
        den = functools.reduce(lambda p, q: p + q, ws)
        num = functools.reduce(lambda p, q: p + q, [w * r[...] for w, r in zip(ws, o_refs)])
        a = num / den
    out_ref[...] = x_ref[...] + jnp.dot(a.astype(BF16), w_ref[...], preferred_element_type=F32)


def merge_proj(os_, lses, w_bf16, x):
    M, D = x.shape
    n_g = len(os_)
    tm = min(256, M)
    assert M % tm == 0
    row = pl.BlockSpec((tm, D), lambda i: (i, 0))
    ins = list(os_) + (list(lses) if n_g > 1 else [])
    return pl.pallas_call(
        functools.partial(_merge_proj_kernel, n_g=n_g),
        out_shape=jax.ShapeDtypeStruct((M, D), F32),
        grid=(M // tm,),
        in_specs=[row] * len(ins) + [_const_spec(w_bf16.shape), row],
        out_specs=row,
        compiler_params=_params(("parallel",)),
        name="merge_proj",
    )(*ins, w_bf16, x)


def _silu_gate(ca, cb):
    return (ca / (1.0 + jnp.exp(-ca))) * cb


def _ffn_prompt_kernel(x_ref, xp_ref, g_ref, wup_ref, cw_ref, cb_ref, wdn_ref, y_ref, ut_ref,
                       ua_scr, ub_scr, acc_scr, *, tm, ck, d_ff, tiles_per_seq):
    i = pl.program_id(0)
    x = x_ref[...]
    keep = jnp.where(i % tiles_per_seq == 0, 0.0, 1.0)
    xc = jnp.concatenate([xp_ref[...] * keep, x], axis=0)
    r = lax.rsqrt(jnp.mean(xc * xc, axis=-1, keepdims=True) + EPS)
    h = (xc * r * g_ref[...]).astype(BF16)
    acc_scr[...] = jnp.zeros_like(acc_scr)
    for c in range(d_ff // ck):
        a0, b0 = c * ck, d_ff + c * ck
        ua_scr[...] = jnp.dot(h, wup_ref[:, a0:a0 + ck], preferred_element_type=F32)
        ub_scr[...] = jnp.dot(h, wup_ref[:, b0:b0 + ck], preferred_element_type=F32)
        ut_ref[:, a0:a0 + ck] = ua_scr[tm:tm + 8, :]
        ut_ref[:, b0:b0 + ck] = ub_scr[tm:tm + 8, :]
        ca = cb_ref[:, a0:a0 + ck]
        cb = cb_ref[:, b0:b0 + ck]
        for j in range(CONV_W):
            ca = ca + cw_ref[j:j + 1, a0:a0 + ck] * ua_scr[6 + j:6 + j + tm, :]
            cb = cb + cw_ref[j:j + 1, b0:b0 + ck] * ub_scr[6 + j:6 + j + tm, :]
        gt = _silu_gate(ca, cb).astype(BF16)
        acc_scr[...] += jnp.dot(gt, wdn_ref[a0:a0 + ck, :], preferred_element_type=F32)
    y_ref[...] = x + acc_scr[...]


def ffn_prompt(x, seq_len, g, wup_bf16, cw, cb, wdn_bf16, *, tm=256, ck=256):
    M, D = x.shape
    d_ff = wdn_bf16.shape[0]
    assert seq_len % tm == 0 and M % seq_len == 0 and d_ff % ck == 0 and CONV_W - 1 <= 8
    n_tiles = M // tm
    y, ut = pl.pallas_call(
        functools.partial(_ffn_prompt_kernel, tm=tm, ck=ck, d_ff=d_ff, tiles_per_seq=seq_len // tm),
        out_shape=(jax.ShapeDtypeStruct((M, D), F32), jax.ShapeDtypeStruct((n_tiles * 8, 2 * d_ff), F32)),
        grid=(n_tiles,),
        in_specs=[
            pl.BlockSpec((tm, D), lambda i: (i, 0)),
            pl.BlockSpec((8, D), lambda i: (jnp.maximum(i * (tm // 8) - 1, 0), 0)),
            _const_spec((1, D)),
            _const_spec(wup_bf16.shape),
            _const_spec(cw.shape),
            _const_spec((1, 2 * d_ff)),
            _const_spec(wdn_bf16.shape),
        ],
        out_specs=(pl.BlockSpec((tm, D), lambda i: (i, 0)), pl.BlockSpec((8, 2 * d_ff), lambda i: (i, 0))),
        scratch_shapes=[pltpu.VMEM((tm + 8, ck), F32), pltpu.VMEM((tm + 8, ck), F32), pltpu.VMEM((tm, D), F32)],
        compiler_params=_params(("parallel",)),
        name="ffn_prompt",
    )(x, x, g.reshape(1, D), wup_bf16, cw, cb.reshape(1, 2 * d_ff), wdn_bf16)
    return y, ut


def _ffn_sample_kernel(x_ref, g_ref, wa_ref, wb_ref, cwa_ref, cwb_ref, cba_ref, cbb_ref,
                       e1a_ref, e1b_ref, e0a_ref, e0b_ref, wdn_ref, y_ref, ua_ref, ub_ref,
                       h_scr, ua_scr, ub_scr, *, m, t_len):
    c = pl.program_id(0)

    @pl.when(c == 0)
    def _():
        x = x_ref[...]
        r = lax.rsqrt(jnp.mean(x * x, axis=-1, keepdims=True) + EPS)
        h_scr[...] = (x * r * g_ref[...]).astype(BF16)
        y_ref[...] = x
        ua_scr[0:8, :] = jnp.zeros((8, ua_scr.shape[1]), F32)
        ub_scr[0:8, :] = jnp.zeros((8, ub_scr.shape[1]), F32)

    h = h_scr[...]
    ua = jnp.dot(h, wa_ref[...], preferred_element_type=F32)
    ub = jnp.dot(h, wb_ref[...], preferred_element_type=F32)
    ua_ref[...] = ua
    ub_ref[...] = ub
    ua_scr[8:8 + m, :] = ua
    ub_scr[8:8 + m, :] = ub
    t = lax.broadcasted_iota(jnp.int32, ua.shape, 0) % t_len

    def conv(u, u_scr, cw_ref, cb_ref, e1_ref, e0_ref):
        um1 = jnp.where(t >= 1, u_scr[7:7 + m, :], e1_ref[...])
        um2 = jnp.where(t >= 2, u_scr[6:6 + m, :], jnp.where(t == 1, e1_ref[...], e0_ref[...]))
        return cb_ref[...] + cw_ref[0:1, :] * um2 + cw_ref[1:2, :] * um1 + cw_ref[2:3, :] * u

    ca = conv(ua, ua_scr, cwa_ref, cba_ref, e1a_ref, e0a_ref)
    cb = conv(ub, ub_scr, cwb_ref, cbb_ref, e1b_ref, e0b_ref)
    gt = _silu_gate(ca, cb).astype(BF16)
    y_ref[...] += jnp.dot(gt, wdn_ref[...], preferred_element_type=F32)


def ffn_sample(x, t_len, prev, g, wup_bf16, cw, cb, wdn_bf16, *, ck=256):
    M, D = x.shape
    d_ff = wdn_bf16.shape[0]
    assert d_ff % ck == 0 and CONV_W == 3 and t_len >= 2
    nc = d_ff // ck
    e1 = jnp.repeat(prev[:, 1], t_len, axis=0)
    e0 = jnp.repeat(prev[:, 0], t_len, axis=0)
    cb2 = cb.reshape(1, 2 * d_ff)
    a_col = lambda c: (0, c)
    b_col = lambda c: (0, nc + c)
    return pl.pallas_call(
        functools.partial(_ffn_sample_kernel, m=M, t_len=t_len),
        out_shape=(jax.ShapeDtypeStruct((M, D), F32), jax.ShapeDtypeStruct((M, d_ff), F32),
                   jax.ShapeDtypeStruct((M, d_ff), F32)),
        grid=(nc,),
        in_specs=[
            pl.BlockSpec((M, D), lambda c: (0, 0)),
            pl.BlockSpec((1, D), lambda c: (0, 0)),
            pl.BlockSpec((D, ck), a_col), pl.BlockSpec((D, ck), b_col),
            pl.BlockSpec((CONV_W, ck), a_col), pl.BlockSpec((CONV_W, ck), b_col),
            pl.BlockSpec((1, ck), a_col), pl.BlockSpec((1, ck), b_col),
            pl.BlockSpec((M, ck), a_col), pl.BlockSpec((M, ck), b_col),
            pl.BlockSpec((M, ck), a_col), pl.BlockSpec((M, ck), b_col),
            pl.BlockSpec((ck, D), lambda c: (c, 0)),
        ],
        out_specs=(pl.BlockSpec((M, D), lambda c: (0, 0)),
                   pl.BlockSpec((M, ck), a_col), pl.BlockSpec((M, ck), a_col)),
        scratch_shapes=[pltpu.VMEM((M, D), BF16), pltpu.VMEM((M + 8, ck), F32), pltpu.VMEM((M + 8, ck), F32)],
        compiler_params=_params(("arbitrary",)),
        name="ffn_sample",
    )(x, g.reshape(1, D), wup_bf16, wup_bf16, cw, cw, cb2, cb2, e1, e1, e0, e0, wdn_bf16)


def _softmax_pv(s, v):
    m = jnp.max(s, axis=-1, keepdims=True)
    e = jnp.exp(s - m)
    l = jnp.sum(e, axis=-1, keepdims=True)
    o = jnp.dot(e.astype(BF16), v, preferred_element_type=F32) / l
    return o, m + jnp.log(l)


def _toeplitz(seg_row, rows, lo, width):
    x = jnp.broadcast_to(seg_row, (rows, seg_row.shape[1]))
    return pltpu.roll(x, 0, 1, stride=1, stride_axis=0)[:, lo:lo + width]


def _dil_prompt_kernel(q_ref, kc_ref, kp_ref, vc_ref, vp_ref, seg_ref, o_ref, lse_ref, bias_scr, *, tq, d):
    b = pl.program_id(1)
    i = pl.program_id(2)
    r = pl.program_id(3)

    @pl.when((b == 0) & (i == 0) & (r == 0))
    def _():
        for h in range(HEADS_PER_LANE_TILE):
            bias_scr[h] = _toeplitz(seg_ref[h], tq, tq, tq + N_BACK)

    def rows(ref, n):
        return ref[...] if d == 1 else ref[pl.ds(r, n, stride=d), :]

    q = rows(q_ref, tq) * SCALE
    k = jnp.concatenate([rows(kp_ref, N_BACK), rows(kc_ref, tq)], axis=0).astype(BF16)
    v = jnp.concatenate([rows(vp_ref, N_BACK), rows(vc_ref, tq)], axis=0).astype(BF16)
    lane = lax.broadcasted_iota(jnp.int32, (tq, LANES), 1)
    col = lax.broadcasted_iota(jnp.int32, (tq, tq + N_BACK), 1)
    n_dead = jnp.where(i == 0, N_BACK, 0)
    outs, lses = [], []
    for h in range(HEADS_PER_LANE_TILE):
        hm = (lane >= h * HEAD_DIM) & (lane < (h + 1) * HEAD_DIM)
        qh = jnp.where(hm, q, 0.0).astype(BF16)
        s = lax.dot_general(qh, k, (((1,), (1,)), ((), ())), preferred_element_type=F32) + bias_scr[h]
        s = jnp.where(col < n_dead, NEG, s)
        o, lse = _softmax_pv(s, v)
        outs.append(o)
        lses.append(lse)
    first = lane < HEAD_DIM
    o_val = jnp.where(first, outs[0], outs[1])
    lse_val = jnp.where(first, lses[0], lses[1])
    if d == 1:
        o_ref[...] = o_val
        lse_ref[...] = lse_val
    else:
        o_ref[pl.ds(r, tq, stride=d), :] = o_val
        lse_ref[pl.ds(r, tq, stride=d), :] = lse_val


def dil_prompt(qkv, batch, seq_len, gi, bias_k):
    win, d = DIL_GROUPS[gi]
    assert win // d == N_BACK and seq_len % (d * N_BACK) == 0
    sub = seq_len // d
    tq = min(512 if d < 16 else N_BACK, sub)
    assert sub % tq == 0 and tq % N_BACK == 0
    hp_tiles = HD // LANES
    qkv3 = qkv.reshape(batch, seq_len, qkv.shape[1])
    width = 2 * tq + N_BACK
    step = N_BACK + tq - jnp.arange(width)
    seg = jnp.where(((step >= 0) & (step <= N_BACK))[:, None], bias_k[jnp.clip(step, 0, N_BACK)].astype(F32), NEG)
    seg = jnp.transpose(seg)[:, None, :]

    def col(which):
        return (which * N_GROUPS + gi) * hp_tiles

    cur = lambda which: pl.BlockSpec((None, tq * d, LANES), lambda hp, b, i, r: (b, i, col(which) + hp))
    prev = lambda which: pl.BlockSpec((None, N_BACK * d, LANES),
                                      lambda hp, b, i, r: (b, jnp.maximum(i * (tq // N_BACK) - 1, 0), col(which) + hp))
    out_spec = pl.BlockSpec((None, tq * d, LANES), lambda hp, b, i, r: (b, i, hp))
    o, lse = pl.pallas_call(
        functools.partial(_dil_prompt_kernel, tq=tq, d=d),
        out_shape=(jax.ShapeDtypeStruct((batch, seq_len, HD), F32),) * 2,
        grid=(hp_tiles, batch, sub // tq, d),
        in_specs=[cur(0), cur(1), prev(1), cur(2), prev(2),
                  pl.BlockSpec((HEADS_PER_LANE_TILE, 1, width), lambda hp, b, i, r: (hp, 0, 0))],
        out_specs=(out_spec, out_spec),
        scratch_shapes=[pltpu.VMEM((HEADS_PER_LANE_TILE, tq, tq + N_BACK), F32)],
        compiler_params=_params(("parallel", "arbitrary", "arbitrary", "arbitrary")),
        name=f"dil_prompt_g{gi}",
    )(qkv3, qkv3, qkv3, qkv3, qkv3, seg)
    return o.reshape(batch * seq_len, HD), lse.reshape(batch * seq_len, HD)


def _dil_sample_kernel(cache_ref, new_ref, q_ref, bias_ref, ones_ref, o_ref, lse_ref, *, d, n_q, t_len):
    r = pl.program_id(1)
    n_keys = N_BACK + t_len
    k_all = jnp.concatenate([cache_ref[:, 0], new_ref[:, 0]], axis=0)
    v_all = jnp.concatenate([cache_ref[:, 1], new_ref[:, 1]], axis=0)
    for m in range(n_q):
        t = r + m * d
        q_t = q_ref[t] * SCALE
        prod = (k_all * q_t[None]).reshape(n_keys * N_HEADS, HEAD_DIM)
        s = jnp.dot(prod, ones_ref[...], precision=HIGHEST, preferred_element_type=F32)
        s = s.reshape(n_keys, N_HEADS, LANES) + bias_ref[m]
        mx = jnp.max(s, axis=0, keepdims=True)
        e = jnp.exp(s - mx)
        l = jnp.sum(e, axis=0, keepdims=True)
        p = e / l
        o_ref[t] = jnp.sum(p[:, :, :HEAD_DIM] * v_all, axis=0)
        lse_ref[t] = (mx + jnp.log(l))[0, :, :HEAD_DIM]


def dil_sample(cache, new_kv, q_s, dec_batch, t_len, gi, bias_k):
    win, d = DIL_GROUPS[gi]
    L = cache.shape[1]
    assert L == win and win // d == N_BACK, "cached window must hold exactly the group's window"
    n_r = min(d, t_len)
    n_q = -(-t_len // d)
    assert n_r * n_q == t_len
    n_keys = N_BACK + t_len
    rr = jnp.arange(n_r)[:, None, None]
    mm = jnp.arange(n_q)[None, :, None]
    kk = jnp.arange(n_keys)[None, None, :]
    t_q = rr + mm * d
    t2 = kk - N_BACK
    step_cache = N_BACK + mm - kk
    ok_cache = (kk < N_BACK) & (kk >= mm)
    ok_new = (kk >= N_BACK) & (t2 <= t_q) & ((t_q - t2) % d == 0)
    step = jnp.where(kk < N_BACK, step_cache, (t_q - t2) // d)
    ok = ok_cache | ok_new
    bias = jnp.where(ok[..., None], bias_k[jnp.clip(step, 0, N_BACK)].astype(F32), NEG)
    bias = jnp.broadcast_to(bias[..., None], bias.shape + (LANES,))
    cache6 = cache.reshape(dec_batch, N_BACK, d, 2, N_HEADS, HEAD_DIM)
    ones = jnp.ones((HEAD_DIM, LANES), F32)
    bias_spec = pl.BlockSpec((None, n_q, n_keys, N_HEADS, LANES), lambda b, r: (r, 0, 0, 0, 0),
                             pipeline_mode=pl.Buffered(1) if n_r == 1 else None)
    out_spec = pl.BlockSpec((None, t_len, N_HEADS, HEAD_DIM), lambda b, r: (b, 0, 0, 0))
    o, lse = pl.pallas_call(
        functools.partial(_dil_sample_kernel, d=d, n_q=n_q, t_len=t_len),
        out_shape=(jax.ShapeDtypeStruct((dec_batch, t_len, N_HEADS, HEAD_DIM), F32),) * 2,
        grid=(dec_batch, n_r),
        in_specs=[
            pl.BlockSpec((None, N_BACK, None, 2, N_HEADS, HEAD_DIM), lambda b, r: (b, 0, r, 0, 0, 0)),
            pl.BlockSpec((None, t_len, 2, N_HEADS, HEAD_DIM), lambda b, r: (b, 0, 0, 0, 0)),
            pl.BlockSpec((None, t_len, N_HEADS, HEAD_DIM), lambda b, r: (b, 0, 0, 0)),
            bias_spec,
            pl.BlockSpec(ones.shape, lambda b, r: (0, 0)),
        ],
        out_specs=(out_spec, out_spec),
        compiler_params=_params(("parallel", "arbitrary")),
        name=f"dil_sample_g{gi}",
    )(cache6, new_kv, q_s, bias, ones)
    return o.reshape(dec_batch * t_len, HD), lse.reshape(dec_batch * t_len, HD)


def _cache_shift_kernel(cur_ref, nxt_ref, new_ref, out_ref, *, t_len, rows):
    j = pl.program_id(1)
    last = pl.num_programs(1) - 1
    out_ref[0:rows - t_len] = cur_ref[t_len:rows]

    @pl.when(j < last)
    def _():
        out_ref[rows - t_len:rows] = nxt_ref[...]

    @pl.when(j == last)
    def _():
        out_ref[rows - t_len:rows] = new_ref[...]


def cache_shift(cache, new_kv, dec_batch, t_len):
    L = cache.shape[1]
    rows = min(256, L)
    assert L % rows == 0 and rows % t_len == 0 and rows > t_len
    tail = cache.shape[2:]
    zeros = (0,) * len(tail)
    return pl.pallas_call(
        functools.partial(_cache_shift_kernel, t_len=t_len, rows=rows),
        out_shape=jax.ShapeDtypeStruct(cache.shape, F32),
        grid=(dec_batch, L // rows),
        in_specs=[
            pl.BlockSpec((None, rows) + tail, lambda b, j: (b, j) + zeros),
            pl.BlockSpec((None, t_len) + tail,
                         lambda b, j: (b, jnp.minimum((j + 1) * (rows // t_len), L // t_len - 1)) + zeros),
            pl.BlockSpec((None, t_len) + tail, lambda b, j: (b, 0) + zeros),
        ],
        out_specs=pl.BlockSpec((None, rows) + tail, lambda b, j: (b, j) + zeros),
        compiler_params=_params(("parallel", "arbitrary")),
        name="cache_shift",
    )(cache, cache, new_kv)


def _block_mean_kernel(k_ref, o_ref):
    o_ref[...] = jnp.broadcast_to(jnp.sum(k_ref[...], axis=0, keepdims=True) * (1.0 / MOBA_BLOCK), o_ref.shape)


def block_means(kvb, n_blocks_total):
    out = pl.pallas_call(
        _block_mean_kernel,
        out_shape=jax.ShapeDtypeStruct((n_blocks_total, 8, HD), F32),
        grid=(n_blocks_total,),
        in_specs=[pl.BlockSpec((MOBA_BLOCK, HD), lambda i: (i, 0))],
        out_specs=pl.BlockSpec((None, 8, HD), lambda i: (i, 0, 0)),
        compiler_params=_params(("parallel",)),
        name="block_means",
    )(kvb)
    return out[:, 0]


def _top_blocks(sc, blk, n_valid, n_sel, axis):
    sel = jnp.zeros(sc.shape, F32)
    big = float(sc.shape[axis])
    for it in range(n_sel):
        mx = jnp.max(sc, axis=axis, keepdims=True)
        idx = jnp.min(jnp.where(sc == mx, blk, big), axis=axis, keepdims=True)
        hit = blk == idx
        sel = jnp.maximum(sel, jnp.where(hit, jnp.where(it < n_valid, 1.0, 0.0), 0.0))
        sc = jnp.where(hit, -jnp.inf, sc)
    return sel


def _moba_prompt_kernel(it_ref, kt_ref, fl_ref, q_ref, k_ref, vt_ref, km_ref, seg_ref, o_ref,
                        m_scr, l_scr, acc_scr, selb_scr, bias_scr, *, nb, n_sel, kb):
    b = pl.program_id(1)
    t = pl.program_id(2)
    i = it_ref[t]
    kt = kt_ref[t]
    first = (fl_ref[t] & 1) != 0
    last = (fl_ref[t] & 2) != 0
    tb = MOBA_BLOCK

    def head_mask(shape, axis, h):
        idx = lax.broadcasted_iota(jnp.int32, shape, axis)
        return (idx >= h * HEAD_DIM) & (idx < (h + 1) * HEAD_DIM)

    @pl.when((b == 0) & (t == 0))
    def _():
        row = lax.broadcasted_iota(jnp.int32, (tb, tb), 0)
        colm = lax.broadcasted_iota(jnp.int32, (tb, tb), 1)
        for h in range(HEADS_PER_LANE_TILE):
            def fill(delta, carry, h=h):
                bias_scr[h, delta] = _toeplitz(seg_ref[h, pl.ds(delta, 1), :], tb, tb, tb)
                return carry
            lax.fori_loop(0, nb, fill, 0)
            bias_scr[h, 0] = jnp.where(colm >= row, bias_scr[h, 0], NEG)

    @pl.when(first)
    def _():
        m_scr[...] = jnp.full_like(m_scr, NEG)
        l_scr[...] = jnp.zeros_like(l_scr)
        acc_scr[...] = jnp.zeros_like(acc_scr)
        q = q_ref[...]
        km = km_ref[...]
        blk = lax.broadcasted_iota(jnp.int32, (nb, tb), 0).astype(F32)
        i_f = i.astype(F32)
        for h in range(HEADS_PER_LANE_TILE):
            kmh = jnp.where(head_mask(km.shape, 1, h), km, 0.0)
            sc = lax.dot_general(kmh, q, (((1,), (1,)), ((), ())), precision=HIGHEST, preferred_element_type=F32)
            sc = jnp.where(blk < i_f, sc, -jnp.inf)
            sel = _top_blocks(sc, blk, i, n_sel, axis=0) + jnp.where(blk == i_f, 1.0, 0.0)
            selb_scr[h] = jnp.where(sel > 0.5, 0.0, NEG)

    q = q_ref[...] * SCALE
    k = k_ref[...].astype(BF16)
    vt = vt_ref[...].astype(BF16)
    for h in range(HEADS_PER_LANE_TILE):
        qh = jnp.where(head_mask(q.shape, 1, h), q, 0.0).astype(BF16)
        s = lax.dot_general(k, qh, (((1,), (1,)), ((), ())), preferred_element_type=F32)
        parts = []
        for c in range(kb):
            jj = kt * kb + c
            parts.append(s[c * tb:(c + 1) * tb, :] + bias_scr[h, jnp.maximum(i - jj, 0)]
                         + selb_scr[h, pl.ds(jj, 1), :])
        s = jnp.concatenate(parts, axis=0)
        m_old = m_scr[h]
        m_new = jnp.maximum(m_old, jnp.max(s, axis=0, keepdims=True))
        e = jnp.exp(s - m_new)
        alpha = jnp.exp(m_old - m_new)
        l_scr[h] = alpha * l_scr[h] + jnp.sum(e, axis=0, keepdims=True)
        acc_scr[h] = alpha * acc_scr[h] + jnp.dot(vt, e.astype(BF16), preferred_element_type=F32)
        m_scr[h] = m_new

    @pl.when(last)
    def _():
        o_t = jnp.where(head_mask((LANES, tb), 0, 0), acc_scr[0] / l_scr[0], acc_scr[1] / l_scr[1])
        o_ref[...] = o_t.T


def moba_prompt(q, kvb, k_means, tab, batch, seq_len):
    tb = MOBA_BLOCK
    kb = 4
    assert seq_len % (kb * tb) == 0
    nb = seq_len // tb
    n_sel = min(MOBA_TOPK, nb)
    hp_tiles = HD // LANES
    sched = [(i, kt) for i in range(nb) for kt in [i // kb] + list(range(i // kb))]
    it = jnp.array([s[0] for s in sched], jnp.int32)
    ktile = jnp.array([s[1] for s in sched], jnp.int32)
    flags = jnp.array([(1 if kt == i // kb else 0) | (2 if kt == (i // kb - 1 if i >= kb else 0) else 0)
                       for i, kt in sched], jnp.int32)
    dist = jnp.arange(nb)[:, None] * tb + jnp.arange(2 * tb)[None, :] - tb
    seg = tab.T.astype(F32)[:, jnp.clip(dist, 0, tab.shape[0] - 1)]
    km3 = k_means.reshape(batch, nb, HD)
    v_t = kvb[:, HD:].T
    nkt = nb // kb
    return pl.pallas_call(
        functools.partial(_moba_prompt_kernel, nb=nb, n_sel=n_sel, kb=kb),
        out_shape=jax.ShapeDtypeStruct(q.shape, F32),
        grid_spec=pltpu.PrefetchScalarGridSpec(
            num_scalar_prefetch=3,
            grid=(hp_tiles, batch, len(sched)),
            in_specs=[
                pl.BlockSpec((tb, LANES), lambda hp, b, t, it, kt, fl: (b * nb + it[t], hp)),
                pl.BlockSpec((kb * tb, LANES), lambda hp, b, t, it, kt, fl: (b * nkt + kt[t], hp)),
                pl.BlockSpec((LANES, kb * tb), lambda hp, b, t, it, kt, fl: (hp, b * nkt + kt[t])),
                pl.BlockSpec((None, nb, LANES), lambda hp, b, t, it, kt, fl: (b, 0, hp)),
                pl.BlockSpec((HEADS_PER_LANE_TILE, nb, 2 * tb), lambda hp, b, t, it, kt, fl: (hp, 0, 0)),
            ],
            out_specs=pl.BlockSpec((tb, LANES), lambda hp, b, t, it, kt, fl: (b * nb + it[t], hp)),
            scratch_shapes=[pltpu.VMEM((HEADS_PER_LANE_TILE, 1, tb), F32),
                            pltpu.VMEM((HEADS_PER_LANE_TILE, 1, tb), F32),
                            pltpu.VMEM((HEADS_PER_LANE_TILE, LANES, tb), F32),
                            pltpu.VMEM((HEADS_PER_LANE_TILE, nb, tb), F32),
                            pltpu.VMEM((HEADS_PER_LANE_TILE, nb, tb, tb), F32)],
        ),
        compiler_params=_params(("parallel", "arbitrary", "arbitrary")),
        name="moba_prompt",
    )(it, ktile, flags, q, kvb, v_t, km3, seg)


def _moba_page_kernel(pt_ref, pool_ref, qbd_ref, bias_ref, dmask_ref, ksum_ref, m_ref, l_ref, acc_ref, *, t_len):
    del pt_ref
    kv = pool_ref[...]
    k = kv[:, :HD]
    v = kv[:, HD:].astype(BF16)
    ksum_ref[...] = jnp.broadcast_to(jnp.sum(k, axis=0, keepdims=True), ksum_ref.shape)
    s = jnp.dot(k.astype(BF16), qbd_ref[...].astype(BF16), preferred_element_type=F32) * SCALE + bias_ref[...]
    m = jnp.max(s, axis=0, keepdims=True)
    e = jnp.exp(s - m)
    m_ref[...] = jnp.broadcast_to(m, m_ref.shape)
    l_ref[...] = jnp.broadcast_to(jnp.sum(e, axis=0, keepdims=True), l_ref.shape)
    pv = jnp.dot(e.T.astype(BF16), v, preferred_element_type=F32)
    dmask = dmask_ref[...]
    rows = [jnp.sum(pv[t * N_HEADS:(t + 1) * N_HEADS, :] * dmask, axis=0, keepdims=True) for t in range(t_len)]
    acc_ref[...] = jnp.concatenate(rows, axis=0)


def _moba_combine_kernel(ks_ref, m_ref, l_ref, acc_ref, qbd_ref, new_ref, bias_ref, ex_ref, o_ref,
                         *, n_pages, ppb, n_blocks, n_sel, t_len):
    qbd = qbd_ref[...]
    ks = ks_ref[...]
    kmean = functools.reduce(lambda a, b: a + b, [ks[:, p * HD:(p + 1) * HD] for p in range(ppb)]) * (1.0 / MOBA_BLOCK)
    sc = jnp.dot(kmean, qbd, precision=HIGHEST, preferred_element_type=F32)
    n_col = sc.shape[1]
    blk = lax.broadcasted_iota(jnp.int32, (n_blocks, n_col), 0).astype(F32)
    page_blk = (lax.broadcasted_iota(jnp.int32, (n_pages, n_col), 0) // ppb).astype(F32)
    sel = jnp.zeros((n_pages, n_col), F32)
    for _ in range(n_sel):
        mx = jnp.max(sc, axis=0, keepdims=True)
        idx = jnp.min(jnp.where(sc == mx, blk, float(n_blocks)), axis=0, keepdims=True)
        sel = jnp.where(page_blk == idx, 1.0, sel)
        sc = jnp.where(blk == idx, -jnp.inf, sc)
    picked = sel > 0.5
    m_p = m_ref[...]
    knew = new_ref[:, :HD]
    vnew = new_ref[:, HD:]
    s_own = jnp.dot(knew.astype(BF16), qbd.astype(BF16), preferred_element_type=F32) * SCALE + bias_ref[...]
    m_tot = jnp.maximum(jnp.max(jnp.where(picked, m_p, NEG), axis=0, keepdims=True),
                        jnp.max(s_own, axis=0, keepdims=True))
    w = jnp.where(picked, jnp.exp(m_p - m_tot), 0.0)
    e_own = jnp.exp(s_own - m_tot)
    l_tot = jnp.sum(w * l_ref[...], axis=0, keepdims=True) + jnp.sum(e_own, axis=0, keepdims=True)
    l8 = jnp.broadcast_to(l_tot, (8, n_col))
    rows = []
    for t in range(t_len):
        ex = ex_ref[t]
        w_e = jnp.dot(w, ex, precision=HIGHEST, preferred_element_type=F32)
        num = jnp.sum(w_e * acc_ref[t], axis=0, keepdims=True)
        own_e = jnp.dot(e_own, ex, precision=HIGHEST, preferred_element_type=F32)
        num = num + jnp.sum(own_e * vnew, axis=0, keepdims=True)
        den = jnp.dot(l8, ex, precision=HIGHEST, preferred_element_type=F32)[0:1, :]
        rows.append(num / den)
    o_ref[...] = jnp.concatenate(rows, axis=0)


def moba_sample(q_s, kvb_s, pool, page_table, tab, dec_batch, t_len):
    n_pool, page = pool.shape[0], pool.shape[1]
    n_pages = page_table.shape[1]
    past = n_pages * page
    assert MOBA_BLOCK % page == 0 and past % MOBA_BLOCK == 0 and t_len <= MOBA_BLOCK and t_len == 8
    ppb = MOBA_BLOCK // page
    n_blocks = past // MOBA_BLOCK
    n_sel = min(MOBA_TOPK, n_blocks + 1)
    assert n_blocks >= n_sel, "fewer cached blocks than top-k picks is not supported"
    n_col = t_len * N_HEADS
    pool3 = pool.reshape(n_pool, page, 2 * HD)
    q4 = q_s.reshape(dec_batch, t_len, N_HEADS, HEAD_DIM)
    qbd = jnp.einsum('bthx,hg->bhxtg', q4, jnp.eye(N_HEADS, dtype=F32)).reshape(dec_batch, HD, n_col)
    dist = past + jnp.arange(t_len)[None, None, :] - (jnp.arange(n_pages)[:, None, None] * page
                                                      + jnp.arange(page)[None, :, None])
    bias_pages = tab.astype(F32)[dist].reshape(n_pages, page, n_col)
    d_own = jnp.arange(t_len)[None, :] - jnp.arange(t_len)[:, None]
    bias_own = jnp.where((d_own >= 0)[:, :, None], tab.astype(F32)[jnp.maximum(d_own, 0)], NEG).reshape(t_len, n_col)
    dmask = (jnp.arange(N_HEADS)[:, None] == (jnp.arange(HD) // HEAD_DIM)[None, :]).astype(F32)
    col_t = jnp.arange(n_col) // N_HEADS
    col_h = jnp.arange(n_col) % N_HEADS
    expand = ((col_t[None, :, None] == jnp.arange(t_len)[:, None, None])
              & (col_h[None, :, None] == (jnp.arange(HD) // HEAD_DIM)[None, None, :])).astype(F32)

    ksum, m_p, l_p, acc = pl.pallas_call(
        functools.partial(_moba_page_kernel, t_len=t_len),
        out_shape=(jax.ShapeDtypeStruct((dec_batch, n_pages, 8, HD), F32),
                   jax.ShapeDtypeStruct((dec_batch, n_pages, 8, n_col), F32),
                   jax.ShapeDtypeStruct((dec_batch, n_pages, 8, n_col), F32),
                   jax.ShapeDtypeStruct((dec_batch, n_pages * t_len, HD), F32)),
        grid_spec=pltpu.PrefetchScalarGridSpec(
            num_scalar_prefetch=1,
            grid=(dec_batch, n_pages),
            in_specs=[
                pl.BlockSpec((None, page, 2 * HD), lambda b, p, pt: (pt[b, p], 0, 0)),
                pl.BlockSpec((None, HD, n_col), lambda b, p, pt: (b, 0, 0)),
                pl.BlockSpec((None, page, n_col), lambda b, p, pt: (p, 0, 0)),
                pl.BlockSpec((N_HEADS, HD), lambda b, p, pt: (0, 0)),
            ],
            out_specs=(pl.BlockSpec((None, None, 8, HD), lambda b, p, pt: (b, p, 0, 0)),
                       pl.BlockSpec((None, None, 8, n_col), lambda b, p, pt: (b, p, 0, 0)),
                       pl.BlockSpec((None, None, 8, n_col), lambda b, p, pt: (b, p, 0, 0)),
                       pl.BlockSpec((None, t_len, HD), lambda b, p, pt: (b, p, 0))),
        ),
        compiler_params=_params(("parallel", "parallel")),
        name="moba_page",
    )(page_table, pool3, qbd, bias_pages, dmask)

    ks = ksum[:, :, 0].reshape(dec_batch, n_blocks, ppb * HD)
    acc = jnp.swapaxes(acc.reshape(dec_batch, n_pages, t_len, HD), 1, 2)
    new3 = kvb_s.reshape(dec_batch, t_len, 2 * HD)
    per_b = lambda shape: pl.BlockSpec((None,) + shape, lambda b: (b,) + (0,) * len(shape))
    out = pl.pallas_call(
        functools.partial(_moba_combine_kernel, n_pages=n_pages, ppb=ppb, n_blocks=n_blocks, n_sel=n_sel,
                          t_len=t_len),
        out_shape=jax.ShapeDtypeStruct((dec_batch, t_len, HD), F32),
        grid=(dec_batch,),
        in_specs=[per_b((n_blocks, ppb * HD)), per_b((n_pages, n_col)), per_b((n_pages, n_col)),
                  per_b((t_len, n_pages, HD)), per_b((HD, n_col)), per_b((t_len, 2 * HD)),
                  pl.BlockSpec((t_len, n_col), lambda b: (0, 0)),
                  pl.BlockSpec((t_len, n_col, HD), lambda b: (0, 0, 0))],
        out_specs=per_b((t_len, HD)),
        compiler_params=_params(("parallel",)),
        name="moba_combine",
    )(ks, m_p[:, :, 0], l_p[:, :, 0], acc, qbd, new3, bias_own, expand)
    return out.reshape(dec_batch * t_len, HD)


def kernel(x_prompt, x_sample, cache_a_kv_g0, cache_a_kv_g1, cache_a_kv_g2, cache_b_kv_pool, page_table, state_ffn_conv, rel_bias, norm_attn, norm_ffn, w_qkv_a, q_norm_a, k_norm_a, w_o_a, norm_kv, w_kv_b, k_norm_b, w_q_b, q_norm_b, w_o_b, w_up, conv_w, conv_b, w_down):
    B, S, D = x_prompt.shape
    DB, T, _ = x_sample.shape
    depth = norm_attn.shape[0]
    n_self = w_qkv_a.shape[0]
    caches = (cache_a_kv_g0, cache_a_kv_g1, cache_a_kv_g2)
    past = page_table.shape[1] * cache_b_kv_pool.shape[1]
    assert D == HD

    group_bias = [rel_bias[_rel_bucket(dil * jnp.arange(win // dil + 1))] for (win, dil) in DIL_GROUPS]
    tab = rel_bias[_rel_bucket(jnp.arange(max(S, past + T)))]

    ones =jnp.ones((HD,), F32)
    tile_h = lambda gvec, n: jnp.tile(gvec, n * N_HEADS)

    def self_layer(x, l, prompt):
        hg = jnp.concatenate([tile_h(q_norm_a[l], N_GROUPS), tile_h(k_norm_a[l], N_GROUPS), jnp.tile(ones, N_GROUPS)])
        qkv = norm_matmul(x, norm_attn[l], w_qkv_a[l].astype(BF16), hg, 2 * N_GROUPS * HD)
        outs, lses, bufs = [], [], []
        qkv6 = qkv.reshape(-1, S if prompt else T, 3, N_GROUPS, N_HEADS, HEAD_DIM)
        for gi, (win, dil) in enumerate(DIL_GROUPS):
            if prompt:
                o, lse = dil_prompt(qkv, B, S, gi, group_bias[gi])
                bufs.append(qkv6[:, S - min(win, S):, 1:, gi])
            else:
                new_kv = qkv6[:, :, 1:, gi]
                o, lse = dil_sample(caches[gi][l], new_kv, qkv6[:, :, 0, gi], DB, T, gi, group_bias[gi])
                bufs.append(cache_shift(caches[gi][l], new_kv, DB, T))
            outs.append(o)
            lses.append(lse)
        return merge_proj(outs, lses, w_o_a[l].astype(BF16), x), bufs

    def ffn(x, l, prompt):
        wup, wdn = w_up[l].astype(BF16), w_down[l].astype(BF16)
        if prompt:
            tm = 256
            y, ut = ffn_prompt(x, S, norm_ffn[l], wup, conv_w[l], conv_b[l], wdn, tm=tm)
            cs = ut.reshape(B, S // tm, 8, ut.shape[1])[:, -1, 8 - (CONV_W - 1):]
        else:
            y, ua, ub = ffn_sample(x, T, state_ffn_conv[l], norm_ffn[l], wup, conv_w[l], conv_b[l], wdn)
            u = jnp.concatenate([ua, ub], axis=1)
            cs = u.reshape(DB, T, u.shape[1])[:, T - (CONV_W - 1):]
        return y, cs

    def trunk(x, prompt):
        nb_rows = B if prompt else DB
        a_new = [[] for _ in range(N_GROUPS)]
        conv_new = []
        kvb = None
        k_means = None
        for l in range(depth):
            if l < n_self:
                x, bufs = self_layer(x, l, prompt)
                for gi in range(N_GROUPS):
                    a_new[gi].append(bufs[gi])
            else:
                if l == n_self:
                    hg = jnp.concatenate([tile_h(k_norm_b, 1), ones])
                    kvb = norm_matmul(x, norm_kv, w_kv_b.astype(BF16), hg, HD)
                    if prompt:
                        k_means = block_means(kvb, B * S // MOBA_BLOCK)
                lb = l - n_self
                q = norm_matmul(x, norm_attn[l], w_q_b[lb].astype(BF16), tile_h(q_norm_b[lb], 1), HD)
                if prompt:
                    o = moba_prompt(q, kvb, k_means, tab, B, S)
                else:
                    o = moba_sample(q, kvb, cache_b_kv_pool, page_table, tab, DB, T)
                x = merge_proj([o], None, w_o_b[lb].astype(BF16), x)
            x, cs = ffn(x, l, prompt)
            conv_new.append(cs)
        seq = S if prompt else T
        kv_new = kvb.reshape(nb_rows, seq, 2, N_HEADS, HEAD_DIM)
        return x.reshape(nb_rows, seq, D), [jnp.stack(a, 0) for a in a_new], kv_new, jnp.stack(conv_new, 0)

    y_p, a_p, kv_p, conv_p = trunk(x_prompt.reshape(B * S, D), True)
    y_s, a_s, kv_s, conv_s = trunk(x_sample.reshape(DB * T, D), False)
    return (y_p, y_s, a_p[0], a_p[1], a_p[2], a_s[0], a_s[1], a_s[2], kv_p, kv_s, conv_p, conv_s)
```

```python
import functools
import math

import jax
import jax.numpy as jnp
from jax import lax
from jax.experimental import pallas as pl
from jax.experimental.pallas import tpu as pltpu

HEAD_DIM = 64
N_HEADS = 16
HD = N_HEADS * HEAD_DIM
DIL_GROUPS = ((128, 1), (512, 4), (2048, 16))
N_GROUPS = len(DIL_GROUPS)
N_BACK = 128
MOBA_BLOCK = 256
MOBA_TOPK = 3
NUM_BUCKETS = 32
REL_MAX_DIST = 4096
CONV_W = 3
EPS = 1e-6
SCALE = HEAD_DIM ** -0.5
NEG = -1e30
LANES = 128
HEADS_PER_LANE_TILE = LANES // HEAD_DIM
VMEM_LIMIT = 56 * 1024 * 1024

F32 = jnp.float32
BF16 = jnp.bfloat16
HIGHEST = lax.Precision.HIGHEST


def _params(sem, vmem=VMEM_LIMIT):
    return pltpu.CompilerParams(dimension_semantics=sem, vmem_limit_bytes=vmem)


def _const_spec(shape):
    nd = len(shape)
    return pl.BlockSpec(shape, lambda *_: (0,) * nd, pipeline_mode=pl.Buffered(1))


def _rel_bucket(dist):
    n = jnp.maximum(dist, 0)
    exact = NUM_BUCKETS // 2
    nf = jnp.maximum(n, exact).astype(F32)
    large = exact + (jnp.log(nf / exact) / math.log(REL_MAX_DIST / exact) * (NUM_BUCKETS - exact)).astype(jnp.int32)
    return jnp.where(n < exact, n, jnp.minimum(large, NUM_BUCKETS - 1))


def _norm_matmul_kernel(x_ref, g_ref, w_ref, hg_ref, bd_ref, o_ref, h_scr, *, n_norm_tiles):
    j = pl.program_id(1)

    @pl.when(j == 0)
    def _():
        x = x_ref[...]
        r = lax.rsqrt(jnp.mean(x * x, axis=-1, keepdims=True) + EPS)
        h_scr[...] = (x * r * g_ref[...]).astype(BF16)

    y = jnp.dot(h_scr[...], w_ref[...], preferred_element_type=F32)

    @pl.when(j < n_norm_tiles)
    def _():
        ss = jnp.dot((y * y).astype(BF16), bd_ref[...], preferred_element_type=F32)
        o_ref[...] = y * lax.rsqrt(ss * (1.0 / HEAD_DIM) + EPS) * hg_ref[...]

    @pl.when(j >= n_norm_tiles)
    def _():
        o_ref[...] = y


def norm_matmul(x, g, w_bf16, head_gain, n_norm_cols, *, tn=512):
    M, K = x.shape
    N = w_bf16.shape[1]
    tm = min(512, M)
    assert M % tm == 0 and N % tn == 0 and n_norm_cols % tn == 0 and tn % HEAD_DIM == 0
    hid = jnp.arange(tn) // HEAD_DIM
    bd = (hid[:, None] == hid[None, :]).astype(BF16)
    return pl.pallas_call(
        functools.partial(_norm_matmul_kernel, n_norm_tiles=n_norm_cols // tn),
        out_shape=jax.ShapeDtypeStruct((M, N), F32),
        grid=(M // tm, N // tn),
        in_specs=[
            pl.BlockSpec((tm, K), lambda i, j: (i, 0)),
            pl.BlockSpec((1, K), lambda i, j: (0, 0)),
            pl.BlockSpec((K, tn), lambda i, j: (0, j)),
            pl.BlockSpec((1, tn), lambda i, j: (0, j)),
            pl.BlockSpec((tn, tn), lambda i, j: (0, 0)),
        ],
        out_specs=pl.BlockSpec((tm, tn), lambda i, j: (i, j)),
        scratch_shapes=[pltpu.VMEM((tm, K), BF16)],
        compiler_params=_params(("parallel", "arbitrary")),
        name="norm_matmul",
    )(x, g.reshape(1, K), w_bf16, head_gain.reshape(1, N), bd)


def _merge_proj_kernel(*refs, n_g):
    o_refs = refs[:n_g]
    lse_refs = refs[n_g:2 * n_g] if n_g > 1 else ()
    w_ref, x_ref, out_ref = refs[-3:]
    if n_g == 1:
        a = o_refs[0][...]
    else:
        ls = [r[...] for r in lse_refs]
        mx = functools.reduce(jnp.maximum, ls)
        ws = [jnp.exp(l - mx) for l in ls]
        den = functools.reduce(lambda p, q: p + q, ws)
        num = functools.reduce(lambda p, q: p + q, [w * r[...] for w, r in zip(ws, o_refs)])
        a = num / den
    out_ref[...] = x_ref[...] + jnp.dot(a.astype(BF16), w_ref[...], preferred_element_type=F32)


def merge_proj(os_, lses, w_bf16, x):
    M, D = x.shape
    n_g = len(os_)
    tm = min(256, M)
    assert M % tm == 0
    row = pl.BlockSpec((tm, D), lambda i: (i, 0))
    ins = list(os_) + (list(lses) if n_g > 1 else [])
    return pl.pallas_call(
        functools.partial(_merge_proj_kernel, n_g=n_g),
        out_shape=jax.ShapeDtypeStruct((M, D), F32),
        grid=(M // tm,),
        in_specs=[row] * len(ins) + [_const_spec(w_bf16.shape), row],
        out_specs=row,
        compiler_params=_params(("parallel",)),
        name="merge_proj",
    )(*ins, w_bf16, x)


def _silu_gate(ca, cb):
    return (ca / (1.0 + jnp.exp(-ca))) * cb


def _ffn_prompt_kernel(x_ref, xp_ref, g_ref, wup_ref, cw_ref, cb_ref, wdn_ref, y_ref, ut_ref,
                       ua_scr, ub_scr, acc_scr, *, tm, ck, d_ff, tiles_per_seq):
    i = pl.program_id(0)
    x = x_ref[...]
    keep = jnp.where(i % tiles_per_seq == 0, 0.0, 1.0)
    xc = jnp.concatenate([xp_ref[...] * keep, x], axis=0)
    r = lax.rsqrt(jnp.mean(xc * xc, axis=-1, keepdims=True) + EPS)
    h = (xc * r * g_ref[...]).astype(BF16)
    acc_scr[...] = jnp.zeros_like(acc_scr)
    for c in range(d_ff // ck):
        a0, b0 = c * ck, d_ff + c * ck
        ua_scr[...] = jnp.dot(h, wup_ref[:, a0:a0 + ck], preferred_element_type=F32)
        ub_scr[...] = jnp.dot(h, wup_ref[:, b0:b0 + ck], preferred_element_type=F32)
        ut_ref[:, a0:a0 + ck] = ua_scr[tm:tm + 8, :]
        ut_ref[:, b0:b0 + ck] = ub_scr[tm:tm + 8, :]
        ca = cb_ref[:, a0:a0 + ck]
        cb = cb_ref[:, b0:b0 + ck]
        for j in range(CONV_W):
            ca = ca + cw_ref[j:j + 1, a0:a0 + ck] * ua_scr[6 + j:6 + j + tm, :]
            cb = cb + cw_ref[j:j + 1, b0:b0 + ck] * ub_scr[6 + j:6 + j + tm, :]
        gt = _silu_gate(ca, cb).astype(BF16)
        acc_scr[...] += jnp.dot(gt, wdn_ref[a0:a0 + ck, :], preferred_element_type=F32)
    y_ref[...] = x + acc_scr[...]


def ffn_prompt(x, seq_len, g, wup_bf16, cw, cb, wdn_bf16, *, tm=256, ck=256):
    M, D = x.shape
    d_ff = wdn_bf16.shape[0]
    assert seq_len % tm == 0 and M % seq_len == 0 and d_ff % ck == 0 and CONV_W - 1 <= 8
    n_tiles = M // tm
    y, ut = pl.pallas_call(
        functools.partial(_ffn_prompt_kernel, tm=tm, ck=ck, d_ff=d_ff, tiles_per_seq=seq_len // tm),
        out_shape=(jax.ShapeDtypeStruct((M, D), F32), jax.ShapeDtypeStruct((n_tiles * 8, 2 * d_ff), F32)),
        grid=(n_tiles,),
        in_specs=[
            pl.BlockSpec((tm, D), lambda i: (i, 0)),
            pl.BlockSpec((8, D), lambda i: (jnp.maximum(i * (tm // 8) - 1, 0), 0)),
            _const_spec((1, D)),
            _const_spec(wup_bf16.shape),
            _const_spec(cw.shape),
            _const_spec((1, 2 * d_ff)),
            _const_spec(wdn_bf16.shape),
        ],
        out_specs=(pl.BlockSpec((tm, D), lambda i: (i, 0)), pl.BlockSpec((8, 2 * d_ff), lambda i: (i, 0))),
        scratch_shapes=[pltpu.VMEM((tm + 8, ck), F32), pltpu.VMEM((tm + 8, ck), F32), pltpu.VMEM((tm, D), F32)],
        compiler_params=_params(("parallel",)),
        name="ffn_prompt",
    )(x, x, g.reshape(1, D), wup_bf16, cw, cb.reshape(1, 2 * d_ff), wdn_bf16)
    return y, ut


def _ffn_sample_kernel(x_ref, g_ref, wa_ref, wb_ref, cwa_ref, cwb_ref, cba_ref, cbb_ref,
                       e1a_ref, e1b_ref, e0a_ref, e0b_ref, wdn_ref, y_ref, ua_ref, ub_ref,
                       h_scr, ua_scr, ub_scr, *, m, t_len):
    c = pl.program_id(0)

    @pl.when(c == 0)
    def _():
        x = x_ref[...]
        r = lax.rsqrt(jnp.mean(x * x, axis=-1, keepdims=True) + EPS)
        h_scr[...] = (x * r * g_ref[...]).astype(BF16)
        y_ref[...] = x
        ua_scr[0:8, :] = jnp.zeros((8, ua_scr.shape[1]), F32)
        ub_scr[0:8, :] = jnp.zeros((8, ub_scr.shape[1]), F32)

    h = h_scr[...]
    ua = jnp.dot(h, wa_ref[...], preferred_element_type=F32)
    ub = jnp.dot(h, wb_ref[...], preferred_element_type=F32)
    ua_ref[...] = ua
    ub_ref[...] = ub
    ua_scr[8:8 + m, :] = ua
    ub_scr[8:8 + m, :] = ub
    t = lax.broadcasted_iota(jnp.int32, ua.shape, 0) % t_len

    def conv(u, u_scr, cw_ref, cb_ref, e1_ref, e0_ref):
        um1 = jnp.where(t >= 1, u_scr[7:7 + m, :], e1_ref[...])
        um2 = jnp.where(t >= 2, u_scr[6:6 + m, :], jnp.where(t == 1, e1_ref[...], e0_ref[...]))
        return cb_ref[...] + cw_ref[0:1, :] * um2 + cw_ref[1:2, :] * um1 + cw_ref[2:3, :] * u

    ca = conv(ua, ua_scr, cwa_ref, cba_ref, e1a_ref, e0a_ref)
    cb = conv(ub, ub_scr, cwb_ref, cbb_ref, e1b_ref, e0b_ref)
    gt = _silu_gate(ca, cb).astype(BF16)
    y_ref[...] += jnp.dot(gt, wdn_ref[...], preferred_element_type=F32)


def ffn_sample(x, t_len, prev, g, wup_bf16, cw, cb, wdn_bf16, *, ck=256):
    M, D = x.shape
    d_ff = wdn_bf16.shape[0]
    assert d_ff % ck == 0 and CONV_W == 3 and t_len >= 2
    nc = d_ff // ck
    e1 = jnp.repeat(prev[:, 1], t_len, axis=0)
    e0 = jnp.repeat(prev[:, 0], t_len, axis=0)
    cb2 = cb.reshape(1, 2 * d_ff)
    a_col = lambda c: (0, c)
    b_col = lambda c: (0, nc + c)
    return pl.pallas_call(
        functools.partial(_ffn_sample_kernel, m=M, t_len=t_len),
        out_shape=(jax.ShapeDtypeStruct((M, D), F32), jax.ShapeDtypeStruct((M, d_ff), F32),
                   jax.ShapeDtypeStruct((M, d_ff), F32)),
        grid=(nc,),
        in_specs=[
            pl.BlockSpec((M, D), lambda c: (0, 0)),
            pl.BlockSpec((1, D), lambda c: (0, 0)),
            pl.BlockSpec((D, ck), a_col), pl.BlockSpec((D, ck), b_col),
            pl.BlockSpec((CONV_W, ck), a_col), pl.BlockSpec((CONV_W, ck), b_col),
            pl.BlockSpec((1, ck), a_col), pl.BlockSpec((1, ck), b_col),
            pl.BlockSpec((M, ck), a_col), pl.BlockSpec((M, ck), b_col),
            pl.BlockSpec((M, ck), a_col), pl.BlockSpec((M, ck), b_col),
            pl.BlockSpec((ck, D), lambda c: (c, 0)),
        ],
        out_specs=(pl.BlockSpec((M, D), lambda c: (0, 0)),
                   pl.BlockSpec((M, ck), a_col), pl.BlockSpec((M, ck), a_col)),
        scratch_shapes=[pltpu.VMEM((M, D), BF16), pltpu.VMEM((M + 8, ck), F32), pltpu.VMEM((M + 8, ck), F32)],
        compiler_params=_params(("arbitrary",)),
        name="ffn_sample",
    )(x, g.reshape(1, D), wup_bf16, wup_bf16, cw, cw, cb2, cb2, e1, e1, e0, e0, wdn_bf16)


def _softmax_pv(s, v):
    m = jnp.max(s, axis=-1, keepdims=True)
    e = jnp.exp(s - m)
    l = jnp.sum(e, axis=-1, keepdims=True)
    o = jnp.dot(e.astype(BF16), v, preferred_element_type=F32) / l
    return o, m + jnp.log(l)


def _toeplitz(seg_row, rows, lo, width):
    x = jnp.broadcast_to(seg_row, (rows, seg_row.shape[1]))
    return pltpu.roll(x, 0, 1, stride=1, stride_axis=0)[:, lo:lo + width]


def _dil_prompt_kernel(q_ref, kc_ref, kp_ref, vc_ref, vp_ref, seg_ref, o_ref, lse_ref, bias_scr, *, tq, d):
    b = pl.program_id(1)
    i = pl.program_id(2)
    r = pl.program_id(3)

    @pl.when((b == 0) & (i == 0) & (r == 0))
    def _():
        for h in range(HEADS_PER_LANE_TILE):
            bias_scr[h] = _toeplitz(seg_ref[h], tq, tq, tq + N_BACK)

    def rows(ref, n):
        return ref[...] if d == 1 else ref[pl.ds(r, n, stride=d), :]

    q = rows(q_ref, tq) * SCALE
    k = jnp.concatenate([rows(kp_ref, N_BACK), rows(kc_ref, tq)], axis=0).astype(BF16)
    v = jnp.concatenate([rows(vp_ref, N_BACK), rows(vc_ref, tq)], axis=0).astype(BF16)
    lane = lax.broadcasted_iota(jnp.int32, (tq, LANES), 1)
    col = lax.broadcasted_iota(jnp.int32, (tq, tq + N_BACK), 1)
    n_dead = jnp.where(i == 0, N_BACK, 0)
    outs, lses = [], []
    for h in range(HEADS_PER_LANE_TILE):
        hm = (lane >= h * HEAD_DIM) & (lane < (h + 1) * HEAD_DIM)
        qh = jnp.where(hm, q, 0.0).astype(BF16)
        s = lax.dot_general(qh, k, (((1,), (1,)), ((), ())), preferred_element_type=F32) + bias_scr[h]
        s = jnp.where(col < n_dead, NEG, s)
        o, lse = _softmax_pv(s, v)
        outs.append(o)
        lses.append(lse)
    first = lane < HEAD_DIM
    o_val = jnp.where(first, outs[0], outs[1])
    lse_val = jnp.where(first, lses[0], lses[1])
    if d == 1:
        o_ref[...] = o_val
        lse_ref[...] = lse_val
    else:
        o_ref[pl.ds(r, tq, stride=d), :] = o_val
        lse_ref[pl.ds(r, tq, stride=d), :] = lse_val


def dil_prompt(qkv, batch, seq_len, gi, bias_k):
    win, d = DIL_GROUPS[gi]
    assert win // d == N_BACK and seq_len % (d * N_BACK) == 0
    sub = seq_len // d
    tq = min(512 if d < 16 else N_BACK, sub)
    assert sub % tq == 0 and tq % N_BACK == 0
    hp_tiles = HD // LANES
    qkv3 = qkv.reshape(batch, seq_len, qkv.shape[1])
    width = 2 * tq + N_BACK
    step = N_BACK + tq - jnp.arange(width)
    seg = jnp.where(((step >= 0) & (step <= N_BACK))[:, None], bias_k[jnp.clip(step, 0, N_BACK)].astype(F32), NEG)
    seg = jnp.transpose(seg)[:, None, :]

    def col(which):
        return (which * N_GROUPS + gi) * hp_tiles

    cur = lambda which: pl.BlockSpec((None, tq * d, LANES), lambda hp, b, i, r: (b, i, col(which) + hp))
    prev = lambda which: pl.BlockSpec((None, N_BACK * d, LANES),
                                      lambda hp, b, i, r: (b, jnp.maximum(i * (tq // N_BACK) - 1, 0), col(which) + hp))
    out_spec = pl.BlockSpec((None, tq * d, LANES), lambda hp, b, i, r: (b, i, hp))
    o, lse = pl.pallas_call(
        functools.partial(_dil_prompt_kernel, tq=tq, d=d),
        out_shape=(jax.ShapeDtypeStruct((batch, seq_len, HD), F32),) * 2,
        grid=(hp_tiles, batch, sub // tq, d),
        in_specs=[cur(0), cur(1), prev(1), cur(2), prev(2),
                  pl.BlockSpec((HEADS_PER_LANE_TILE, 1, width), lambda hp, b, i, r: (hp, 0, 0))],
        out_specs=(out_spec, out_spec),
        scratch_shapes=[pltpu.VMEM((HEADS_PER_LANE_TILE, tq, tq + N_BACK), F32)],
        compiler_params=_params(("parallel", "arbitrary", "arbitrary", "arbitrary")),
        name=f"dil_prompt_g{gi}",
    )(qkv3, qkv3, qkv3, qkv3, qkv3, seg)
    return o.reshape(batch * seq_len, HD), lse.reshape(batch * seq_len, HD)


def _dil_sample_kernel(ct_ref, q_ref, newt_ref, bias_c_ref, bias_n_ref, ot_ref, o_ref, lse_ref, *, hb, t_len, L):
    lane = lax.broadcasted_iota(jnp.int32, (HEAD_DIM, LANES), 1)
    nt = (((1,), (1,)), ((), ()))
    for hh in range(hb):
        q = (q_ref[hh] * SCALE).astype(BF16)
        kt, vt = ct_ref[0, hh], ct_ref[1, hh]
        knt, vnt = newt_ref[0, hh], newt_ref[1, hh]
        s_c = jnp.dot(q, kt.astype(BF16), preferred_element_type=F32) + bias_c_ref[hh]
        s_n = jnp.dot(q, knt.astype(BF16), preferred_element_type=F32) + bias_n_ref[hh]
        m = jnp.maximum(jnp.max(s_c, axis=1, keepdims=True), jnp.max(s_n, axis=1, keepdims=True))
        e_c = jnp.exp(s_c - m)
        e_n = jnp.exp(s_n - m)
        l = jnp.sum(e_c, axis=1, keepdims=True) + jnp.sum(e_n, axis=1, keepdims=True)
        pv = (lax.dot_general(e_c.astype(BF16), vt.astype(BF16), nt, preferred_element_type=F32)
              + lax.dot_general(e_n.astype(BF16), vnt.astype(BF16), nt, preferred_element_type=F32))
        o_ref[hh] = pv / l
        lse_ref[hh] = jnp.broadcast_to(m + jnp.log(l), (t_len, HEAD_DIM))
        for kv, (old, new) in enumerate(((kt, knt), (vt, vnt))):
            moved = pltpu.roll(old, L - t_len, 1)
            if L > LANES:
                ot_ref[kv, hh, :, :L - LANES] = moved[:, :L - LANES]
            ot_ref[kv, hh, :, L - LANES:] = jnp.where(lane >= LANES - t_len, new, moved[:, L - LANES:])


def dil_sample(cache, new_kv, q_s, dec_batch, t_len, gi, bias_k):
    win, d = DIL_GROUPS[gi]
    L = cache.shape[1]
    assert L == win and win // d == N_BACK and L % LANES == 0 and t_len <= LANES
    hb = min(N_HEADS, max(1, (4 << 20) // (2 * HEAD_DIM * L * 4)))
    assert N_HEADS % hb == 0
    ct = jnp.transpose(cache, (0, 2, 3, 4, 1))
    newt = jnp.pad(jnp.transpose(new_kv, (0, 2, 3, 4, 1)), ((0, 0),) * 4 + ((LANES - t_len, 0),))
    qh = jnp.transpose(q_s, (0, 2, 1, 3))
    tq = jnp.arange(t_len)[:, None]
    back = L + tq - jnp.arange(L)[None, :]
    ok_c = (back % d == 0) & (back // d <= N_BACK)
    bias_c = jnp.where(ok_c[:, :, None], bias_k[jnp.clip(back // d, 0, N_BACK)].astype(F32), NEG)
    t2 = jnp.arange(LANES)[None, :] - (LANES - t_len)
    ok_n = (t2 >= 0) & (t2 <= tq) & ((tq - t2) % d == 0)
    bias_n = jnp.where(ok_n[:, :, None], bias_k[jnp.clip((tq - t2) // d, 0, N_BACK)].astype(F32), NEG)
    bias_c = jnp.transpose(bias_c, (2, 0, 1))
    bias_n = jnp.transpose(bias_n, (2, 0, 1))
    cache_spec = pl.BlockSpec((None, 2, hb, HEAD_DIM, L), lambda b, j: (b, 0, j, 0, 0))
    row_spec = pl.BlockSpec((None, hb, t_len, HEAD_DIM), lambda b, j: (b, j, 0, 0))
    ot, o, lse = pl.pallas_call(
        functools.partial(_dil_sample_kernel, hb=hb, t_len=t_len, L=L),
        out_shape=(jax.ShapeDtypeStruct(ct.shape, F32),
                   jax.ShapeDtypeStruct(qh.shape, F32), jax.ShapeDtypeStruct(qh.shape, F32)),
        grid=(dec_batch, N_HEADS // hb),
        in_specs=[
            cache_spec,
            row_spec,
            pl.BlockSpec((None, 2, hb, HEAD_DIM, LANES), lambda b, j: (b, 0, j, 0, 0)),
            pl.BlockSpec((hb, t_len, L), lambda b, j: (j, 0, 0)),
            pl.BlockSpec((hb, t_len, LANES), lambda b, j: (j, 0, 0)),
        ],
        out_specs=(cache_spec, row_spec, row_spec),
        compiler_params=_params(("parallel", "parallel")),
        name=f"dil_sample_g{gi}",
    )(ct, qh, newt, bias_c, bias_n)
    to_rows = lambda a: jnp.transpose(a, (0, 2, 1, 3)).reshape(dec_batch * t_len, HD)
    return to_rows(o), to_rows(lse), jnp.transpose(ot, (0, 4, 1, 2, 3))


def _block_mean_kernel(k_ref, o_ref):
    o_ref[...] = jnp.broadcast_to(jnp.sum(k_ref[...], axis=0, keepdims=True) * (1.0 / MOBA_BLOCK), o_ref.shape)


def block_means(kvb, n_blocks_total):
    out = pl.pallas_call(
        _block_mean_kernel,
        out_shape=jax.ShapeDtypeStruct((n_blocks_total, 8, HD), F32),
        grid=(n_blocks_total,),
        in_specs=[pl.BlockSpec((MOBA_BLOCK, HD), lambda i: (i, 0))],
        out_specs=pl.BlockSpec((None, 8, HD), lambda i: (i, 0, 0)),
        compiler_params=_params(("parallel",)),
        name="block_means",
    )(kvb)
    return out[:, 0]


def _top_blocks(sc, blk, n_valid, n_sel, axis):
    sel = jnp.zeros(sc.shape, F32)
    big = float(sc.shape[axis])
    for it in range(n_sel):
        mx = jnp.max(sc, axis=axis, keepdims=True)
        idx = jnp.min(jnp.where(sc == mx, blk, big), axis=axis, keepdims=True)
        hit = blk == idx
        sel = jnp.maximum(sel, jnp.where(hit, jnp.where(it < n_valid, 1.0, 0.0), 0.0))
        sc = jnp.where(hit, -jnp.inf, sc)
    return sel


def _moba_prompt_kernel(it_ref, kt_ref, fl_ref, q_ref, k_ref, vt_ref, km_ref, seg_ref, o_ref,
                        m_scr, l_scr, acc_scr, selb_scr, bias_scr, *, nb, n_sel, kb):
    b = pl.program_id(1)
    t = pl.program_id(2)
    i = it_ref[t]
    kt = kt_ref[t]
    first = (fl_ref[t] & 1) != 0
    last = (fl_ref[t] & 2) != 0
    tb = MOBA_BLOCK

    def head_mask(shape, axis, h):
        idx = lax.broadcasted_iota(jnp.int32, shape, axis)
        return (idx >= h * HEAD_DIM) & (idx < (h + 1) * HEAD_DIM)

    @pl.when((b == 0) & (t == 0))
    def _():
        row = lax.broadcasted_iota(jnp.int32, (tb, tb), 0)
        colm = lax.broadcasted_iota(jnp.int32, (tb, tb), 1)
        for h in range(HEADS_PER_LANE_TILE):
            def fill(delta, carry, h=h):
                bias_scr[h, delta] = _toeplitz(seg_ref[h, pl.ds(delta, 1), :], tb, tb, tb)
                return carry
            lax.fori_loop(0, nb, fill, 0)
            bias_scr[h, 0] = jnp.where(colm >= row, bias_scr[h, 0], NEG)

    @pl.when(first)
    def _():
        m_scr[...] = jnp.full_like(m_scr, NEG)
        l_scr[...] = jnp.zeros_like(l_scr)
        acc_scr[...] = jnp.zeros_like(acc_scr)
        q = q_ref[...]
        km = km_ref[...]
        blk = lax.broadcasted_iota(jnp.int32, (nb, tb), 0).astype(F32)
        i_f = i.astype(F32)
        for h in range(HEADS_PER_LANE_TILE):
            kmh = jnp.where(head_mask(km.shape, 1, h), km, 0.0)
            sc = lax.dot_general(kmh, q, (((1,), (1,)), ((), ())), precision=HIGHEST, preferred_element_type=F32)
            sc = jnp.where(blk < i_f, sc, -jnp.inf)
            sel = _top_blocks(sc, blk, i, n_sel, axis=0) + jnp.where(blk == i_f, 1.0, 0.0)
            selb_scr[h] = jnp.where(sel > 0.5, 0.0, NEG)

    q = q_ref[...] * SCALE
    k = k_ref[...].astype(BF16)
    vt = vt_ref[...].astype(BF16)
    for h in range(HEADS_PER_LANE_TILE):
        qh = jnp.where(head_mask(q.shape, 1, h), q, 0.0).astype(BF16)
        s = lax.dot_general(k, qh, (((1,), (1,)), ((), ())), preferred_element_type=F32)
        parts = []
        for c in range(kb):
            jj = kt * kb + c
            parts.append(s[c * tb:(c + 1) * tb, :] + bias_scr[h, jnp.maximum(i - jj, 0)]
                         + selb_scr[h, pl.ds(jj, 1), :])
        s = jnp.concatenate(parts, axis=0)
        m_old = m_scr[h]
        m_new = jnp.maximum(m_old, jnp.max(s, axis=0, keepdims=True))
        e = jnp.exp(s - m_new)
        alpha = jnp.exp(m_old - m_new)
        l_scr[h] = alpha * l_scr[h] + jnp.sum(e, axis=0, keepdims=True)
        acc_scr[h] = alpha * acc_scr[h] + jnp.dot(vt, e.astype(BF16), preferred_element_type=F32)
        m_scr[h] = m_new

    @pl.when(last)
    def _():
        o_t = jnp.where(head_mask((LANES, tb), 0, 0), acc_scr[0] / l_scr[0], acc_scr[1] / l_scr[1])
        o_ref[...] = o_t.T


def moba_prompt(q, kvb, k_means, tab, batch, seq_len):
    tb = MOBA_BLOCK
    kb = 4
    assert seq_len % (kb * tb) == 0
    nb = seq_len // tb
    n_sel = min(MOBA_TOPK, nb)
    hp_tiles = HD // LANES
    sched = [(i, kt) for i in range(nb) for kt in [i // kb] + list(range(i // kb))]
    it = jnp.array([s[0] for s in sched], jnp.int32)
    ktile = jnp.array([s[1] for s in sched], jnp.int32)
    flags = jnp.array([(1 if kt == i // kb else 0) | (2 if kt == (i // kb - 1 if i >= kb else 0) else 0)
                       for i, kt in sched], jnp.int32)
    dist = jnp.arange(nb)[:, None] * tb + jnp.arange(2 * tb)[None, :] - tb
    seg = tab.T.astype(F32)[:, jnp.clip(dist, 0, tab.shape[0] - 1)]
    km3 = k_means.reshape(batch, nb, HD)
    v_t = kvb[:, HD:].T
    nkt = nb // kb
    return pl.pallas_call(
        functools.partial(_moba_prompt_kernel, nb=nb, n_sel=n_sel, kb=kb),
        out_shape=jax.ShapeDtypeStruct(q.shape, F32),
        grid_spec=pltpu.PrefetchScalarGridSpec(
            num_scalar_prefetch=3,
            grid=(hp_tiles, batch, len(sched)),
            in_specs=[
                pl.BlockSpec((tb, LANES), lambda hp, b, t, it, kt, fl: (b * nb + it[t], hp)),
                pl.BlockSpec((kb * tb, LANES), lambda hp, b, t, it, kt, fl: (b * nkt + kt[t], hp)),
                pl.BlockSpec((LANES, kb * tb), lambda hp, b, t, it, kt, fl: (hp, b * nkt + kt[t])),
                pl.BlockSpec((None, nb, LANES), lambda hp, b, t, it, kt, fl: (b, 0, hp)),
                pl.BlockSpec((HEADS_PER_LANE_TILE, nb, 2 * tb), lambda hp, b, t, it, kt, fl: (hp, 0, 0)),
            ],
            out_specs=pl.BlockSpec((tb, LANES), lambda hp, b, t, it, kt, fl: (b * nb + it[t], hp)),
            scratch_shapes=[pltpu.VMEM((HEADS_PER_LANE_TILE, 1, tb), F32),
                            pltpu.VMEM((HEADS_PER_LANE_TILE, 1, tb), F32),
                            pltpu.VMEM((HEADS_PER_LANE_TILE, LANES, tb), F32),
                            pltpu.VMEM((HEADS_PER_LANE_TILE, nb, tb), F32),
                            pltpu.VMEM((HEADS_PER_LANE_TILE, nb, tb, tb), F32)],
        ),
        compiler_params=_params(("parallel", "arbitrary", "arbitrary")),
        name="moba_prompt",
    )(it, ktile, flags, q, kvb, v_t, km3, seg)


def _moba_page_kernel(pt_ref, pool_ref, qbd_ref, bias_ref, dmask_ref, ksum_ref, m_ref, l_ref, acc_ref, *, t_len):
    del pt_ref
    p = pl.program_id(1)
    kt = pool_ref[0]
    vt = pool_ref[1].astype(BF16)
    s = jnp.dot(qbd_ref[...].astype(BF16), kt.astype(BF16), preferred_element_type=F32) + bias_ref[...]
    m = jnp.max(s, axis=1, keepdims=True)
    e = jnp.exp(s - m)
    l = jnp.sum(e, axis=1, keepdims=True)
    pv = lax.dot_general(e.astype(BF16), vt, (((1,), (1,)), ((), ())), preferred_element_type=F32)
    dmask = dmask_ref[...]
    rows = [jnp.sum(pv[t * N_HEADS:(t + 1) * N_HEADS, :] * dmask, axis=0, keepdims=True) for t in range(t_len)]
    acc_ref[...] = jnp.concatenate(rows, axis=0)

    @pl.when(p == 0)
    def _():
        ksum_ref[...] = jnp.zeros_like(ksum_ref)
        m_ref[...] = jnp.zeros_like(m_ref)
        l_ref[...] = jnp.zeros_like(l_ref)

    def put(ref, col):
        here = lax.broadcasted_iota(jnp.int32, ref.shape, 1) == p
        ref[...] = jnp.where(here, col, ref[...])

    put(ksum_ref, jnp.sum(kt, axis=1, keepdims=True))
    put(m_ref, m)
    put(l_ref, l)


def _moba_combine_kernel(ks_ref, m_ref, l_ref, acc_ref, qbd_ref, knt_ref, vnew_ref, bias_ref, pair_ref, dmask_ref,
                         o_ref, *, n_pages, n_blocks, n_sel, t_len):
    qbd = qbd_ref[...]
    pair = pair_ref[...]
    sc_pages = jnp.dot(qbd, ks_ref[...], precision=HIGHEST, preferred_element_type=F32)
    sc = jnp.dot(sc_pages, pair, precision=HIGHEST, preferred_element_type=F32) * (1.0 / MOBA_BLOCK)
    lane = lax.broadcasted_iota(jnp.int32, sc.shape, 1).astype(F32)
    sc = jnp.where(lane < n_blocks, sc, -jnp.inf)
    sel = _top_blocks(sc, lane, n_sel, n_sel, axis=1)
    picked = lax.dot_general(sel, pair, (((1,), (1,)), ((), ())), preferred_element_type=F32) > 0.5
    m_p = m_ref[...]
    s_own = jnp.dot(qbd.astype(BF16), knt_ref[...].astype(BF16), preferred_element_type=F32) + bias_ref[...]
    m_tot = jnp.maximum(jnp.max(jnp.where(picked, m_p, NEG), axis=1, keepdims=True),
                        jnp.max(s_own, axis=1, keepdims=True))
    w = jnp.where(picked, jnp.exp(m_p - m_tot), 0.0)
    e_own = jnp.exp(s_own - m_tot)
    l_tot = jnp.sum(w * l_ref[...], axis=1, keepdims=True) + jnp.sum(e_own, axis=1, keepdims=True)
    dmask = dmask_ref[...]
    vnew = vnew_ref[...].astype(BF16)
    rows = []
    for t in range(t_len):
        hs = slice(t * N_HEADS, (t + 1) * N_HEADS)
        a = jnp.dot(w[hs, :n_pages], acc_ref[t], precision=HIGHEST, preferred_element_type=F32)
        a = a + jnp.dot(e_own[hs].astype(BF16), vnew, preferred_element_type=F32)
        num = jnp.sum(a * dmask, axis=0, keepdims=True)
        den = jnp.sum(l_tot[hs] * dmask, axis=0, keepdims=True)
        rows.append(num / den)
    o_ref[...] = jnp.concatenate(rows, axis=0)


def moba_sample(q_s, kvb_s, pool, page_table, tab, dec_batch, t_len):
    n_pool, page = pool.shape[0], pool.shape[1]
    n_pages = page_table.shape[1]
    past = n_pages * page
    assert MOBA_BLOCK % page == 0 and past % MOBA_BLOCK == 0 and t_len <= MOBA_BLOCK and t_len == 8
    ppb = MOBA_BLOCK // page
    n_blocks = past // MOBA_BLOCK
    n_sel = min(MOBA_TOPK, n_blocks + 1)
    assert n_blocks >= n_sel, "fewer cached blocks than top-k picks is not supported"
    n_col = t_len * N_HEADS
    assert n_col == LANES and page == LANES and n_pages <= LANES
    pool_t = jnp.transpose(pool, (0, 2, 3, 4, 1)).reshape(n_pool, 2, HD, page)
    q4 = q_s.reshape(dec_batch, t_len, N_HEADS, HEAD_DIM) * SCALE
    qbd = jnp.einsum('bthx,hg->bthgx', q4, jnp.eye(N_HEADS, dtype=F32)).reshape(dec_batch, n_col, HD)
    dist = past + jnp.arange(t_len)[None, :, None] - (jnp.arange(n_pages)[:, None, None] * page
                                                      + jnp.arange(page)[None, None, :])
    bias_pages = jnp.swapaxes(tab.astype(F32)[dist], 2, 3).reshape(n_pages, n_col, page)
    d_own = jnp.arange(t_len)[:, None] - jnp.arange(LANES)[None, :]
    ok_own = (d_own >= 0) & (jnp.arange(LANES)[None, :] < t_len)
    bias_own = jnp.where(ok_own[:, :, None], tab.astype(F32)[jnp.clip(d_own, 0, t_len)], NEG)
    bias_own = jnp.swapaxes(bias_own, 1, 2).reshape(n_col, LANES)
    dmask = (jnp.arange(N_HEADS)[:, None] == (jnp.arange(HD) // HEAD_DIM)[None, :]).astype(F32)
    pair = ((jnp.arange(LANES)[:, None] // ppb == jnp.arange(LANES)[None, :])
            & (jnp.arange(LANES)[:, None] < n_pages)).astype(F32)
    new3 = kvb_s.reshape(dec_batch, t_len, 2 * HD)
    knt = jnp.pad(jnp.swapaxes(new3[:, :, :HD], 1, 2), ((0, 0), (0, 0), (0, LANES - t_len)))
    vnew = jnp.pad(new3[:, :, HD:], ((0, 0), (0, LANES - t_len), (0, 0)))

    stat = lambda rows: pl.BlockSpec((None, rows, LANES), lambda b, p, pt: (b, 0, 0))
    ksum, m_p, l_p, acc = pl.pallas_call(
        functools.partial(_moba_page_kernel, t_len=t_len),
        out_shape=(jax.ShapeDtypeStruct((dec_batch, HD, LANES), F32),
                   jax.ShapeDtypeStruct((dec_batch, n_col, LANES), F32),
                   jax.ShapeDtypeStruct((dec_batch, n_col, LANES), F32),
                   jax.ShapeDtypeStruct((dec_batch, n_pages * t_len, HD), F32)),
        grid_spec=pltpu.PrefetchScalarGridSpec(
            num_scalar_prefetch=1,
            grid=(dec_batch, n_pages),
            in_specs=[
                pl.BlockSpec((None, 2, HD, page), lambda b, p, pt: (pt[b, p], 0, 0, 0)),
                pl.BlockSpec((None, n_col, HD), lambda b, p, pt: (b, 0, 0)),
                pl.BlockSpec((None, n_col, page), lambda b, p, pt: (p, 0, 0)),
                pl.BlockSpec((N_HEADS, HD), lambda b, p, pt: (0, 0)),
            ],
            out_specs=(stat(HD), stat(n_col), stat(n_col),
                       pl.BlockSpec((None, t_len, HD), lambda b, p, pt: (b, p, 0))),
        ),
        compiler_params=_params(("parallel", "arbitrary")),
        name="moba_page",
    )(page_table, pool_t, qbd, bias_pages, dmask)

    acc = jnp.swapaxes(acc.reshape(dec_batch, n_pages, t_len, HD), 1, 2)
    per_b = lambda shape: pl.BlockSpec((None,) + shape, lambda b: (b,) + (0,) * len(shape))
    const = lambda shape: pl.BlockSpec(shape, lambda b: (0,) * len(shape))
    out = pl.pallas_call(
        functools.partial(_moba_combine_kernel, n_pages=n_pages, n_blocks=n_blocks, n_sel=n_sel, t_len=t_len),
        out_shape=jax.ShapeDtypeStruct((dec_batch, t_len, HD), F32),
        grid=(dec_batch,),
        in_specs=[per_b((HD, LANES)), per_b((n_col, LANES)), per_b((n_col, LANES)),
                  per_b((t_len, n_pages, HD)), per_b((n_col, HD)), per_b((HD, LANES)), per_b((LANES, HD)),
                  const((n_col, LANES)), const((LANES, LANES)), const((N_HEADS, HD))],
        out_specs=per_b((t_len, HD)),
        compiler_params=_params(("parallel",)),
        name="moba_combine",
    )(ksum, m_p, l_p, acc, qbd, knt, vnew, bias_own, pair, dmask)
    return out.reshape(dec_batch * t_len, HD)


def kernel(x_prompt, x_sample, cache_a_kv_g0, cache_a_kv_g1, cache_a_kv_g2, cache_b_kv_pool, page_table, state_ffn_conv, rel_bias, norm_attn, norm_ffn, w_qkv_a, q_norm_a, k_norm_a, w_o_a, norm_kv, w_kv_b, k_norm_b, w_q_b, q_norm_b, w_o_b, w_up, conv_w, conv_b, w_down):
    B, S, D = x_prompt.shape
    DB, T, _ = x_sample.shape
    depth = norm_attn.shape[0]
    n_self = w_qkv_a.shape[0]
    caches = (cache_a_kv_g0, cache_a_kv_g1, cache_a_kv_g2)
    past = page_table.shape[1] * cache_b_kv_pool.shape[1]
    assert D == HD

    group_bias = [rel_bias[_rel_bucket(dil * jnp.arange(win // dil + 1))] for (win, dil) in DIL_GROUPS]
    tab = rel_bias[_rel_bucket(jnp.arange(max(S, past + T)))]

    ones =jnp.ones((HD,), F32)
    tile_h = lambda gvec, n: jnp.tile(gvec, n * N_HEADS)

    def self_layer(x, l, prompt):
        hg = jnp.concatenate([tile_h(q_norm_a[l], N_GROUPS), tile_h(k_norm_a[l], N_GROUPS), jnp.tile(ones, N_GROUPS)])
        qkv = norm_matmul(x, norm_attn[l], w_qkv_a[l].astype(BF16), hg, 2 * N_GROUPS * HD)
        outs, lses, bufs = [], [], []
        seq = S if prompt else T
        qkv3 = qkv.reshape(-1, seq, qkv.shape[1])
        for gi, (win, dil) in enumerate(DIL_GROUPS):
            keep = min(win, seq)
            part = lambda which: qkv3[:, seq - keep:, (which * N_GROUPS + gi) * HD:(which * N_GROUPS + gi + 1) * HD
                                      ].reshape(-1, keep, N_HEADS, HEAD_DIM)
            new_kv = jnp.stack([part(1), part(2)], axis=2)
            if prompt:
                o, lse = dil_prompt(qkv, B, S, gi, group_bias[gi])
                bufs.append(new_kv)
            else:
                o, lse, buf = dil_sample(caches[gi][l], new_kv, part(0), DB, T, gi, group_bias[gi])
                bufs.append(buf)
            outs.append(o)
            lses.append(lse)
        return merge_proj(outs, lses, w_o_a[l].astype(BF16), x), bufs

    def ffn(x, l, prompt):
        wup, wdn = w_up[l].astype(BF16), w_down[l].astype(BF16)
        if prompt:
            tm = 256
            y, ut = ffn_prompt(x, S, norm_ffn[l], wup, conv_w[l], conv_b[l], wdn, tm=tm)
            cs = ut.reshape(B, S // tm, 8, ut.shape[1])[:, -1, 8 - (CONV_W - 1):]
        else:
            y, ua, ub = ffn_sample(x, T, state_ffn_conv[l], norm_ffn[l], wup, conv_w[l], conv_b[l], wdn)
            u = jnp.concatenate([ua, ub], axis=1)
            cs = u.reshape(DB, T, u.shape[1])[:, T - (CONV_W - 1):]
        return y, cs

    def trunk(x, prompt):
        nb_rows = B if prompt else DB
        a_new = [[] for _ in range(N_GROUPS)]
        conv_new = []
        kvb = None
        k_means = None
        for l in range(depth):
            if l < n_self:
                x, bufs = self_layer(x, l, prompt)
                for gi in range(N_GROUPS):
                    a_new[gi].append(bufs[gi])
            else:
                if l == n_self:
                    hg = jnp.concatenate([tile_h(k_norm_b, 1), ones])
                    kvb = norm_matmul(x, norm_kv, w_kv_b.astype(BF16), hg, HD)
                    if prompt:
                        k_means = block_means(kvb, B * S // MOBA_BLOCK)
                lb = l - n_self
                q = norm_matmul(x, norm_attn[l], w_q_b[lb].astype(BF16), tile_h(q_norm_b[lb], 1), HD)
                if prompt:
                    o = moba_prompt(q, kvb, k_means, tab, B, S)
                else:
                    o = moba_sample(q, kvb, cache_b_kv_pool, page_table, tab, DB, T)
                x = merge_proj([o], None, w_o_b[lb].astype(BF16), x)
            x, cs = ffn(x, l, prompt)
            conv_new.append(cs)
        seq = S if prompt else T
        kv_new = kvb.reshape(nb_rows, seq, 2, N_HEADS, HEAD_DIM)
        return x.reshape(nb_rows, seq, D), [jnp.stack(a, 0) for a in a_new], kv_new, jnp.stack(conv_new, 0)

    y_p, a_p, kv_p, conv_p = trunk(x_prompt.reshape(B * S, D), True)
    y_s, a_s, kv_s, conv_s = trunk(x_sample.reshape(DB * T, D), False)
    return (y_p, y_s, a_p[0], a_p[1], a_p[2], a_s[0], a_s[1], a_s[2], kv_p, kv_s, conv_p, conv_s)
```

```python
import functools
import math

import jax
import jax.numpy as jnp
from jax import lax
from jax.experimental import pallas as pl
from jax.experimental.pallas import tpu as pltpu

HEAD_DIM = 64
N_HEADS = 16
HD = N_HEADS * HEAD_DIM
DIL_GROUPS = ((128, 1), (512, 4), (2048, 16))
N_GROUPS = len(DIL_GROUPS)
N_BACK = 128
MOBA_BLOCK = 256
MOBA_TOPK = 3
NUM_BUCKETS = 32
REL_MAX_DIST = 4096
CONV_W = 3
EPS = 1e-6
SCALE = HEAD_DIM ** -0.5
NEG = -1e30
LOG2E = 1.4426950408889634
LANES = 128
MXU_TILE = 256
HEADS_PER_LANE_TILE = LANES // HEAD_DIM
VMEM_LIMIT = 56 * 1024 * 1024

F32 = jnp.float32
BF16 = jnp.bfloat16
HIGHEST = lax.Precision.HIGHEST


def _params(sem, vmem=VMEM_LIMIT):
    return pltpu.CompilerParams(dimension_semantics=sem, vmem_limit_bytes=vmem)


def _const_spec(shape):
    nd = len(shape)
    return pl.BlockSpec(shape, lambda *_: (0,) * nd, pipeline_mode=pl.Buffered(1))


def _rel_bucket(dist):
    n = jnp.maximum(dist, 0)
    exact = NUM_BUCKETS // 2
    nf = jnp.maximum(n, exact).astype(F32)
    large = exact + (jnp.log(nf / exact) / math.log(REL_MAX_DIST / exact) * (NUM_BUCKETS - exact)).astype(jnp.int32)
    return jnp.where(n < exact, n, jnp.minimum(large, NUM_BUCKETS - 1))


def _norm_matmul_kernel(x_ref, g_ref, w_ref, hg_ref, bd_ref, o_ref, h_scr, *, n_norm_tiles):
    j = pl.program_id(1)

    @pl.when(j == 0)
    def _():
        x = x_ref[...]
        r = lax.rsqrt(jnp.mean(x * x, axis=-1, keepdims=True) + EPS)
        h_scr[...] = (x * r * g_ref[...]).astype(BF16)

    y = jnp.dot(h_scr[...], w_ref[...], preferred_element_type=F32)

    @pl.when(j < n_norm_tiles)
    def _():
        y2 = (y * y).astype(BF16)
        ss = jnp.concatenate([jnp.dot(y2[:, c:c + MXU_TILE], bd_ref[...], preferred_element_type=F32)
                              for c in range(0, y.shape[1], MXU_TILE)], axis=1)
        o_ref[...] = y * lax.rsqrt(ss * (1.0 / HEAD_DIM) + EPS) * hg_ref[...]

    @pl.when(j >= n_norm_tiles)
    def _():
        o_ref[...] = y


def norm_matmul(x, g, w_bf16, head_gain, n_norm_cols, *, tn=512):
    M, K = x.shape
    N = w_bf16.shape[1]
    tm = min(1024, M)
    assert M % tm == 0 and N % tn == 0 and n_norm_cols % tn == 0 and tn % MXU_TILE == 0
    hid = jnp.arange(MXU_TILE) // HEAD_DIM
    bd = (hid[:, None] == hid[None, :]).astype(BF16)
    return pl.pallas_call(
        functools.partial(_norm_matmul_kernel, n_norm_tiles=n_norm_cols // tn),
        out_shape=jax.ShapeDtypeStruct((M, N), F32),
        grid=(M // tm, N // tn),
        in_specs=[
            pl.BlockSpec((tm, K), lambda i, j: (i, 0)),
            pl.BlockSpec((1, K), lambda i, j: (0, 0)),
            pl.BlockSpec((K, tn), lambda i, j: (0, j)),
            pl.BlockSpec((1, tn), lambda i, j: (0, j)),
            pl.BlockSpec((MXU_TILE, MXU_TILE), lambda i, j: (0, 0)),
        ],
        out_specs=pl.BlockSpec((tm, tn), lambda i, j: (i, j)),
        scratch_shapes=[pltpu.VMEM((tm, K), BF16)],
        compiler_params=_params(("parallel", "arbitrary")),
        name="norm_matmul",
    )(x, g.reshape(1, K), w_bf16, head_gain.reshape(1, N), bd)


def _merge_proj_kernel(*refs, n_g):
    o_refs = refs[:n_g]
    lse_refs = refs[n_g:2 * n_g] if n_g > 1 else ()
    w_ref, x_ref, out_ref = refs[-3:]
    if n_g == 1:
        a = o_refs[0][...]
    else:
        ls = [r[...] for r in lse_refs]
        mx = functools.reduce(jnp.maximum, ls)
        ws = [jnp.exp(l - mx) for l in ls]
        den = functools.reduce(lambda p, q: p + q, ws)
        num = functools.reduce(lambda p, q: p + q, [w * r[...] for w, r in zip(ws, o_refs)])
        a = num / den
    out_ref[...] = x_ref[...] + jnp.dot(a.astype(BF16), w_ref[...], preferred_element_type=F32)


def merge_proj(os_, lses, w_bf16, x):
    M, D = x.shape
    n_g = len(os_)
    tm = min(256, M)
    assert M % tm == 0
    row = pl.BlockSpec((tm, D), lambda i: (i, 0))
    ins = list(os_) + (list(lses) if n_g > 1 else [])
    return pl.pallas_call(
        functools.partial(_merge_proj_kernel, n_g=n_g),
        out_shape=jax.ShapeDtypeStruct((M, D), F32),
        grid=(M // tm,),
        in_specs=[row] * len(ins) + [_const_spec(w_bf16.shape), row],
        out_specs=row,
        compiler_params=_params(("parallel",)),
        name="merge_proj",
    )(*ins, w_bf16, x)


def _silu_gate(ca, cb):
    return (ca / (1.0 + jnp.exp(-ca))) * cb


def _ffn_prompt_kernel(x_ref, xp_ref, g_ref, wup_ref, cw_ref, cb_ref, wdn_ref, y_ref, ut_ref,
                       ua_scr, ub_scr, gate_scr, *, tm, ck, d_ff, tiles_per_seq):
    i = pl.program_id(0)
    x = x_ref[...]
    keep = jnp.where(i % tiles_per_seq == 0, 0.0, 1.0)
    xc = jnp.concatenate([xp_ref[...] * keep, x], axis=0)
    r = lax.rsqrt(jnp.mean(xc * xc, axis=-1, keepdims=True) + EPS)
    h = (xc * r * g_ref[...]).astype(BF16)
    for c in range(d_ff // ck):
        a0, b0 = c * ck, d_ff + c * ck
        ua_scr[...] = jnp.dot(h, wup_ref[:, a0:a0 + ck], preferred_element_type=F32)
        ub_scr[...] = jnp.dot(h, wup_ref[:, b0:b0 + ck], preferred_element_type=F32)
        ut_ref[:, a0:a0 + ck] = ua_scr[tm:tm + 8, :]
        ut_ref[:, b0:b0 + ck] = ub_scr[tm:tm + 8, :]
        ca = cb_ref[:, a0:a0 + ck]
        cb = cb_ref[:, b0:b0 + ck]
        for j in range(CONV_W):
            ca = ca + cw_ref[j:j + 1, a0:a0 + ck] * ua_scr[6 + j:6 + j + tm, :]
            cb = cb + cw_ref[j:j + 1, b0:b0 + ck] * ub_scr[6 + j:6 + j + tm, :]
        gate_scr[:, a0:a0 + ck] = _silu_gate(ca, cb).astype(BF16)
    y_ref[...] = x + jnp.dot(gate_scr[...], wdn_ref[...], preferred_element_type=F32)


def ffn_prompt(x, seq_len, g, wup_bf16, cw, cb, wdn_bf16, *, tm=256, ck=256):
    M, D = x.shape
    d_ff = wdn_bf16.shape[0]
    assert seq_len % tm == 0 and M % seq_len == 0 and d_ff % ck == 0 and CONV_W - 1 <= 8
    n_tiles = M // tm
    y, ut = pl.pallas_call(
        functools.partial(_ffn_prompt_kernel, tm=tm, ck=ck, d_ff=d_ff, tiles_per_seq=seq_len // tm),
        out_shape=(jax.ShapeDtypeStruct((M, D), F32), jax.ShapeDtypeStruct((n_tiles * 8, 2 * d_ff), F32)),
        grid=(n_tiles,),
        in_specs=[
            pl.BlockSpec((tm, D), lambda i: (i, 0)),
            pl.BlockSpec((8, D), lambda i: (jnp.maximum(i * (tm // 8) - 1, 0), 0)),
            _const_spec((1, D)),
            _const_spec(wup_bf16.shape),
            _const_spec(cw.shape),
            _const_spec((1, 2 * d_ff)),
            _const_spec(wdn_bf16.shape),
        ],
        out_specs=(pl.BlockSpec((tm, D), lambda i: (i, 0)), pl.BlockSpec((8, 2 * d_ff), lambda i: (i, 0))),
        scratch_shapes=[pltpu.VMEM((tm + 8, ck), F32), pltpu.VMEM((tm + 8, ck), F32), pltpu.VMEM((tm, d_ff), BF16)],
        compiler_params=_params(("parallel",)),
        name="ffn_prompt",
    )(x, x, g.reshape(1, D), wup_bf16, cw, cb.reshape(1, 2 * d_ff), wdn_bf16)
    return y, ut


def _ffn_sample_kernel(x_ref, g_ref, wa_ref, wb_ref, cwa_ref, cwb_ref, cba_ref, cbb_ref,
                       e1a_ref, e1b_ref, e0a_ref, e0b_ref, wdn_ref, y_ref, ua_ref, ub_ref,
                       h_scr, ua_scr, ub_scr, *, m, t_len):
    c = pl.program_id(0)

    @pl.when(c == 0)
    def _():
        x = x_ref[...]
        r = lax.rsqrt(jnp.mean(x * x, axis=-1, keepdims=True) + EPS)
        h_scr[...] = (x * r * g_ref[...]).astype(BF16)
        y_ref[...] = x
        ua_scr[0:8, :] = jnp.zeros((8, ua_scr.shape[1]), F32)
        ub_scr[0:8, :] = jnp.zeros((8, ub_scr.shape[1]), F32)

    h = h_scr[...]
    ua = jnp.dot(h, wa_ref[...], preferred_element_type=F32)
    ub = jnp.dot(h, wb_ref[...], preferred_element_type=F32)
    ua_ref[...] = ua
    ub_ref[...] = ub
    ua_scr[8:8 + m, :] = ua
    ub_scr[8:8 + m, :] = ub
    t = lax.broadcasted_iota(jnp.int32, ua.shape, 0) % t_len

    def conv(u, u_scr, cw_ref, cb_ref, e1_ref, e0_ref):
        um1 = jnp.where(t >= 1, u_scr[7:7 + m, :], e1_ref[...])
        um2 = jnp.where(t >= 2, u_scr[6:6 + m, :], jnp.where(t == 1, e1_ref[...], e0_ref[...]))
        return cb_ref[...] + cw_ref[0:1, :] * um2 + cw_ref[1:2, :] * um1 + cw_ref[2:3, :] * u

    ca = conv(ua, ua_scr, cwa_ref, cba_ref, e1a_ref, e0a_ref)
    cb = conv(ub, ub_scr, cwb_ref, cbb_ref, e1b_ref, e0b_ref)
    gt = _silu_gate(ca, cb).astype(BF16)
    y_ref[...] += jnp.dot(gt, wdn_ref[...], preferred_element_type=F32)


def ffn_sample(x, t_len, prev, g, wup_bf16, cw, cb, wdn_bf16, *, ck=256):
    M, D = x.shape
    d_ff = wdn_bf16.shape[0]
    assert d_ff % ck == 0 and CONV_W == 3 and t_len >= 2
    nc = d_ff // ck
    e1 = jnp.repeat(prev[:, 1], t_len, axis=0)
    e0 = jnp.repeat(prev[:, 0], t_len, axis=0)
    cb2 = cb.reshape(1, 2 * d_ff)
    a_col = lambda c: (0, c)
    b_col = lambda c: (0, nc + c)
    return pl.pallas_call(
        functools.partial(_ffn_sample_kernel, m=M, t_len=t_len),
        out_shape=(jax.ShapeDtypeStruct((M, D), F32), jax.ShapeDtypeStruct((M, d_ff), F32),
                   jax.ShapeDtypeStruct((M, d_ff), F32)),
        grid=(nc,),
        in_specs=[
            pl.BlockSpec((M, D), lambda c: (0, 0)),
            pl.BlockSpec((1, D), lambda c: (0, 0)),
            pl.BlockSpec((D, ck), a_col), pl.BlockSpec((D, ck), b_col),
            pl.BlockSpec((CONV_W, ck), a_col), pl.BlockSpec((CONV_W, ck), b_col),
            pl.BlockSpec((1, ck), a_col), pl.BlockSpec((1, ck), b_col),
            pl.BlockSpec((M, ck), a_col), pl.BlockSpec((M, ck), b_col),
            pl.BlockSpec((M, ck), a_col), pl.BlockSpec((M, ck), b_col),
            pl.BlockSpec((ck, D), lambda c: (c, 0)),
        ],
        out_specs=(pl.BlockSpec((M, D), lambda c: (0, 0)),
                   pl.BlockSpec((M, ck), a_col), pl.BlockSpec((M, ck), a_col)),
        scratch_shapes=[pltpu.VMEM((M, D), BF16), pltpu.VMEM((M + 8, ck), F32), pltpu.VMEM((M + 8, ck), F32)],
        compiler_params=_params(("arbitrary",)),
        name="ffn_sample",
    )(x, g.reshape(1, D), wup_bf16, wup_bf16, cw, cw, cb2, cb2, e1, e1, e0, e0, wdn_bf16)


def _toeplitz(seg_row, rows, lo, width):
    x = jnp.broadcast_to(seg_row, (rows, seg_row.shape[1]))
    return pltpu.roll(x, 0, 1, stride=1, stride_axis=0)[:, lo:lo + width]


def _dil_prompt_kernel(q_ref, kc_ref, kp_ref, vc_ref, vp_ref, seg_ref, o_ref, lse_ref, bias_scr, *, tq, d, rp):
    b = pl.program_id(1)
    i = pl.program_id(2)
    rg = pl.program_id(3)

    @pl.when((b == 0) & (i == 0) & (rg == 0))
    def _():
        for h in range(HEADS_PER_LANE_TILE):
            bias_scr[h] = _toeplitz(seg_ref[h], tq, tq, tq + N_BACK)

    def rows(ref, n, r):
        return ref[...] if d == 1 else ref[pl.ds(r, n, stride=d), :]

    lane = lax.broadcasted_iota(jnp.int32, (tq, LANES), 1)
    col = lax.broadcasted_iota(jnp.int32, (tq, tq + N_BACK), 1)
    dead = col < jnp.where(i == 0, N_BACK, 0)
    heads = range(HEADS_PER_LANE_TILE)
    res = [rg * rp + u for u in range(rp)]
    logits, vals = [], []
    for r in res:
        q = rows(q_ref, tq, r) * SCALE
        k = jnp.concatenate([rows(kp_ref, N_BACK, r), rows(kc_ref, tq, r)], axis=0).astype(BF16)
        vals.append(jnp.concatenate([rows(vp_ref, N_BACK, r), rows(vc_ref, tq, r)], axis=0).astype(BF16))
        for h in heads:
            hm = (lane >= h * HEAD_DIM) & (lane < (h + 1) * HEAD_DIM)
            qh = jnp.where(hm, q, 0.0).astype(BF16)
            logits.append(lax.dot_general(qh, k, (((1,), (1,)), ((), ())), preferred_element_type=F32))
    probs, stats = [], []
    for n, s in enumerate(logits):
        s = jnp.where(dead, NEG, s + bias_scr[n % HEADS_PER_LANE_TILE])
        m = jnp.max(s, axis=-1, keepdims=True)
        e = jnp.exp(s - m)
        probs.append(e.astype(BF16))
        stats.append((m, jnp.sum(e, axis=-1, keepdims=True)))
    first = lane < HEAD_DIM
    for u, r in enumerate(res):
        outs, lses = [], []
        for h in heads:
            n = u * HEADS_PER_LANE_TILE + h
            m, l = stats[n]
            outs.append(jnp.dot(probs[n], vals[u], preferred_element_type=F32) / l)
            lses.append(m + jnp.log(l))
        o_val = jnp.where(first, outs[0], outs[1])
        lse_val = jnp.where(first, lses[0], lses[1])
        if d == 1:
            o_ref[...] = o_val
            lse_ref[...] = lse_val
        else:
            o_ref[pl.ds(r, tq, stride=d), :] = o_val
            lse_ref[pl.ds(r, tq, stride=d), :] = lse_val


def dil_prompt(qkv, batch, seq_len, gi, bias_k):
    win, d = DIL_GROUPS[gi]
    assert win // d == N_BACK and seq_len % (d * N_BACK) == 0
    sub = seq_len // d
    tq = min(512 if d < 16 else N_BACK, sub)
    assert sub % tq == 0 and tq % N_BACK == 0
    rp = max(1, min(d, 512 // tq))
    assert d % rp == 0
    hp_tiles = HD // LANES
    qkv3 = qkv.reshape(batch, seq_len, qkv.shape[1])
    width = 2 * tq + N_BACK
    step = N_BACK + tq - jnp.arange(width)
    seg = jnp.where(((step >= 0) & (step <= N_BACK))[:, None], bias_k[jnp.clip(step, 0, N_BACK)].astype(F32), NEG)
    seg = jnp.transpose(seg)[:, None, :]

    def col(which):
        return (which * N_GROUPS + gi) * hp_tiles

    cur = lambda which: pl.BlockSpec((None, tq * d, LANES), lambda hp, b, i, r: (b, i, col(which) + hp))
    prev = lambda which: pl.BlockSpec((None, N_BACK * d, LANES),
                                      lambda hp, b, i, r: (b, jnp.maximum(i * (tq // N_BACK) - 1, 0), col(which) + hp))
    out_spec = pl.BlockSpec((None, tq * d, LANES), lambda hp, b, i, r: (b, i, hp))
    o, lse = pl.pallas_call(
        functools.partial(_dil_prompt_kernel, tq=tq, d=d, rp=rp),
        out_shape=(jax.ShapeDtypeStruct((batch, seq_len, HD), F32),) * 2,
        grid=(hp_tiles, batch, sub // tq, d // rp),
        in_specs=[cur(0), cur(1), prev(1), cur(2), prev(2),
                  pl.BlockSpec((HEADS_PER_LANE_TILE, 1, width), lambda hp, b, i, r: (hp, 0, 0))],
        out_specs=(out_spec, out_spec),
        scratch_shapes=[pltpu.VMEM((HEADS_PER_LANE_TILE, tq, tq + N_BACK), F32)],
        compiler_params=_params(("parallel", "arbitrary", "arbitrary", "arbitrary")),
        name=f"dil_prompt_g{gi}",
    )(qkv3, qkv3, qkv3, qkv3, qkv3, seg)
    return o.reshape(batch * seq_len, HD), lse.reshape(batch * seq_len, HD)


def _dil_sample_kernel(ct_ref, q_ref, newt_ref, bias_c_ref, bias_n_ref, ot_ref, o_ref, lse_ref, *, hb, t_len, L):
    lane = lax.broadcasted_iota(jnp.int32, (HEAD_DIM, LANES), 1)
    nt = (((1,), (1,)), ((), ()))
    for hh in range(hb):
        q = (q_ref[hh] * SCALE).astype(BF16)
        kt, vt = ct_ref[0, hh], ct_ref[1, hh]
        knt, vnt = newt_ref[0, hh], newt_ref[1, hh]
        s_c = jnp.dot(q, kt.astype(BF16), preferred_element_type=F32) + bias_c_ref[hh]
        s_n = jnp.dot(q, knt.astype(BF16), preferred_element_type=F32) + bias_n_ref[hh]
        m = jnp.maximum(jnp.max(s_c, axis=1, keepdims=True), jnp.max(s_n, axis=1, keepdims=True))
        e_c = jnp.exp(s_c - m)
        e_n = jnp.exp(s_n - m)
        l = jnp.sum(e_c, axis=1, keepdims=True) + jnp.sum(e_n, axis=1, keepdims=True)
        pv = (lax.dot_general(e_c.astype(BF16), vt.astype(BF16), nt, preferred_element_type=F32)
              + lax.dot_general(e_n.astype(BF16), vnt.astype(BF16), nt, preferred_element_type=F32))
        o_ref[hh] = pv / l
        lse_ref[hh] = jnp.broadcast_to(m + jnp.log(l), (t_len, HEAD_DIM))
        for kv, (old, new) in enumerate(((kt, knt), (vt, vnt))):
            moved = pltpu.roll(old, L - t_len, 1)
            if L > LANES:
                ot_ref[kv, hh, :, :L - LANES] = moved[:, :L - LANES]
            ot_ref[kv, hh, :, L - LANES:] = jnp.where(lane >= LANES - t_len, new, moved[:, L - LANES:])


def dil_sample(cache, new_kv, q_s, dec_batch, t_len, gi, bias_k):
    win, d = DIL_GROUPS[gi]
    L = cache.shape[1]
    assert L == win and win // d == N_BACK and L % LANES == 0 and t_len <= LANES
    hb = min(N_HEADS, max(1, (4 << 20) // (2 * HEAD_DIM * L * 4)))
    assert N_HEADS % hb == 0
    ct = jnp.transpose(cache, (0, 2, 3, 4, 1))
    newt = jnp.pad(jnp.transpose(new_kv, (0, 2, 3, 4, 1)), ((0, 0),) * 4 + ((LANES - t_len, 0),))
    qh = jnp.transpose(q_s, (0, 2, 1, 3))
    tq = jnp.arange(t_len)[:, None]
    back = L + tq - jnp.arange(L)[None, :]
    ok_c = (back % d == 0) & (back // d <= N_BACK)
    bias_c = jnp.where(ok_c[:, :, None], bias_k[jnp.clip(back // d, 0, N_BACK)].astype(F32), NEG)
    t2 = jnp.arange(LANES)[None, :] - (LANES - t_len)
    ok_n = (t2 >= 0) & (t2 <= tq) & ((tq - t2) % d == 0)
    bias_n = jnp.where(ok_n[:, :, None], bias_k[jnp.clip((tq - t2) // d, 0, N_BACK)].astype(F32), NEG)
    bias_c = jnp.transpose(bias_c, (2, 0, 1))
    bias_n = jnp.transpose(bias_n, (2, 0, 1))
    cache_spec = pl.BlockSpec((None, 2, hb, HEAD_DIM, L), lambda b, j: (b, 0, j, 0, 0))
    row_spec = pl.BlockSpec((None, hb, t_len, HEAD_DIM), lambda b, j: (b, j, 0, 0))
    ot, o, lse = pl.pallas_call(
        functools.partial(_dil_sample_kernel, hb=hb, t_len=t_len, L=L),
        out_shape=(jax.ShapeDtypeStruct(ct.shape, F32),
                   jax.ShapeDtypeStruct(qh.shape, F32), jax.ShapeDtypeStruct(qh.shape, F32)),
        grid=(dec_batch, N_HEADS // hb),
        in_specs=[
            cache_spec,
            row_spec,
            pl.BlockSpec((None, 2, hb, HEAD_DIM, LANES), lambda b, j: (b, 0, j, 0, 0)),
            pl.BlockSpec((hb, t_len, L), lambda b, j: (j, 0, 0)),
            pl.BlockSpec((hb, t_len, LANES), lambda b, j: (j, 0, 0)),
        ],
        out_specs=(cache_spec, row_spec, row_spec),
        compiler_params=_params(("parallel", "parallel")),
        name=f"dil_sample_g{gi}",
    )(ct, qh, newt, bias_c, bias_n)
    to_rows = lambda a: jnp.transpose(a, (0, 2, 1, 3)).reshape(dec_batch * t_len, HD)
    return to_rows(o), to_rows(lse), jnp.transpose(ot, (0, 4, 1, 2, 3))


def _block_mean_kernel(k_ref, o_ref):
    o_ref[...] = jnp.broadcast_to(jnp.sum(k_ref[...], axis=0, keepdims=True) * (1.0 / MOBA_BLOCK), o_ref.shape)


def block_means(kvb, n_blocks_total):
    out = pl.pallas_call(
        _block_mean_kernel,
        out_shape=jax.ShapeDtypeStruct((n_blocks_total, 8, HD), F32),
        grid=(n_blocks_total,),
        in_specs=[pl.BlockSpec((MOBA_BLOCK, HD), lambda i: (i, 0))],
        out_specs=pl.BlockSpec((None, 8, HD), lambda i: (i, 0, 0)),
        compiler_params=_params(("parallel",)),
        name="block_means",
    )(kvb)
    return out[:, 0]


def _top_blocks(sc, blk, n_valid, n_sel, axis):
    sel = jnp.zeros(sc.shape, F32)
    big = float(sc.shape[axis])
    for it in range(n_sel):
        mx = jnp.max(sc, axis=axis, keepdims=True)
        idx = jnp.min(jnp.where(sc == mx, blk, big), axis=axis, keepdims=True)
        hit = blk == idx
        sel = jnp.maximum(sel, jnp.where(hit, jnp.where(it < n_valid, 1.0, 0.0), 0.0))
        sc = jnp.where(hit, -jnp.inf, sc)
    return sel


def _moba_prompt_kernel(it_ref, kt_ref, fl_ref, q_ref, k_ref, vt_ref, km_ref, seg_ref, o_ref,
                        m_scr, l_scr, acc_scr, selb_scr, bias_scr, *, nb, n_sel, kb):
    b = pl.program_id(1)
    t = pl.program_id(2)
    i = it_ref[t]
    kt = kt_ref[t]
    first = (fl_ref[t] & 1) != 0
    last = (fl_ref[t] & 2) != 0
    tb = MOBA_BLOCK

    def head_mask(shape, axis, h):
        idx = lax.broadcasted_iota(jnp.int32, shape, axis)
        return (idx >= h * HEAD_DIM) & (idx < (h + 1) * HEAD_DIM)

    @pl.when((b == 0) & (t == 0))
    def _():
        row = lax.broadcasted_iota(jnp.int32, (tb, tb), 0)
        colm = lax.broadcasted_iota(jnp.int32, (tb, tb), 1)
        for h in range(HEADS_PER_LANE_TILE):
            def fill(delta, carry, h=h):
                bias_scr[h, delta] = _toeplitz(seg_ref[h, pl.ds(delta, 1), :], tb, tb, tb) * LOG2E
                return carry
            lax.fori_loop(0, nb, fill, 0)
            bias_scr[h, 0] = jnp.where(colm >= row, bias_scr[h, 0], NEG)

    @pl.when(first)
    def _():
        m_scr[...] = jnp.full_like(m_scr, NEG)
        l_scr[...] = jnp.zeros_like(l_scr)
        acc_scr[...] = jnp.zeros_like(acc_scr)
        q = q_ref[...]
        km = km_ref[...]
        blk = lax.broadcasted_iota(jnp.int32, (nb, tb), 0).astype(F32)
        i_f = i.astype(F32)
        for h in range(HEADS_PER_LANE_TILE):
            kmh = jnp.where(head_mask(km.shape, 1, h), km, 0.0)
            sc = lax.dot_general(kmh, q, (((1,), (1,)), ((), ())), precision=HIGHEST, preferred_element_type=F32)
            sc = jnp.where(blk < i_f, sc, -jnp.inf)
            sel = _top_blocks(sc, blk, i, n_sel, axis=0) + jnp.where(blk == i_f, 1.0, 0.0)
            selb_scr[h] = jnp.where(sel > 0.5, 0.0, NEG)

    q = q_ref[...] * (SCALE * LOG2E)
    k = k_ref[...].astype(BF16)
    vt = vt_ref[...].astype(BF16)
    heads = range(HEADS_PER_LANE_TILE)
    raw = []
    for h in heads:
        qh = jnp.where(head_mask(q.shape, 1, h), q, 0.0).astype(BF16)
        raw.append([lax.dot_general(k[c * tb:(c + 1) * tb], qh, (((1,), (1,)), ((), ())), preferred_element_type=F32)
                    for c in range(kb)])
    parts, shifts, alphas = [], [], []
    for h in heads:
        ps, sbs, cms = [], [], []
        for c in range(kb):
            jj = kt * kb + c
            p = raw[h][c] + bias_scr[h, jnp.maximum(i - jj, 0)]
            sb = selb_scr[h, pl.ds(jj, 1), :]
            ps.append(p)
            sbs.append(sb)
            cms.append(jnp.max(p, axis=0, keepdims=True) + sb)
        m_old = m_scr[h]
        m_new = jnp.maximum(m_old, functools.reduce(jnp.maximum, cms))
        parts.append(ps)
        shifts.append([m_new - sb for sb in sbs])
        alphas.append(jnp.exp2(m_old - m_new))
        m_scr[h] = m_new
    for h in heads:
        acc = alphas[h] * acc_scr[h]
        l_new = alphas[h] * l_scr[h]
        for c in range(kb):
            e = jnp.exp2(parts[h][c] - shifts[h][c])
            l_new = l_new + jnp.sum(e, axis=0, keepdims=True)
            acc = acc + jnp.dot(vt[h * HEAD_DIM:(h + 1) * HEAD_DIM, c * tb:(c + 1) * tb], e.astype(BF16),
                                preferred_element_type=F32)
        l_scr[h] = l_new
        acc_scr[h] = acc

    @pl.when(last)
    def _():
        o_t = jnp.concatenate([acc_scr[h] / l_scr[h] for h in heads], axis=0)
        o_ref[...] = o_t.T


def moba_prompt(q, kvb, k_means, tab, batch, seq_len):
    tb = MOBA_BLOCK
    kb = 4
    assert seq_len % (kb * tb) == 0
    nb = seq_len // tb
    n_sel = min(MOBA_TOPK, nb)
    hp_tiles = HD // LANES
    sched = [(i, kt) for i in range(nb) for kt in [i // kb] + list(range(i // kb))]
    it = jnp.array([s[0] for s in sched], jnp.int32)
    ktile = jnp.array([s[1] for s in sched], jnp.int32)
    flags = jnp.array([(1 if kt == i // kb else 0) | (2 if kt == (i // kb - 1 if i >= kb else 0) else 0)
                       for i, kt in sched], jnp.int32)
    dist = jnp.arange(nb)[:, None] * tb + jnp.arange(2 * tb)[None, :] - tb
    seg = tab.T.astype(F32)[:, jnp.clip(dist, 0, tab.shape[0] - 1)]
    km3 = k_means.reshape(batch, nb, HD)
    v_t = kvb[:, HD:].T
    nkt = nb // kb
    return pl.pallas_call(
        functools.partial(_moba_prompt_kernel, nb=nb, n_sel=n_sel, kb=kb),
        out_shape=jax.ShapeDtypeStruct(q.shape, F32),
        grid_spec=pltpu.PrefetchScalarGridSpec(
            num_scalar_prefetch=3,
            grid=(hp_tiles, batch, len(sched)),
            in_specs=[
                pl.BlockSpec((tb, LANES), lambda hp, b, t, it, kt, fl: (b * nb + it[t], hp)),
                pl.BlockSpec((kb * tb, LANES), lambda hp, b, t, it, kt, fl: (b * nkt + kt[t], hp)),
                pl.BlockSpec((LANES, kb * tb), lambda hp, b, t, it, kt, fl: (hp, b * nkt + kt[t])),
                pl.BlockSpec((None, nb, LANES), lambda hp, b, t, it, kt, fl: (b, 0, hp)),
                pl.BlockSpec((HEADS_PER_LANE_TILE, nb, 2 * tb), lambda hp, b, t, it, kt, fl: (hp, 0, 0)),
            ],
            out_specs=pl.BlockSpec((tb, LANES), lambda hp, b, t, it, kt, fl: (b * nb + it[t], hp)),
            scratch_shapes=[pltpu.VMEM((HEADS_PER_LANE_TILE, 1, tb), F32),
                            pltpu.VMEM((HEADS_PER_LANE_TILE, 1, tb), F32),
                            pltpu.VMEM((HEADS_PER_LANE_TILE, HEAD_DIM, tb), F32),
                            pltpu.VMEM((HEADS_PER_LANE_TILE, nb, tb), F32),
                            pltpu.VMEM((HEADS_PER_LANE_TILE, nb, tb, tb), F32)],
        ),
        compiler_params=_params(("parallel", "arbitrary", "arbitrary")),
        name="moba_prompt",
    )(it, ktile, flags, q, kvb, v_t, km3, seg)


def _moba_page_kernel(pt_ref, *rest, t_len, pp):
    del pt_ref
    pool_refs, (qbd_ref, bias_ref, dmask_ref, ksum_ref, m_ref, l_ref, acc_ref) = rest[:pp], rest[pp:]
    j = pl.program_id(1)
    qbd = qbd_ref[...].astype(BF16)
    dmask = dmask_ref[...]
    kts = [ref[0] for ref in pool_refs]
    logits = [jnp.dot(qbd, kt.astype(BF16), preferred_element_type=F32) for kt in kts]
    ms, ls, es = [], [], []
    for u, s in enumerate(logits):
        s = s + bias_ref[u]
        m = jnp.max(s, axis=1, keepdims=True)
        e = jnp.exp(s - m)
        ms.append(m)
        ls.append(jnp.sum(e, axis=1, keepdims=True))
        es.append(e.astype(BF16))
    for u in range(pp):
        pv = lax.dot_general(es[u], pool_refs[u][1].astype(BF16), (((1,), (1,)), ((), ())),
                             preferred_element_type=F32)
        rows = [jnp.sum(pv[t * N_HEADS:(t + 1) * N_HEADS, :] * dmask, axis=0, keepdims=True) for t in range(t_len)]
        acc_ref[u * t_len:(u + 1) * t_len, :] = jnp.concatenate(rows, axis=0)

    @pl.when(j == 0)
    def _():
        ksum_ref[...] = jnp.zeros_like(ksum_ref)
        m_ref[...] = jnp.zeros_like(m_ref)
        l_ref[...] = jnp.zeros_like(l_ref)

    def put(ref, cols):
        lane = lax.broadcasted_iota(jnp.int32, ref.shape, 1)
        val = ref[...]
        for u, col in enumerate(cols):
            val = jnp.where(lane == j * pp + u, col, val)
        ref[...] = val

    put(ksum_ref, [jnp.sum(kt, axis=1, keepdims=True) for kt in kts])
    put(m_ref, ms)
    put(l_ref, ls)


def _moba_combine_kernel(ks_ref, m_ref, l_ref, acc_ref, qbd_ref, knt_ref, vnew_ref, bias_ref, pair_ref, dmask_ref,
                         o_ref, *, n_pages, n_blocks, n_sel, t_len):
    qbd = qbd_ref[...]
    pair = pair_ref[...]
    sc_pages = jnp.dot(qbd, ks_ref[...], precision=HIGHEST, preferred_element_type=F32)
    sc = jnp.dot(sc_pages, pair, precision=HIGHEST, preferred_element_type=F32) * (1.0 / MOBA_BLOCK)
    lane = lax.broadcasted_iota(jnp.int32, sc.shape, 1).astype(F32)
    sc = jnp.where(lane < n_blocks, sc, -jnp.inf)
    sel = _top_blocks(sc, lane, n_sel, n_sel, axis=1)
    picked = lax.dot_general(sel, pair, (((1,), (1,)), ((), ())), preferred_element_type=F32) > 0.5
    m_p = m_ref[...]
    s_own = jnp.dot(qbd.astype(BF16), knt_ref[...].astype(BF16), preferred_element_type=F32) + bias_ref[...]
    m_tot = jnp.maximum(jnp.max(jnp.where(picked, m_p, NEG), axis=1, keepdims=True),
                        jnp.max(s_own, axis=1, keepdims=True))
    w = jnp.where(picked, jnp.exp(m_p - m_tot), 0.0)
    e_own = jnp.exp(s_own - m_tot)
    l_tot = jnp.sum(w * l_ref[...], axis=1, keepdims=True) + jnp.sum(e_own, axis=1, keepdims=True)
    dmask = dmask_ref[...]
    vnew = vnew_ref[...].astype(BF16)
    rows = []
    for t in range(t_len):
        hs = slice(t * N_HEADS, (t + 1) * N_HEADS)
        a = jnp.dot(w[hs, :n_pages], acc_ref[t], precision=HIGHEST, preferred_element_type=F32)
        a = a + jnp.dot(e_own[hs].astype(BF16), vnew, preferred_element_type=F32)
        num = jnp.sum(a * dmask, axis=0, keepdims=True)
        den = jnp.sum(l_tot[hs] * dmask, axis=0, keepdims=True)
        rows.append(num / den)
    o_ref[...] = jnp.concatenate(rows, axis=0)


def moba_sample(q_s, kvb_s, pool, page_table, tab, dec_batch, t_len):
    n_pool, page = pool.shape[0], pool.shape[1]
    n_pages = page_table.shape[1]
    past = n_pages * page
    assert MOBA_BLOCK % page == 0 and past % MOBA_BLOCK == 0 and t_len <= MOBA_BLOCK and t_len == 8
    ppb = MOBA_BLOCK // page
    n_blocks = past // MOBA_BLOCK
    n_sel = min(MOBA_TOPK, n_blocks + 1)
    assert n_blocks >= n_sel, "fewer cached blocks than top-k picks is not supported"
    n_col = t_len * N_HEADS
    assert n_col == LANES and page == LANES and n_pages <= LANES
    pool_t = jnp.transpose(pool, (0, 2, 3, 4, 1)).reshape(n_pool, 2, HD, page)
    q4 = q_s.reshape(dec_batch, t_len, N_HEADS, HEAD_DIM) * SCALE
    qbd = jnp.einsum('bthx,hg->bthgx', q4, jnp.eye(N_HEADS, dtype=F32)).reshape(dec_batch, n_col, HD)
    dist = past + jnp.arange(t_len)[None, :, None] - (jnp.arange(n_pages)[:, None, None] * page
                                                      + jnp.arange(page)[None, None, :])
    bias_pages = jnp.swapaxes(tab.astype(F32)[dist], 2, 3).reshape(n_pages, n_col, page)
    d_own = jnp.arange(t_len)[:, None] - jnp.arange(LANES)[None, :]
    ok_own = (d_own >= 0) & (jnp.arange(LANES)[None, :] < t_len)
    bias_own = jnp.where(ok_own[:, :, None], tab.astype(F32)[jnp.clip(d_own, 0, t_len)], NEG)
    bias_own = jnp.swapaxes(bias_own, 1, 2).reshape(n_col, LANES)
    dmask = (jnp.arange(N_HEADS)[:, None] == (jnp.arange(HD) // HEAD_DIM)[None, :]).astype(F32)
    pair = ((jnp.arange(LANES)[:, None] // ppb == jnp.arange(LANES)[None, :])
            & (jnp.arange(LANES)[:, None] < n_pages)).astype(F32)
    new3 = kvb_s.reshape(dec_batch, t_len, 2 * HD)
    knt = jnp.pad(jnp.swapaxes(new3[:, :, :HD], 1, 2), ((0, 0), (0, 0), (0, LANES - t_len)))
    vnew = jnp.pad(new3[:, :, HD:], ((0, 0), (0, LANES - t_len), (0, 0)))

    pp = 4 if n_pages % 4 == 0 else 1
    page_spec = lambda u: pl.BlockSpec((None, 2, HD, page), lambda b, j, pt: (pt[b, j * pp + u], 0, 0, 0))
    stat = lambda rows: pl.BlockSpec((None, rows, LANES), lambda b, j, pt: (b, 0, 0))
    ksum, m_p, l_p, acc = pl.pallas_call(
        functools.partial(_moba_page_kernel, t_len=t_len, pp=pp),
        out_shape=(jax.ShapeDtypeStruct((dec_batch, HD, LANES), F32),
                   jax.ShapeDtypeStruct((dec_batch, n_col, LANES), F32),
                   jax.ShapeDtypeStruct((dec_batch, n_col, LANES), F32),
                   jax.ShapeDtypeStruct((dec_batch, n_pages * t_len, HD), F32)),
        grid_spec=pltpu.PrefetchScalarGridSpec(
            num_scalar_prefetch=1,
            grid=(dec_batch, n_pages // pp),
            in_specs=[page_spec(u) for u in range(pp)] + [
                pl.BlockSpec((None, n_col, HD), lambda b, j, pt: (b, 0, 0)),
                pl.BlockSpec((pp, n_col, page), lambda b, j, pt: (j, 0, 0)),
                pl.BlockSpec((N_HEADS, HD), lambda b, j, pt: (0, 0)),
            ],
            out_specs=(stat(HD), stat(n_col), stat(n_col),
                       pl.BlockSpec((None, pp * t_len, HD), lambda b, j, pt: (b, j, 0))),
        ),
        compiler_params=_params(("parallel", "arbitrary")),
        name="moba_page",
    )(page_table, *([pool_t] * pp), qbd, bias_pages, dmask)

    acc = jnp.swapaxes(acc.reshape(dec_batch, n_pages, t_len, HD), 1, 2)
    per_b = lambda shape: pl.BlockSpec((None,) + shape, lambda b: (b,) + (0,) * len(shape))
    const = lambda shape: pl.BlockSpec(shape, lambda b: (0,) * len(shape))
    out = pl.pallas_call(
        functools.partial(_moba_combine_kernel, n_pages=n_pages, n_blocks=n_blocks, n_sel=n_sel, t_len=t_len),
        out_shape=jax.ShapeDtypeStruct((dec_batch, t_len, HD), F32),
        grid=(dec_batch,),
        in_specs=[per_b((HD, LANES)), per_b((n_col, LANES)), per_b((n_col, LANES)),
                  per_b((t_len, n_pages, HD)), per_b((n_col, HD)), per_b((HD, LANES)), per_b((LANES, HD)),
                  const((n_col, LANES)), const((LANES, LANES)), const((N_HEADS, HD))],
        out_specs=per_b((t_len, HD)),
        compiler_params=_params(("parallel",)),
        name="moba_combine",
    )(ksum, m_p, l_p, acc, qbd, knt, vnew, bias_own, pair, dmask)
    return out.reshape(dec_batch * t_len, HD)


def kernel(x_prompt, x_sample, cache_a_kv_g0, cache_a_kv_g1, cache_a_kv_g2, cache_b_kv_pool, page_table, state_ffn_conv, rel_bias, norm_attn, norm_ffn, w_qkv_a, q_norm_a, k_norm_a, w_o_a, norm_kv, w_kv_b, k_norm_b, w_q_b, q_norm_b, w_o_b, w_up, conv_w, conv_b, w_down):
    B, S, D = x_prompt.shape
    DB, T, _ = x_sample.shape
    depth = norm_attn.shape[0]
    n_self = w_qkv_a.shape[0]
    caches = (cache_a_kv_g0, cache_a_kv_g1, cache_a_kv_g2)
    past = page_table.shape[1] * cache_b_kv_pool.shape[1]
    assert D == HD

    group_bias = [rel_bias[_rel_bucket(dil * jnp.arange(win // dil + 1))] for (win, dil) in DIL_GROUPS]
    tab = rel_bias[_rel_bucket(jnp.arange(max(S, past + T)))]

    ones =jnp.ones((HD,), F32)
    tile_h = lambda gvec, n: jnp.tile(gvec, n * N_HEADS)

    def self_layer(x, l, prompt):
        hg = jnp.concatenate([tile_h(q_norm_a[l], N_GROUPS), tile_h(k_norm_a[l], N_GROUPS), jnp.tile(ones, N_GROUPS)])
        qkv = norm_matmul(x, norm_attn[l], w_qkv_a[l].astype(BF16), hg, 2 * N_GROUPS * HD)
        outs, lses, bufs = [], [], []
        seq = S if prompt else T
        qkv3 = qkv.reshape(-1, seq, qkv.shape[1])
        for gi, (win, dil) in enumerate(DIL_GROUPS):
            keep = min(win, seq)
            part = lambda which: qkv3[:, seq - keep:, (which * N_GROUPS + gi) * HD:(which * N_GROUPS + gi + 1) * HD
                                      ].reshape(-1, keep, N_HEADS, HEAD_DIM)
            new_kv = jnp.stack([part(1), part(2)], axis=2)
            if prompt:
                o, lse = dil_prompt(qkv, B, S, gi, group_bias[gi])
                bufs.append(new_kv)
            else:
                o, lse, buf = dil_sample(caches[gi][l], new_kv, part(0), DB, T, gi, group_bias[gi])
                bufs.append(buf)
            outs.append(o)
            lses.append(lse)
        return merge_proj(outs, lses, w_o_a[l].astype(BF16), x), bufs

    def ffn(x, l, prompt):
        wup, wdn = w_up[l].astype(BF16), w_down[l].astype(BF16)
        if prompt:
            tm = 256
            y, ut = ffn_prompt(x, S, norm_ffn[l], wup, conv_w[l], conv_b[l], wdn, tm=tm)
            cs = ut.reshape(B, S // tm, 8, ut.shape[1])[:, -1, 8 - (CONV_W - 1):]
        else:
            y, ua, ub = ffn_sample(x, T, state_ffn_conv[l], norm_ffn[l], wup, conv_w[l], conv_b[l], wdn)
            u = jnp.concatenate([ua, ub], axis=1)
            cs = u.reshape(DB, T, u.shape[1])[:, T - (CONV_W - 1):]
        return y, cs

    def trunk(x, prompt):
        nb_rows = B if prompt else DB
        a_new = [[] for _ in range(N_GROUPS)]
        conv_new = []
        kvb = None
        k_means = None
        for l in range(depth):
            if l < n_self:
                x, bufs = self_layer(x, l, prompt)
                for gi in range(N_GROUPS):
                    a_new[gi].append(bufs[gi])
            else:
                if l == n_self:
                    hg = jnp.concatenate([tile_h(k_norm_b, 1), ones])
                    kvb = norm_matmul(x, norm_kv, w_kv_b.astype(BF16), hg, HD)
                    if prompt:
                        k_means = block_means(kvb, B * S // MOBA_BLOCK)
                lb = l - n_self
                q = norm_matmul(x, norm_attn[l], w_q_b[lb].astype(BF16), tile_h(q_norm_b[lb], 1), HD)
                if prompt:
                    o = moba_prompt(q, kvb, k_means, tab, B, S)
                else:
                    o = moba_sample(q, kvb, cache_b_kv_pool, page_table, tab, DB, T)
                x = merge_proj([o], None, w_o_b[lb].astype(BF16), x)
            x, cs = ffn(x, l, prompt)
            conv_new.append(cs)
        seq = S if prompt else T
        kv_new = kvb.reshape(nb_rows, seq, 2, N_HEADS, HEAD_DIM)
        return x.reshape(nb_rows, seq, D), [jnp.stack(a, 0) for a in a_new], kv_new, jnp.stack(conv_new, 0)

    y_p, a_p, kv_p, conv_p = trunk(x_prompt.reshape(B * S, D), True)
    y_s, a_s, kv_s, conv_s = trunk(x_sample.reshape(DB * T, D), False)
    return (y_p, y_s, a_p[0], a_p[1], a_p[2], a_s[0], a_s[1], a_s[2], kv_p, kv_s, conv_p, conv_s)
```

```python
import functools
import math

import jax
import jax.numpy as jnp
from jax import lax
from jax.experimental import pallas as pl
from jax.experimental.pallas import tpu as pltpu

HEAD_DIM = 64
N_HEADS = 16
HD = N_HEADS * HEAD_DIM
DIL_GROUPS = ((128, 1), (512, 4), (2048, 16))
N_GROUPS = len(DIL_GROUPS)
N_BACK = 128
MOBA_BLOCK = 256
MOBA_TOPK = 3
NUM_BUCKETS = 32
REL_MAX_DIST = 4096
CONV_W = 3
EPS = 1e-6
SCALE = HEAD_DIM ** -0.5
NEG = -1e30
LOG2E = 1.4426950408889634
LANES = 128
MXU_TILE = 256
HEADS_PER_LANE_TILE = LANES // HEAD_DIM
VMEM_LIMIT = 56 * 1024 * 1024

F32 = jnp.float32
BF16 = jnp.bfloat16
HIGHEST = lax.Precision.HIGHEST


def _params(sem, vmem=VMEM_LIMIT):
    return pltpu.CompilerParams(dimension_semantics=sem, vmem_limit_bytes=vmem)


def _const_spec(shape):
    nd = len(shape)
    return pl.BlockSpec(shape, lambda *_: (0,) * nd, pipeline_mode=pl.Buffered(1))


def _rel_bucket(dist):
    n = jnp.maximum(dist, 0)
    exact = NUM_BUCKETS // 2
    nf = jnp.maximum(n, exact).astype(F32)
    large = exact + (jnp.log(nf / exact) / math.log(REL_MAX_DIST / exact) * (NUM_BUCKETS - exact)).astype(jnp.int32)
    return jnp.where(n < exact, n, jnp.minimum(large, NUM_BUCKETS - 1))


def _norm_matmul_kernel(x_ref, g_ref, w_ref, hg_ref, bd_ref, o_ref, h_scr, *, n_norm_tiles):
    j = pl.program_id(1)

    @pl.when(j == 0)
    def _():
        x = x_ref[...]
        r = lax.rsqrt(jnp.mean(x * x, axis=-1, keepdims=True) + EPS)
        h_scr[...] = (x * r * g_ref[...]).astype(BF16)

    y = jnp.dot(h_scr[...], w_ref[...], preferred_element_type=F32)

    @pl.when(j < n_norm_tiles)
    def _():
        y2 = (y * y).astype(BF16)
        ss = jnp.concatenate([jnp.dot(y2[:, c:c + MXU_TILE], bd_ref[...], preferred_element_type=F32)
                              for c in range(0, y.shape[1], MXU_TILE)], axis=1)
        o_ref[...] = y * lax.rsqrt(ss * (1.0 / HEAD_DIM) + EPS) * hg_ref[...]

    @pl.when(j >= n_norm_tiles)
    def _():
        o_ref[...] = y


def norm_matmul(x, g, w_bf16, head_gain, n_norm_cols, *, tn=512):
    M, K = x.shape
    N = w_bf16.shape[1]
    tm = min(1024, M)
    assert M % tm == 0 and N % tn == 0 and n_norm_cols % tn == 0 and tn % MXU_TILE == 0
    hid = jnp.arange(MXU_TILE) // HEAD_DIM
    bd = (hid[:, None] == hid[None, :]).astype(BF16)
    return pl.pallas_call(
        functools.partial(_norm_matmul_kernel, n_norm_tiles=n_norm_cols // tn),
        out_shape=jax.ShapeDtypeStruct((M, N), F32),
        grid=(M // tm, N // tn),
        in_specs=[
            pl.BlockSpec((tm, K), lambda i, j: (i, 0)),
            pl.BlockSpec((1, K), lambda i, j: (0, 0)),
            pl.BlockSpec((K, tn), lambda i, j: (0, j)),
            pl.BlockSpec((1, tn), lambda i, j: (0, j)),
            pl.BlockSpec((MXU_TILE, MXU_TILE), lambda i, j: (0, 0)),
        ],
        out_specs=pl.BlockSpec((tm, tn), lambda i, j: (i, j)),
        scratch_shapes=[pltpu.VMEM((tm, K), BF16)],
        compiler_params=_params(("parallel", "arbitrary")),
        name="norm_matmul",
    )(x, g.reshape(1, K), w_bf16, head_gain.reshape(1, N), bd)


def _merge_proj_kernel(*refs, n_g):
    o_refs = refs[:n_g]
    lse_refs = refs[n_g:2 * n_g] if n_g > 1 else ()
    w_ref, x_ref, out_ref = refs[-3:]
    if n_g == 1:
        a = o_refs[0][...]
    else:
        ls = [r[...] for r in lse_refs]
        mx = functools.reduce(jnp.maximum, ls)
        ws = [jnp.exp(l - mx) for l in ls]
        den = functools.reduce(lambda p, q: p + q, ws)
        num = functools.reduce(lambda p, q: p + q, [w * r[...] for w, r in zip(ws, o_refs)])
        a = num / den
    out_ref[...] = x_ref[...] + jnp.dot(a.astype(BF16), w_ref[...], preferred_element_type=F32)


def merge_proj(os_, lses, w_bf16, x):
    M, D = x.shape
    n_g = len(os_)
    tm = min(256, M)
    assert M % tm == 0
    row = pl.BlockSpec((tm, D), lambda i: (i, 0))
    ins = list(os_) + (list(lses) if n_g > 1 else [])
    return pl.pallas_call(
        functools.partial(_merge_proj_kernel, n_g=n_g),
        out_shape=jax.ShapeDtypeStruct((M, D), F32),
        grid=(M // tm,),
        in_specs=[row] * len(ins) + [_const_spec(w_bf16.shape), row],
        out_specs=row,
        compiler_params=_params(("parallel",)),
        name="merge_proj",
    )(*ins, w_bf16, x)


def _silu_gate(ca, cb):
    return (ca / (1.0 + jnp.exp(-ca))) * cb


def _ffn_prompt_kernel(x_ref, xp_ref, g_ref, wup_ref, cw_ref, cb_ref, wdn_ref, y_ref, ut_ref,
                       ua_scr, ub_scr, gate_scr, *, tm, ck, d_ff, tiles_per_seq):
    i = pl.program_id(0)
    x = x_ref[...]
    keep = jnp.where(i % tiles_per_seq == 0, 0.0, 1.0)
    xc = jnp.concatenate([xp_ref[...] * keep, x], axis=0)
    r = lax.rsqrt(jnp.mean(xc * xc, axis=-1, keepdims=True) + EPS)
    h = (xc * r * g_ref[...]).astype(BF16)
    for c in range(d_ff // ck):
        a0, b0 = c * ck, d_ff + c * ck
        ua_scr[...] = jnp.dot(h, wup_ref[:, a0:a0 + ck], preferred_element_type=F32)
        ub_scr[...] = jnp.dot(h, wup_ref[:, b0:b0 + ck], preferred_element_type=F32)
        ut_ref[:, a0:a0 + ck] = ua_scr[tm:tm + 8, :]
        ut_ref[:, b0:b0 + ck] = ub_scr[tm:tm + 8, :]
        ca = cb_ref[:, a0:a0 + ck]
        cb = cb_ref[:, b0:b0 + ck]
        for j in range(CONV_W):
            ca = ca + cw_ref[j:j + 1, a0:a0 + ck] * ua_scr[6 + j:6 + j + tm, :]
            cb = cb + cw_ref[j:j + 1, b0:b0 + ck] * ub_scr[6 + j:6 + j + tm, :]
        gate_scr[:, a0:a0 + ck] = _silu_gate(ca, cb).astype(BF16)
    y_ref[...] = x + jnp.dot(gate_scr[...], wdn_ref[...], preferred_element_type=F32)


def ffn_prompt(x, seq_len, g, wup_bf16, cw, cb, wdn_bf16, *, tm=256, ck=256):
    M, D = x.shape
    d_ff = wdn_bf16.shape[0]
    assert seq_len % tm == 0 and M % seq_len == 0 and d_ff % ck == 0 and CONV_W - 1 <= 8
    n_tiles = M // tm
    y, ut = pl.pallas_call(
        functools.partial(_ffn_prompt_kernel, tm=tm, ck=ck, d_ff=d_ff, tiles_per_seq=seq_len // tm),
        out_shape=(jax.ShapeDtypeStruct((M, D), F32), jax.ShapeDtypeStruct((n_tiles * 8, 2 * d_ff), F32)),
        grid=(n_tiles,),
        in_specs=[
            pl.BlockSpec((tm, D), lambda i: (i, 0)),
            pl.BlockSpec((8, D), lambda i: (jnp.maximum(i * (tm // 8) - 1, 0), 0)),
            _const_spec((1, D)),
            _const_spec(wup_bf16.shape),
            _const_spec(cw.shape),
            _const_spec((1, 2 * d_ff)),
            _const_spec(wdn_bf16.shape),
        ],
        out_specs=(pl.BlockSpec((tm, D), lambda i: (i, 0)), pl.BlockSpec((8, 2 * d_ff), lambda i: (i, 0))),
        scratch_shapes=[pltpu.VMEM((tm + 8, ck), F32), pltpu.VMEM((tm + 8, ck), F32), pltpu.VMEM((tm, d_ff), BF16)],
        compiler_params=_params(("parallel",)),
        name="ffn_prompt",
    )(x, x, g.reshape(1, D), wup_bf16, cw, cb.reshape(1, 2 * d_ff), wdn_bf16)
    return y, ut


def _ffn_sample_kernel(x_ref, g_ref, wa_ref, wb_ref, cwa_ref, cwb_ref, cba_ref, cbb_ref,
                       e1a_ref, e1b_ref, e0a_ref, e0b_ref, wdn_ref, y_ref, ua_ref, ub_ref,
                       h_scr, ua_scr, ub_scr, *, m, t_len):
    c = pl.program_id(0)

    @pl.when(c == 0)
    def _():
        x = x_ref[...]
        r = lax.rsqrt(jnp.mean(x * x, axis=-1, keepdims=True) + EPS)
        h_scr[...] = (x * r * g_ref[...]).astype(BF16)
        y_ref[...] = x
        ua_scr[0:8, :] = jnp.zeros((8, ua_scr.shape[1]), F32)
        ub_scr[0:8, :] = jnp.zeros((8, ub_scr.shape[1]), F32)

    h = h_scr[...]
    ua = jnp.dot(h, wa_ref[...], preferred_element_type=F32)
    ub = jnp.dot(h, wb_ref[...], preferred_element_type=F32)
    ua_ref[...] = ua
    ub_ref[...] = ub
    ua_scr[8:8 + m, :] = ua
    ub_scr[8:8 + m, :] = ub
    t = lax.broadcasted_iota(jnp.int32, ua.shape, 0) % t_len

    def conv(u, u_scr, cw_ref, cb_ref, e1_ref, e0_ref):
        um1 = jnp.where(t >= 1, u_scr[7:7 + m, :], e1_ref[...])
        um2 = jnp.where(t >= 2, u_scr[6:6 + m, :], jnp.where(t == 1, e1_ref[...], e0_ref[...]))
        return cb_ref[...] + cw_ref[0:1, :] * um2 + cw_ref[1:2, :] * um1 + cw_ref[2:3, :] * u

    ca = conv(ua, ua_scr, cwa_ref, cba_ref, e1a_ref, e0a_ref)
    cb = conv(ub, ub_scr, cwb_ref, cbb_ref, e1b_ref, e0b_ref)
    gt = _silu_gate(ca, cb).astype(BF16)
    y_ref[...] += jnp.dot(gt, wdn_ref[...], preferred_element_type=F32)


def ffn_sample(x, t_len, prev, g, wup_bf16, cw, cb, wdn_bf16, *, ck=256):
    M, D = x.shape
    d_ff = wdn_bf16.shape[0]
    assert d_ff % ck == 0 and CONV_W == 3 and t_len >= 2
    nc = d_ff // ck
    e1 = jnp.repeat(prev[:, 1], t_len, axis=0)
    e0 = jnp.repeat(prev[:, 0], t_len, axis=0)
    cb2 = cb.reshape(1, 2 * d_ff)
    a_col = lambda c: (0, c)
    b_col = lambda c: (0, nc + c)
    return pl.pallas_call(
        functools.partial(_ffn_sample_kernel, m=M, t_len=t_len),
        out_shape=(jax.ShapeDtypeStruct((M, D), F32), jax.ShapeDtypeStruct((M, d_ff), F32),
                   jax.ShapeDtypeStruct((M, d_ff), F32)),
        grid=(nc,),
        in_specs=[
            pl.BlockSpec((M, D), lambda c: (0, 0)),
            pl.BlockSpec((1, D), lambda c: (0, 0)),
            pl.BlockSpec((D, ck), a_col), pl.BlockSpec((D, ck), b_col),
            pl.BlockSpec((CONV_W, ck), a_col), pl.BlockSpec((CONV_W, ck), b_col),
            pl.BlockSpec((1, ck), a_col), pl.BlockSpec((1, ck), b_col),
            pl.BlockSpec((M, ck), a_col), pl.BlockSpec((M, ck), b_col),
            pl.BlockSpec((M, ck), a_col), pl.BlockSpec((M, ck), b_col),
            pl.BlockSpec((ck, D), lambda c: (c, 0)),
        ],
        out_specs=(pl.BlockSpec((M, D), lambda c: (0, 0)),
                   pl.BlockSpec((M, ck), a_col), pl.BlockSpec((M, ck), a_col)),
        scratch_shapes=[pltpu.VMEM((M, D), BF16), pltpu.VMEM((M + 8, ck), F32), pltpu.VMEM((M + 8, ck), F32)],
        compiler_params=_params(("arbitrary",)),
        name="ffn_sample",
    )(x, g.reshape(1, D), wup_bf16, wup_bf16, cw, cw, cb2, cb2, e1, e1, e0, e0, wdn_bf16)


def _toeplitz(seg_row, rows, lo, width):
    x = jnp.broadcast_to(seg_row, (rows, seg_row.shape[1]))
    return pltpu.roll(x, 0, 1, stride=1, stride_axis=0)[:, lo:lo + width]


def _dil_prompt_kernel(q_ref, kc_ref, kp_ref, vc_ref, vp_ref, seg_ref, o_ref, lse_ref, bias_scr, *, tq, d, rp):
    b = pl.program_id(1)
    i = pl.program_id(2)
    rg = pl.program_id(3)

    @pl.when((b == 0) & (i == 0) & (rg == 0))
    def _():
        for h in range(HEADS_PER_LANE_TILE):
            bias_scr[h] = _toeplitz(seg_ref[h], tq, tq, tq + N_BACK)

    def rows(ref, n, r):
        return ref[...] if d == 1 else ref[pl.ds(r, n, stride=d), :]

    lane = lax.broadcasted_iota(jnp.int32, (tq, LANES), 1)
    col = lax.broadcasted_iota(jnp.int32, (tq, tq + N_BACK), 1)
    dead = col < jnp.where(i == 0, N_BACK, 0)
    heads = range(HEADS_PER_LANE_TILE)
    res = [rg * rp + u for u in range(rp)]
    logits, vals = [], []
    for r in res:
        q = rows(q_ref, tq, r) * SCALE
        k = jnp.concatenate([rows(kp_ref, N_BACK, r), rows(kc_ref, tq, r)], axis=0).astype(BF16)
        vals.append(jnp.concatenate([rows(vp_ref, N_BACK, r), rows(vc_ref, tq, r)], axis=0).astype(BF16))
        for h in heads:
            hm = (lane >= h * HEAD_DIM) & (lane < (h + 1) * HEAD_DIM)
            qh = jnp.where(hm, q, 0.0).astype(BF16)
            logits.append(lax.dot_general(qh, k, (((1,), (1,)), ((), ())), preferred_element_type=F32))
    probs, stats = [], []
    for n, s in enumerate(logits):
        s = jnp.where(dead, NEG, s + bias_scr[n % HEADS_PER_LANE_TILE])
        m = jnp.max(s, axis=-1, keepdims=True)
        e = jnp.exp(s - m)
        probs.append(e.astype(BF16))
        stats.append((m, jnp.sum(e, axis=-1, keepdims=True)))
    first = lane < HEAD_DIM
    for u, r in enumerate(res):
        outs, lses = [], []
        for h in heads:
            n = u * HEADS_PER_LANE_TILE + h
            m, l = stats[n]
            outs.append(jnp.dot(probs[n], vals[u], preferred_element_type=F32) / l)
            lses.append(m + jnp.log(l))
        o_val = jnp.where(first, outs[0], outs[1])
        lse_val = jnp.where(first, lses[0], lses[1])
        if d == 1:
            o_ref[...] = o_val
            lse_ref[...] = lse_val
        else:
            o_ref[pl.ds(r, tq, stride=d), :] = o_val
            lse_ref[pl.ds(r, tq, stride=d), :] = lse_val


def dil_prompt(qkv, batch, seq_len, gi, bias_k):
    win, d = DIL_GROUPS[gi]
    assert win // d == N_BACK and seq_len % (d * N_BACK) == 0
    sub = seq_len // d
    tq = min(512 if d < 16 else N_BACK, sub)
    assert sub % tq == 0 and tq % N_BACK == 0
    rp = max(1, min(d, 512 // tq))
    assert d % rp == 0
    hp_tiles = HD // LANES
    qkv3 = qkv.reshape(batch, seq_len, qkv.shape[1])
    width = 2 * tq + N_BACK
    step = N_BACK + tq - jnp.arange(width)
    seg = jnp.where(((step >= 0) & (step <= N_BACK))[:, None], bias_k[jnp.clip(step, 0, N_BACK)].astype(F32), NEG)
    seg = jnp.transpose(seg)[:, None, :]

    def col(which):
        return (which * N_GROUPS + gi) * hp_tiles

    cur = lambda which: pl.BlockSpec((None, tq * d, LANES), lambda hp, b, i, r: (b, i, col(which) + hp))
    prev = lambda which: pl.BlockSpec((None, N_BACK * d, LANES),
                                      lambda hp, b, i, r: (b, jnp.maximum(i * (tq // N_BACK) - 1, 0), col(which) + hp))
    out_spec = pl.BlockSpec((None, tq * d, LANES), lambda hp, b, i, r: (b, i, hp))
    o, lse = pl.pallas_call(
        functools.partial(_dil_prompt_kernel, tq=tq, d=d, rp=rp),
        out_shape=(jax.ShapeDtypeStruct((batch, seq_len, HD), F32),) * 2,
        grid=(hp_tiles, batch, sub // tq, d // rp),
        in_specs=[cur(0), cur(1), prev(1), cur(2), prev(2),
                  pl.BlockSpec((HEADS_PER_LANE_TILE, 1, width), lambda hp, b, i, r: (hp, 0, 0))],
        out_specs=(out_spec, out_spec),
        scratch_shapes=[pltpu.VMEM((HEADS_PER_LANE_TILE, tq, tq + N_BACK), F32)],
        compiler_params=_params(("parallel", "arbitrary", "arbitrary", "arbitrary")),
        name=f"dil_prompt_g{gi}",
    )(qkv3, qkv3, qkv3, qkv3, qkv3, seg)
    return o.reshape(batch * seq_len, HD), lse.reshape(batch * seq_len, HD)


def _dil_sample_kernel(ct_ref, q_ref, newt_ref, bias_c_ref, bias_n_ref, ot_ref, o_ref, lse_ref, *, hb, t_len, L):
    lane = lax.broadcasted_iota(jnp.int32, (HEAD_DIM, LANES), 1)
    nt = (((1,), (1,)), ((), ()))
    for hh in range(hb):
        q = (q_ref[hh] * SCALE).astype(BF16)
        kt, vt = ct_ref[0, hh], ct_ref[1, hh]
        knt, vnt = newt_ref[0, hh], newt_ref[1, hh]
        s_c = jnp.dot(q, kt.astype(BF16), preferred_element_type=F32) + bias_c_ref[hh]
        s_n = jnp.dot(q, knt.astype(BF16), preferred_element_type=F32) + bias_n_ref[hh]
        m = jnp.maximum(jnp.max(s_c, axis=1, keepdims=True), jnp.max(s_n, axis=1, keepdims=True))
        e_c = jnp.exp(s_c - m)
        e_n = jnp.exp(s_n - m)
        l = jnp.sum(e_c, axis=1, keepdims=True) + jnp.sum(e_n, axis=1, keepdims=True)
        pv = (lax.dot_general(e_c.astype(BF16), vt.astype(BF16), nt, preferred_element_type=F32)
              + lax.dot_general(e_n.astype(BF16), vnt.astype(BF16), nt, preferred_element_type=F32))
        o_ref[hh] = pv / l
        lse_ref[hh] = jnp.broadcast_to(m + jnp.log(l), (t_len, HEAD_DIM))
        for kv, (old, new) in enumerate(((kt, knt), (vt, vnt))):
            moved = pltpu.roll(old, L - t_len, 1)
            if L > LANES:
                ot_ref[kv, hh, :, :L - LANES] = moved[:, :L - LANES]
            ot_ref[kv, hh, :, L - LANES:] = jnp.where(lane >= LANES - t_len, new, moved[:, L - LANES:])


def dil_sample(cache, new_kv, q_s, dec_batch, t_len, gi, bias_k):
    win, d = DIL_GROUPS[gi]
    L = cache.shape[1]
    assert L == win and win // d == N_BACK and L % LANES == 0 and t_len <= LANES
    hb = min(N_HEADS, max(1, (4 << 20) // (2 * HEAD_DIM * L * 4)))
    assert N_HEADS % hb == 0
    ct = jnp.transpose(cache, (0, 2, 3, 4, 1))
    newt = jnp.pad(jnp.transpose(new_kv, (0, 2, 3, 4, 1)), ((0, 0),) * 4 + ((LANES - t_len, 0),))
    qh = jnp.transpose(q_s, (0, 2, 1, 3))
    tq = jnp.arange(t_len)[:, None]
    back = L + tq - jnp.arange(L)[None, :]
    ok_c = (back % d == 0) & (back // d <= N_BACK)
    bias_c = jnp.where(ok_c[:, :, None], bias_k[jnp.clip(back // d, 0, N_BACK)].astype(F32), NEG)
    t2 = jnp.arange(LANES)[None, :] - (LANES - t_len)
    ok_n = (t2 >= 0) & (t2 <= tq) & ((tq - t2) % d == 0)
    bias_n = jnp.where(ok_n[:, :, None], bias_k[jnp.clip((tq - t2) // d, 0, N_BACK)].astype(F32), NEG)
    bias_c = jnp.transpose(bias_c, (2, 0, 1))
    bias_n = jnp.transpose(bias_n, (2, 0, 1))
    cache_spec = pl.BlockSpec((None, 2, hb, HEAD_DIM, L), lambda b, j: (b, 0, j, 0, 0))
    row_spec = pl.BlockSpec((None, hb, t_len, HEAD_DIM), lambda b, j: (b, j, 0, 0))
    ot, o, lse = pl.pallas_call(
        functools.partial(_dil_sample_kernel, hb=hb, t_len=t_len, L=L),
        out_shape=(jax.ShapeDtypeStruct(ct.shape, F32),
                   jax.ShapeDtypeStruct(qh.shape, F32), jax.ShapeDtypeStruct(qh.shape, F32)),
        grid=(dec_batch, N_HEADS // hb),
        in_specs=[
            cache_spec,
            row_spec,
            pl.BlockSpec((None, 2, hb, HEAD_DIM, LANES), lambda b, j: (b, 0, j, 0, 0)),
            pl.BlockSpec((hb, t_len, L), lambda b, j: (j, 0, 0)),
            pl.BlockSpec((hb, t_len, LANES), lambda b, j: (j, 0, 0)),
        ],
        out_specs=(cache_spec, row_spec, row_spec),
        compiler_params=_params(("parallel", "parallel")),
        name=f"dil_sample_g{gi}",
    )(ct, qh, newt, bias_c, bias_n)
    to_rows = lambda a: jnp.transpose(a, (0, 2, 1, 3)).reshape(dec_batch * t_len, HD)
    return to_rows(o), to_rows(lse), jnp.transpose(ot, (0, 4, 1, 2, 3))


def _block_mean_kernel(k_ref, o_ref):
    o_ref[...] = jnp.broadcast_to(jnp.sum(k_ref[...], axis=0, keepdims=True) * (1.0 / MOBA_BLOCK), o_ref.shape)


def block_means(kvb, n_blocks_total):
    out = pl.pallas_call(
        _block_mean_kernel,
        out_shape=jax.ShapeDtypeStruct((n_blocks_total, 8, HD), F32),
        grid=(n_blocks_total,),
        in_specs=[pl.BlockSpec((MOBA_BLOCK, HD), lambda i: (i, 0))],
        out_specs=pl.BlockSpec((None, 8, HD), lambda i: (i, 0, 0)),
        compiler_params=_params(("parallel",)),
        name="block_means",
    )(kvb)
    return out[:, 0]


def _top_blocks(sc, blk, n_valid, n_sel, axis):
    sel = jnp.zeros(sc.shape, F32)
    big = float(sc.shape[axis])
    for it in range(n_sel):
        mx = jnp.max(sc, axis=axis, keepdims=True)
        idx = jnp.min(jnp.where(sc == mx, blk, big), axis=axis, keepdims=True)
        hit = blk == idx
        sel = jnp.maximum(sel, jnp.where(hit, jnp.where(it < n_valid, 1.0, 0.0), 0.0))
        sc = jnp.where(hit, -jnp.inf, sc)
    return sel


def _moba_prompt_kernel(q_ref, k_ref, vt_ref, km_ref, seg_ref, o_ref,
                        m_scr, l_scr, acc_scr, selb_scr, bias_scr, *, nb, n_sel, kb):
    b = pl.program_id(1)
    i = pl.program_id(2)
    tb = MOBA_BLOCK

    def head_mask(shape, axis, h):
        idx = lax.broadcasted_iota(jnp.int32, shape, axis)
        return (idx >= h * HEAD_DIM) & (idx < (h + 1) * HEAD_DIM)

    @pl.when((b == 0) & (i == 0))
    def _():
        row = lax.broadcasted_iota(jnp.int32, (tb, tb), 0)
        colm = lax.broadcasted_iota(jnp.int32, (tb, tb), 1)
        for h in range(HEADS_PER_LANE_TILE):
            def fill(delta, carry, h=h):
                bias_scr[h, delta] = _toeplitz(seg_ref[h, pl.ds(delta, 1), :], tb, tb, tb) * LOG2E
                return carry
            lax.fori_loop(0, nb, fill, 0)
            bias_scr[h, 0] = jnp.where(colm >= row, bias_scr[h, 0], NEG)

    heads = range(HEADS_PER_LANE_TILE)
    m_scr[...] = jnp.full_like(m_scr, NEG)
    l_scr[...] = jnp.zeros_like(l_scr)
    acc_scr[...] = jnp.zeros_like(acc_scr)
    q_raw = q_ref[...]
    km = km_ref[...]
    blk = lax.broadcasted_iota(jnp.int32, (nb, tb), 0).astype(F32)
    i_f = i.astype(F32)
    for h in heads:
        kmh = jnp.where(head_mask(km.shape, 1, h), km, 0.0)
        sc = lax.dot_general(kmh, q_raw, (((1,), (1,)), ((), ())), precision=HIGHEST, preferred_element_type=F32)
        sc = jnp.where(blk < i_f, sc, -jnp.inf)
        sel = _top_blocks(sc, blk, i, n_sel, axis=0) + jnp.where(blk == i_f, 1.0, 0.0)
        selb_scr[h] = jnp.where(sel > 0.5, 0.0, NEG)

    q = q_raw * (SCALE * LOG2E)
    qhs = [jnp.where(head_mask(q.shape, 1, h), q, 0.0).astype(BF16) for h in heads]

    def key_tile(kt):
        k0 = pl.multiple_of(kt * (kb * tb), kb * tb)
        k = k_ref[pl.ds(k0, kb * tb), :]
        vt = vt_ref[:, pl.ds(k0, kb * tb)]
        raw = [[lax.dot_general(k[c * tb:(c + 1) * tb], qhs[h], (((1,), (1,)), ((), ())),
                                preferred_element_type=F32) for c in range(kb)] for h in heads]
        parts, shifts, alphas = [], [], []
        for h in heads:
            ps, sbs, cms = [], [], []
            for c in range(kb):
                jj = kt * kb + c
                p = raw[h][c] + bias_scr[h, jnp.maximum(i - jj, 0)]
                sb = selb_scr[h, pl.ds(jj, 1), :]
                ps.append(p)
                sbs.append(sb)
                cms.append(jnp.max(p, axis=0, keepdims=True) + sb)
            m_old = m_scr[h]
            m_new = jnp.maximum(m_old, functools.reduce(jnp.maximum, cms))
            parts.append(ps)
            shifts.append([m_new - sb for sb in sbs])
            alphas.append(jnp.exp2(m_old - m_new))
            m_scr[h] = m_new
        for h in heads:
            acc = alphas[h] * acc_scr[h]
            l_new = alphas[h] * l_scr[h]
            for c in range(kb):
                e = jnp.exp2(parts[h][c] - shifts[h][c])
                l_new = l_new + jnp.sum(e, axis=0, keepdims=True)
                acc = acc + jnp.dot(vt[h * HEAD_DIM:(h + 1) * HEAD_DIM, c * tb:(c + 1) * tb], e.astype(BF16),
                                    preferred_element_type=F32)
            l_scr[h] = l_new
            acc_scr[h] = acc

    own_tile = i // kb
    key_tile(own_tile)

    def past_tile(kt, carry):
        key_tile(kt)
        return carry

    lax.fori_loop(0, own_tile, past_tile, 0)
    o_t = jnp.concatenate([acc_scr[h] / l_scr[h] for h in heads], axis=0)
    o_ref[...] = o_t.T


def moba_prompt(q, kvb, k_means, tab, batch, seq_len):
    tb = MOBA_BLOCK
    kb = 4
    assert seq_len % (kb * tb) == 0
    nb = seq_len // tb
    n_sel = min(MOBA_TOPK, nb)
    hp_tiles = HD // LANES
    dist = jnp.arange(nb)[:, None] * tb + jnp.arange(2 * tb)[None, :] - tb
    seg = tab.T.astype(F32)[:, jnp.clip(dist, 0, tab.shape[0] - 1)]
    km3 = k_means.reshape(batch, nb, HD)
    kv16 = kvb.astype(BF16)
    v_t = kv16[:, HD:].T
    return pl.pallas_call(
        functools.partial(_moba_prompt_kernel, nb=nb, n_sel=n_sel, kb=kb),
        out_shape=jax.ShapeDtypeStruct(q.shape, F32),
        grid=(hp_tiles, batch, nb),
        in_specs=[
            pl.BlockSpec((tb, LANES), lambda hp, b, i: (b * nb + i, hp)),
            pl.BlockSpec((seq_len, LANES), lambda hp, b, i: (b, hp)),
            pl.BlockSpec((LANES, seq_len), lambda hp, b, i: (hp, b)),
            pl.BlockSpec((None, nb, LANES), lambda hp, b, i: (b, 0, hp)),
            pl.BlockSpec((HEADS_PER_LANE_TILE, nb, 2 * tb), lambda hp, b, i: (hp, 0, 0)),
        ],
        out_specs=pl.BlockSpec((tb, LANES), lambda hp, b, i: (b * nb + i, hp)),
        scratch_shapes=[pltpu.VMEM((HEADS_PER_LANE_TILE, 1, tb), F32),
                        pltpu.VMEM((HEADS_PER_LANE_TILE, 1, tb), F32),
                        pltpu.VMEM((HEADS_PER_LANE_TILE, HEAD_DIM, tb), F32),
                        pltpu.VMEM((HEADS_PER_LANE_TILE, nb, tb), F32),
                        pltpu.VMEM((HEADS_PER_LANE_TILE, nb, tb, tb), F32)],
        compiler_params=_params(("parallel", "arbitrary", "arbitrary")),
        name="moba_prompt",
    )(q, kv16, v_t, km3, seg)


def _moba_page_kernel(pt_ref, *rest, t_len, pp):
    del pt_ref
    pool_refs, (qbd_ref, bias_ref, dmask_ref, ksum_ref, m_ref, l_ref, acc_ref) = rest[:pp], rest[pp:]
    j = pl.program_id(1)
    qbd = qbd_ref[...].astype(BF16)
    dmask = dmask_ref[...]
    kts = [ref[0] for ref in pool_refs]
    logits = [jnp.dot(qbd, kt.astype(BF16), preferred_element_type=F32) for kt in kts]
    ms, ls, es = [], [], []
    for u, s in enumerate(logits):
        s = s + bias_ref[u]
        m = jnp.max(s, axis=1, keepdims=True)
        e = jnp.exp(s - m)
        ms.append(m)
        ls.append(jnp.sum(e, axis=1, keepdims=True))
        es.append(e.astype(BF16))
    for u in range(pp):
        pv = lax.dot_general(es[u], pool_refs[u][1].astype(BF16), (((1,), (1,)), ((), ())),
                             preferred_element_type=F32)
        rows = [jnp.sum(pv[t * N_HEADS:(t + 1) * N_HEADS, :] * dmask, axis=0, keepdims=True) for t in range(t_len)]
        acc_ref[u * t_len:(u + 1) * t_len, :] = jnp.concatenate(rows, axis=0)

    @pl.when(j == 0)
    def _():
        ksum_ref[...] = jnp.zeros_like(ksum_ref)
        m_ref[...] = jnp.zeros_like(m_ref)
        l_ref[...] = jnp.zeros_like(l_ref)

    def put(ref, cols):
        lane = lax.broadcasted_iota(jnp.int32, ref.shape, 1)
        val = ref[...]
        for u, col in enumerate(cols):
            val = jnp.where(lane == j * pp + u, col, val)
        ref[...] = val

    put(ksum_ref, [jnp.sum(kt, axis=1, keepdims=True) for kt in kts])
    put(m_ref, ms)
    put(l_ref, ls)


def _moba_combine_kernel(ks_ref, m_ref, l_ref, acc_ref, qbd_ref, knt_ref, vnew_ref, bias_ref, pair_ref, dmask_ref,
                         o_ref, *, n_pages, n_blocks, n_sel, t_len):
    qbd = qbd_ref[...]
    pair = pair_ref[...]
    sc_pages = jnp.dot(qbd, ks_ref[...], precision=HIGHEST, preferred_element_type=F32)
    sc = jnp.dot(sc_pages, pair, precision=HIGHEST, preferred_element_type=F32) * (1.0 / MOBA_BLOCK)
    lane = lax.broadcasted_iota(jnp.int32, sc.shape, 1).astype(F32)
    sc = jnp.where(lane < n_blocks, sc, -jnp.inf)
    sel = _top_blocks(sc, lane, n_sel, n_sel, axis=1)
    picked = lax.dot_general(sel, pair, (((1,), (1,)), ((), ())), preferred_element_type=F32) > 0.5
    m_p = m_ref[...]
    s_own = jnp.dot(qbd.astype(BF16), knt_ref[...].astype(BF16), preferred_element_type=F32) + bias_ref[...]
    m_tot = jnp.maximum(jnp.max(jnp.where(picked, m_p, NEG), axis=1, keepdims=True),
                        jnp.max(s_own, axis=1, keepdims=True))
    w = jnp.where(picked, jnp.exp(m_p - m_tot), 0.0)
    e_own = jnp.exp(s_own - m_tot)
    l_tot = jnp.sum(w * l_ref[...], axis=1, keepdims=True) + jnp.sum(e_own, axis=1, keepdims=True)
    dmask = dmask_ref[...]
    vnew = vnew_ref[...].astype(BF16)
    rows = []
    for t in range(t_len):
        hs = slice(t * N_HEADS, (t + 1) * N_HEADS)
        a = jnp.dot(w[hs, :n_pages], acc_ref[t], precision=HIGHEST, preferred_element_type=F32)
        a = a + jnp.dot(e_own[hs].astype(BF16), vnew, preferred_element_type=F32)
        num = jnp.sum(a * dmask, axis=0, keepdims=True)
        den = jnp.sum(l_tot[hs] * dmask, axis=0, keepdims=True)
        rows.append(num / den)
    o_ref[...] = jnp.concatenate(rows, axis=0)


def moba_sample(q_s, kvb_s, pool, page_table, tab, dec_batch, t_len):
    n_pool, page = pool.shape[0], pool.shape[1]
    n_pages = page_table.shape[1]
    past = n_pages * page
    assert MOBA_BLOCK % page == 0 and past % MOBA_BLOCK == 0 and t_len <= MOBA_BLOCK and t_len == 8
    ppb = MOBA_BLOCK // page
    n_blocks = past // MOBA_BLOCK
    n_sel = min(MOBA_TOPK, n_blocks + 1)
    assert n_blocks >= n_sel, "fewer cached blocks than top-k picks is not supported"
    n_col = t_len * N_HEADS
    assert n_col == LANES and page == LANES and n_pages <= LANES
    pool_t = jnp.transpose(pool, (0, 2, 3, 4, 1)).reshape(n_pool, 2, HD, page)
    q4 = q_s.reshape(dec_batch, t_len, N_HEADS, HEAD_DIM) * SCALE
    qbd = jnp.einsum('bthx,hg->bthgx', q4, jnp.eye(N_HEADS, dtype=F32)).reshape(dec_batch, n_col, HD)
    by_t = jnp.stack([jnp.flip(tab.astype(F32)[t + 1:past + t + 1], axis=0) for t in range(t_len)])
    bias_pages = jnp.transpose(by_t.reshape(t_len, n_pages, page, N_HEADS), (1, 0, 3, 2)).reshape(n_pages, n_col, page)
    d_own = jnp.arange(t_len)[:, None] - jnp.arange(LANES)[None, :]
    ok_own = (d_own >= 0) & (jnp.arange(LANES)[None, :] < t_len)
    bias_own = jnp.where(ok_own[:, :, None], tab.astype(F32)[jnp.clip(d_own, 0, t_len)], NEG)
    bias_own = jnp.swapaxes(bias_own, 1, 2).reshape(n_col, LANES)
    dmask = (jnp.arange(N_HEADS)[:, None] == (jnp.arange(HD) // HEAD_DIM)[None, :]).astype(F32)
    pair = ((jnp.arange(LANES)[:, None] // ppb == jnp.arange(LANES)[None, :])
            & (jnp.arange(LANES)[:, None] < n_pages)).astype(F32)
    new3 = kvb_s.reshape(dec_batch, t_len, 2 * HD)
    knt = jnp.pad(jnp.swapaxes(new3[:, :, :HD], 1, 2), ((0, 0), (0, 0), (0, LANES - t_len)))
    vnew = jnp.pad(new3[:, :, HD:], ((0, 0), (0, LANES - t_len), (0, 0)))

    pp = 4 if n_pages % 4 == 0 else 1
    page_spec = lambda u: pl.BlockSpec((None, 2, HD, page), lambda b, j, pt: (pt[b, j * pp + u], 0, 0, 0))
    stat = lambda rows: pl.BlockSpec((None, rows, LANES), lambda b, j, pt: (b, 0, 0))
    ksum, m_p, l_p, acc = pl.pallas_call(
        functools.partial(_moba_page_kernel, t_len=t_len, pp=pp),
        out_shape=(jax.ShapeDtypeStruct((dec_batch, HD, LANES), F32),
                   jax.ShapeDtypeStruct((dec_batch, n_col, LANES), F32),
                   jax.ShapeDtypeStruct((dec_batch, n_col, LANES), F32),
                   jax.ShapeDtypeStruct((dec_batch, n_pages * t_len, HD), F32)),
        grid_spec=pltpu.PrefetchScalarGridSpec(
            num_scalar_prefetch=1,
            grid=(dec_batch, n_pages // pp),
            in_specs=[page_spec(u) for u in range(pp)] + [
                pl.BlockSpec((None, n_col, HD), lambda b, j, pt: (b, 0, 0)),
                pl.BlockSpec((pp, n_col, page), lambda b, j, pt: (j, 0, 0)),
                pl.BlockSpec((N_HEADS, HD), lambda b, j, pt: (0, 0)),
            ],
            out_specs=(stat(HD), stat(n_col), stat(n_col),
                       pl.BlockSpec((None, pp * t_len, HD), lambda b, j, pt: (b, j, 0))),
        ),
        compiler_params=_params(("parallel", "arbitrary")),
        name="moba_page",
    )(page_table, *([pool_t] * pp), qbd, bias_pages, dmask)

    acc = jnp.swapaxes(acc.reshape(dec_batch, n_pages, t_len, HD), 1, 2)
    per_b = lambda shape: pl.BlockSpec((None,) + shape, lambda b: (b,) + (0,) * len(shape))
    const = lambda shape: pl.BlockSpec(shape, lambda b: (0,) * len(shape))
    out = pl.pallas_call(
        functools.partial(_moba_combine_kernel, n_pages=n_pages, n_blocks=n_blocks, n_sel=n_sel, t_len=t_len),
        out_shape=jax.ShapeDtypeStruct((dec_batch, t_len, HD), F32),
        grid=(dec_batch,),
        in_specs=[per_b((HD, LANES)), per_b((n_col, LANES)), per_b((n_col, LANES)),
                  per_b((t_len, n_pages, HD)), per_b((n_col, HD)), per_b((HD, LANES)), per_b((LANES, HD)),
                  const((n_col, LANES)), const((LANES, LANES)), const((N_HEADS, HD))],
        out_specs=per_b((t_len, HD)),
        compiler_params=_params(("parallel",)),
        name="moba_combine",
    )(ksum, m_p, l_p, acc, qbd, knt, vnew, bias_own, pair, dmask)
    return out.reshape(dec_batch * t_len, HD)


def kernel(x_prompt, x_sample, cache_a_kv_g0, cache_a_kv_g1, cache_a_kv_g2, cache_b_kv_pool, page_table, state_ffn_conv, rel_bias, norm_attn, norm_ffn, w_qkv_a, q_norm_a, k_norm_a, w_o_a, norm_kv, w_kv_b, k_norm_b, w_q_b, q_norm_b, w_o_b, w_up, conv_w, conv_b, w_down):
    B, S, D = x_prompt.shape
    DB, T, _ = x_sample.shape
    depth = norm_attn.shape[0]
    n_self = w_qkv_a.shape[0]
    caches = (cache_a_kv_g0, cache_a_kv_g1, cache_a_kv_g2)
    past = page_table.shape[1] * cache_b_kv_pool.shape[1]
    assert D == HD

    group_bias = [rel_bias[_rel_bucket(dil * jnp.arange(win // dil + 1))] for (win, dil) in DIL_GROUPS]
    tab = rel_bias[_rel_bucket(jnp.arange(max(S, past + T)))]

    ones =jnp.ones((HD,), F32)
    tile_h = lambda gvec, n: jnp.tile(gvec, n * N_HEADS)

    def self_layer(x, l, prompt):
        hg = jnp.concatenate([tile_h(q_norm_a[l], N_GROUPS), tile_h(k_norm_a[l], N_GROUPS), jnp.tile(ones, N_GROUPS)])
        qkv = norm_matmul(x, norm_attn[l], w_qkv_a[l].astype(BF16), hg, 2 * N_GROUPS * HD)
        outs, lses, bufs = [], [], []
        seq = S if prompt else T
        qkv3 = qkv.reshape(-1, seq, qkv.shape[1])
        for gi, (win, dil) in enumerate(DIL_GROUPS):
            keep = min(win, seq)
            part = lambda which: qkv3[:, seq - keep:, (which * N_GROUPS + gi) * HD:(which * N_GROUPS + gi + 1) * HD
                                      ].reshape(-1, keep, N_HEADS, HEAD_DIM)
            new_kv = jnp.stack([part(1), part(2)], axis=2)
            if prompt:
                o, lse = dil_prompt(qkv, B, S, gi, group_bias[gi])
                bufs.append(new_kv)
            else:
                o, lse, buf = dil_sample(caches[gi][l], new_kv, part(0), DB, T, gi, group_bias[gi])
                bufs.append(buf)
            outs.append(o)
            lses.append(lse)
        return merge_proj(outs, lses, w_o_a[l].astype(BF16), x), bufs

    def ffn(x, l, prompt):
        wup, wdn = w_up[l].astype(BF16), w_down[l].astype(BF16)
        if prompt:
            tm = 256
            y, ut = ffn_prompt(x, S, norm_ffn[l], wup, conv_w[l], conv_b[l], wdn, tm=tm)
            cs = ut.reshape(B, S // tm, 8, ut.shape[1])[:, -1, 8 - (CONV_W - 1):]
        else:
            y, ua, ub = ffn_sample(x, T, state_ffn_conv[l], norm_ffn[l], wup, conv_w[l], conv_b[l], wdn)
            u = jnp.concatenate([ua, ub], axis=1)
            cs = u.reshape(DB, T, u.shape[1])[:, T - (CONV_W - 1):]
        return y, cs

    def trunk(x, prompt):
        nb_rows = B if prompt else DB
        a_new = [[] for _ in range(N_GROUPS)]
        conv_new = []
        kvb = None
        k_means = None
        for l in range(depth):
            if l < n_self:
                x, bufs = self_layer(x, l, prompt)
                for gi in range(N_GROUPS):
                    a_new[gi].append(bufs[gi])
            else:
                if l == n_self:
                    hg = jnp.concatenate([tile_h(k_norm_b, 1), ones])
                    kvb = norm_matmul(x, norm_kv, w_kv_b.astype(BF16), hg, HD)
                    if prompt:
                        k_means = block_means(kvb, B * S // MOBA_BLOCK)
                lb = l - n_self
                q = norm_matmul(x, norm_attn[l], w_q_b[lb].astype(BF16), tile_h(q_norm_b[lb], 1), HD)
                if prompt:
                    o = moba_prompt(q, kvb, k_means, tab, B, S)
                else:
                    o = moba_sample(q, kvb, cache_b_kv_pool, page_table, tab, DB, T)
                x = merge_proj([o], None, w_o_b[lb].astype(BF16), x)
            x, cs = ffn(x, l, prompt)
            conv_new.append(cs)
        seq = S if prompt else T
        kv_new = kvb.reshape(nb_rows, seq, 2, N_HEADS, HEAD_DIM)
        return x.reshape(nb_rows, seq, D), [jnp.stack(a, 0) for a in a_new], kv_new, jnp.stack(conv_new, 0)

    y_p, a_p, kv_p, conv_p = trunk(x_prompt.reshape(B * S, D), True)
    y_s, a_s, kv_s, conv_s = trunk(x_sample.reshape(DB * T, D), False)
    return (y_p, y_s, a_p[0], a_p[1], a_p[2], a_s[0], a_s[1], a_s[2], kv_p, kv_s, conv_p, conv_s)
```

```python
import functools
import math

import jax
import jax.numpy as jnp
from jax import lax
from jax.experimental import pallas as pl
from jax.experimental.pallas import tpu as pltpu

HEAD_DIM = 64
N_HEADS = 16
HD = N_HEADS * HEAD_DIM
DIL_GROUPS = ((128, 1), (512, 4), (2048, 16))
N_GROUPS = len(DIL_GROUPS)
N_BACK = 128
MOBA_BLOCK = 256
MOBA_TOPK = 3
NUM_BUCKETS = 32
REL_MAX_DIST = 4096
CONV_W = 3
EPS = 1e-6
SCALE = HEAD_DIM ** -0.5
NEG = -1e30
LOG2E = 1.4426950408889634
LANES = 128
MXU_TILE = 256
HEADS_PER_LANE_TILE = LANES // HEAD_DIM
VMEM_LIMIT = 56 * 1024 * 1024

F32 = jnp.float32
BF16 = jnp.bfloat16
HIGHEST = lax.Precision.HIGHEST


def _params(sem, vmem=VMEM_LIMIT):
    return pltpu.CompilerParams(dimension_semantics=sem, vmem_limit_bytes=vmem)


def _const_spec(shape):
    nd = len(shape)
    return pl.BlockSpec(shape, lambda *_: (0,) * nd, pipeline_mode=pl.Buffered(1))


def _rel_bucket(dist):
    n = jnp.maximum(dist, 0)
    exact = NUM_BUCKETS // 2
    nf = jnp.maximum(n, exact).astype(F32)
    large = exact + (jnp.log(nf / exact) / math.log(REL_MAX_DIST / exact) * (NUM_BUCKETS - exact)).astype(jnp.int32)
    return jnp.where(n < exact, n, jnp.minimum(large, NUM_BUCKETS - 1))


def _norm_matmul_kernel(x_ref, g_ref, w_ref, hg_ref, bd_ref, o_ref, h_scr, *, n_norm_tiles):
    j = pl.program_id(1)

    @pl.when(j == 0)
    def _():
        x = x_ref[...]
        r = lax.rsqrt(jnp.mean(x * x, axis=-1, keepdims=True) + EPS)
        h_scr[...] = (x * r * g_ref[...]).astype(BF16)

    y = jnp.dot(h_scr[...], w_ref[...], preferred_element_type=F32)

    @pl.when(j < n_norm_tiles)
    def _():
        y2 = (y * y).astype(BF16)
        ss = jnp.concatenate([jnp.dot(y2[:, c:c + MXU_TILE], bd_ref[...], preferred_element_type=F32)
                              for c in range(0, y.shape[1], MXU_TILE)], axis=1)
        o_ref[...] = y * lax.rsqrt(ss * (1.0 / HEAD_DIM) + EPS) * hg_ref[...]

    @pl.when(j >= n_norm_tiles)
    def _():
        o_ref[...] = y


def norm_matmul(x, g, w_bf16, head_gain, n_norm_cols, *, tn=512):
    M, K = x.shape
    N = w_bf16.shape[1]
    tm = min(1024, M)
    assert M % tm == 0 and N % tn == 0 and n_norm_cols % tn == 0 and tn % MXU_TILE == 0
    hid = jnp.arange(MXU_TILE) // HEAD_DIM
    bd = (hid[:, None] == hid[None, :]).astype(BF16)
    return pl.pallas_call(
        functools.partial(_norm_matmul_kernel, n_norm_tiles=n_norm_cols // tn),
        out_shape=jax.ShapeDtypeStruct((M, N), F32),
        grid=(M // tm, N // tn),
        in_specs=[
            pl.BlockSpec((tm, K), lambda i, j: (i, 0)),
            pl.BlockSpec((1, K), lambda i, j: (0, 0)),
            pl.BlockSpec((K, tn), lambda i, j: (0, j)),
            pl.BlockSpec((1, tn), lambda i, j: (0, j)),
            pl.BlockSpec((MXU_TILE, MXU_TILE), lambda i, j: (0, 0)),
        ],
        out_specs=pl.BlockSpec((tm, tn), lambda i, j: (i, j)),
        scratch_shapes=[pltpu.VMEM((tm, K), BF16)],
        compiler_params=_params(("parallel", "arbitrary")),
        name="norm_matmul",
    )(x, g.reshape(1, K), w_bf16, head_gain.reshape(1, N), bd)


def _merge_proj_kernel(*refs, n_g):
    o_refs = refs[:n_g]
    lse_refs = refs[n_g:2 * n_g] if n_g > 1 else ()
    w_ref, x_ref, out_ref = refs[-3:]
    if n_g == 1:
        a = o_refs[0][...]
    else:
        ls = [r[...] for r in lse_refs]
        mx = functools.reduce(jnp.maximum, ls)
        ws = [jnp.exp(l - mx) for l in ls]
        den = functools.reduce(lambda p, q: p + q, ws)
        num = functools.reduce(lambda p, q: p + q, [w * r[...] for w, r in zip(ws, o_refs)])
        a = num / den
    out_ref[...] = x_ref[...] + jnp.dot(a.astype(BF16), w_ref[...], preferred_element_type=F32)


def merge_proj(os_, lses, w_bf16, x):
    M, D = x.shape
    n_g = len(os_)
    tm = min(256, M)
    assert M % tm == 0
    row = pl.BlockSpec((tm, D), lambda i: (i, 0))
    ins = list(os_) + (list(lses) if n_g > 1 else [])
    return pl.pallas_call(
        functools.partial(_merge_proj_kernel, n_g=n_g),
        out_shape=jax.ShapeDtypeStruct((M, D), F32),
        grid=(M // tm,),
        in_specs=[row] * len(ins) + [_const_spec(w_bf16.shape), row],
        out_specs=row,
        compiler_params=_params(("parallel",)),
        name="merge_proj",
    )(*ins, w_bf16, x)


def _silu_gate(ca, cb):
    return (ca / (1.0 + jnp.exp(-ca))) * cb


def _ffn_prompt_kernel(x_ref, xp_ref, g_ref, wup_ref, cw_ref, cb_ref, wdn_ref, y_ref, ut_ref,
                       ua_scr, ub_scr, gate_scr, *, tm, ck, d_ff, tiles_per_seq):
    i = pl.program_id(0)
    x = x_ref[...]
    keep = jnp.where(i % tiles_per_seq == 0, 0.0, 1.0)
    xc = jnp.concatenate([xp_ref[...] * keep, x], axis=0)
    r = lax.rsqrt(jnp.mean(xc * xc, axis=-1, keepdims=True) + EPS)
    h = (xc * r * g_ref[...]).astype(BF16)
    for c in range(d_ff // ck):
        a0, b0 = c * ck, d_ff + c * ck
        ua_scr[...] = jnp.dot(h, wup_ref[:, a0:a0 + ck], preferred_element_type=F32)
        ub_scr[...] = jnp.dot(h, wup_ref[:, b0:b0 + ck], preferred_element_type=F32)
        ut_ref[:, a0:a0 + ck] = ua_scr[tm:tm + 8, :]
        ut_ref[:, b0:b0 + ck] = ub_scr[tm:tm + 8, :]
        ca = cb_ref[:, a0:a0 + ck]
        cb = cb_ref[:, b0:b0 + ck]
        for j in range(CONV_W):
            ca = ca + cw_ref[j:j + 1, a0:a0 + ck] * ua_scr[6 + j:6 + j + tm, :]
            cb = cb + cw_ref[j:j + 1, b0:b0 + ck] * ub_scr[6 + j:6 + j + tm, :]
        gate_scr[:, a0:a0 + ck] = _silu_gate(ca, cb).astype(BF16)
    y_ref[...] = x + jnp.dot(gate_scr[...], wdn_ref[...], preferred_element_type=F32)


def ffn_prompt(x, seq_len, g, wup_bf16, cw, cb, wdn_bf16, *, tm=256, ck=256):
    M, D = x.shape
    d_ff = wdn_bf16.shape[0]
    assert seq_len % tm == 0 and M % seq_len == 0 and d_ff % ck == 0 and CONV_W - 1 <= 8
    n_tiles = M // tm
    y, ut = pl.pallas_call(
        functools.partial(_ffn_prompt_kernel, tm=tm, ck=ck, d_ff=d_ff, tiles_per_seq=seq_len // tm),
        out_shape=(jax.ShapeDtypeStruct((M, D), F32), jax.ShapeDtypeStruct((n_tiles * 8, 2 * d_ff), F32)),
        grid=(n_tiles,),
        in_specs=[
            pl.BlockSpec((tm, D), lambda i: (i, 0)),
            pl.BlockSpec((8, D), lambda i: (jnp.maximum(i * (tm // 8) - 1, 0), 0)),
            _const_spec((1, D)),
            _const_spec(wup_bf16.shape),
            _const_spec(cw.shape),
            _const_spec((1, 2 * d_ff)),
            _const_spec(wdn_bf16.shape),
        ],
        out_specs=(pl.BlockSpec((tm, D), lambda i: (i, 0)), pl.BlockSpec((8, 2 * d_ff), lambda i: (i, 0))),
        scratch_shapes=[pltpu.VMEM((tm + 8, ck), F32), pltpu.VMEM((tm + 8, ck), F32), pltpu.VMEM((tm, d_ff), BF16)],
        compiler_params=_params(("parallel",)),
        name="ffn_prompt",
    )(x, x, g.reshape(1, D), wup_bf16, cw, cb.reshape(1, 2 * d_ff), wdn_bf16)
    return y, ut


def _ffn_sample_kernel(x_ref, g_ref, wa_ref, wb_ref, cwa_ref, cwb_ref, cba_ref, cbb_ref,
                       e1a_ref, e1b_ref, e0a_ref, e0b_ref, wdn_ref, y_ref, ua_ref, ub_ref,
                       h_scr, ua_scr, ub_scr, *, m, t_len):
    c = pl.program_id(0)

    @pl.when(c == 0)
    def _():
        x = x_ref[...]
        r = lax.rsqrt(jnp.mean(x * x, axis=-1, keepdims=True) + EPS)
        h_scr[...] = (x * r * g_ref[...]).astype(BF16)
        y_ref[...] = x
        ua_scr[0:8, :] = jnp.zeros((8, ua_scr.shape[1]), F32)
        ub_scr[0:8, :] = jnp.zeros((8, ub_scr.shape[1]), F32)

    h = h_scr[...]
    ua = jnp.dot(h, wa_ref[...], preferred_element_type=F32)
    ub = jnp.dot(h, wb_ref[...], preferred_element_type=F32)
    ua_ref[...] = ua
    ub_ref[...] = ub
    ua_scr[8:8 + m, :] = ua
    ub_scr[8:8 + m, :] = ub
    t = lax.broadcasted_iota(jnp.int32, ua.shape, 0) % t_len

    def conv(u, u_scr, cw_ref, cb_ref, e1_ref, e0_ref):
        um1 = jnp.where(t >= 1, u_scr[7:7 + m, :], e1_ref[...])
        um2 = jnp.where(t >= 2, u_scr[6:6 + m, :], jnp.where(t == 1, e1_ref[...], e0_ref[...]))
        return cb_ref[...] + cw_ref[0:1, :] * um2 + cw_ref[1:2, :] * um1 + cw_ref[2:3, :] * u

    ca = conv(ua, ua_scr, cwa_ref, cba_ref, e1a_ref, e0a_ref)
    cb = conv(ub, ub_scr, cwb_ref, cbb_ref, e1b_ref, e0b_ref)
    gt = _silu_gate(ca, cb).astype(BF16)
    y_ref[...] += jnp.dot(gt, wdn_ref[...], preferred_element_type=F32)


def ffn_sample(x, t_len, prev, g, wup_bf16, cw, cb, wdn_bf16, *, ck=256):
    M, D = x.shape
    d_ff = wdn_bf16.shape[0]
    assert d_ff % ck == 0 and CONV_W == 3 and t_len >= 2
    nc = d_ff // ck
    e1 = jnp.repeat(prev[:, 1], t_len, axis=0)
    e0 = jnp.repeat(prev[:, 0], t_len, axis=0)
    cb2 = cb.reshape(1, 2 * d_ff)
    a_col = lambda c: (0, c)
    b_col = lambda c: (0, nc + c)
    return pl.pallas_call(
        functools.partial(_ffn_sample_kernel, m=M, t_len=t_len),
        out_shape=(jax.ShapeDtypeStruct((M, D), F32), jax.ShapeDtypeStruct((M, d_ff), F32),
                   jax.ShapeDtypeStruct((M, d_ff), F32)),
        grid=(nc,),
        in_specs=[
            pl.BlockSpec((M, D), lambda c: (0, 0)),
            pl.BlockSpec((1, D), lambda c: (0, 0)),
            pl.BlockSpec((D, ck), a_col), pl.BlockSpec((D, ck), b_col),
            pl.BlockSpec((CONV_W, ck), a_col), pl.BlockSpec((CONV_W, ck), b_col),
            pl.BlockSpec((1, ck), a_col), pl.BlockSpec((1, ck), b_col),
            pl.BlockSpec((M, ck), a_col), pl.BlockSpec((M, ck), b_col),
            pl.BlockSpec((M, ck), a_col), pl.BlockSpec((M, ck), b_col),
            pl.BlockSpec((ck, D), lambda c: (c, 0)),
        ],
        out_specs=(pl.BlockSpec((M, D), lambda c: (0, 0)),
                   pl.BlockSpec((M, ck), a_col), pl.BlockSpec((M, ck), a_col)),
        scratch_shapes=[pltpu.VMEM((M, D), BF16), pltpu.VMEM((M + 8, ck), F32), pltpu.VMEM((M + 8, ck), F32)],
        compiler_params=_params(("arbitrary",)),
        name="ffn_sample",
    )(x, g.reshape(1, D), wup_bf16, wup_bf16, cw, cw, cb2, cb2, e1, e1, e0, e0, wdn_bf16)


def _toeplitz(seg_row, rows, lo, width):
    x = jnp.broadcast_to(seg_row, (rows, seg_row.shape[1]))
    return pltpu.roll(x, 0, 1, stride=1, stride_axis=0)[:, lo:lo + width]


DIL_UNIT = 4


def _dil_prompt_kernel(q_ref, kc_ref, kp_ref, vc_ref, vp_ref, seg_ref, o_ref, lse_ref, bias_scr, kcat_scr, vcat_scr,
                       *, d, n_jc):
    b = pl.program_id(1)
    i = pl.program_id(2)
    halo = N_BACK * d
    tq, tk = N_BACK, 2 * N_BACK

    @pl.when((b == 0) & (i == 0))
    def _():
        for h in range(HEADS_PER_LANE_TILE):
            bias_scr[h] = _toeplitz(seg_ref[h], tq, tq, tk)

    kcat_scr[0:halo, :] = kp_ref[...]
    kcat_scr[halo:, :] = kc_ref[...]
    vcat_scr[0:halo, :] = vp_ref[...]
    vcat_scr[halo:, :] = vc_ref[...]

    def rows(ref, start, n):
        return ref[pl.ds(start, n), :] if d == 1 else ref[pl.ds(start, n, stride=d), :]

    lane = lax.broadcasted_iota(jnp.int32, (tq, LANES), 1)
    col = lax.broadcasted_iota(jnp.int32, (tq, tk), 1)
    first = lane < HEAD_DIM
    heads = range(HEADS_PER_LANE_TILE)

    def unit(u, carry):
        starts, deads, logits, vals = [], [], [], []
        for s in range(DIL_UNIT):
            pidx = u * DIL_UNIT + s
            r, jc = pidx // n_jc, pidx % n_jc
            start = r + d * (jc * N_BACK)
            starts.append(start)
            deads.append(jnp.where((i == 0) & (jc == 0), N_BACK, 0))
            q = rows(q_ref, start, tq) * SCALE
            k = rows(kcat_scr, start, tk).astype(BF16)
            vals.append(rows(vcat_scr, start, tk).astype(BF16))
            for h in heads:
                hm = (lane >= h * HEAD_DIM) & (lane < (h + 1) * HEAD_DIM)
                qh = jnp.where(hm, q, 0.0).astype(BF16)
                logits.append(lax.dot_general(qh, k, (((1,), (1,)), ((), ())), preferred_element_type=F32))
        probs, stats = [], []
        for n, sc in enumerate(logits):
            sc = jnp.where(col < deads[n // HEADS_PER_LANE_TILE], NEG, sc + bias_scr[n % HEADS_PER_LANE_TILE])
            m = jnp.max(sc, axis=-1, keepdims=True)
            e = jnp.exp(sc - m)
            probs.append(e.astype(BF16))
            stats.append((m, jnp.sum(e, axis=-1, keepdims=True)))
        for s in range(DIL_UNIT):
            outs, lses = [], []
            for h in heads:
                n = s * HEADS_PER_LANE_TILE + h
                m, l = stats[n]
                outs.append(jnp.dot(probs[n], vals[s], preferred_element_type=F32) / l)
                lses.append(m + jnp.log(l))
            o_val = jnp.where(first, outs[0], outs[1])
            lse_val = jnp.where(first, lses[0], lses[1])
            if d == 1:
                o_ref[pl.ds(starts[s], tq), :] = o_val
                lse_ref[pl.ds(starts[s], tq), :] = lse_val
            else:
                o_ref[pl.ds(starts[s], tq, stride=d), :] = o_val
                lse_ref[pl.ds(starts[s], tq, stride=d), :] = lse_val
        return carry

    lax.fori_loop(0, d * n_jc // DIL_UNIT, unit, 0)


def dil_prompt(qkv, batch, seq_len, gi, bias_k):
    win, d = DIL_GROUPS[gi]
    assert win // d == N_BACK and seq_len % (d * N_BACK) == 0
    halo = N_BACK * d
    blk = max(halo, min(2048, seq_len))
    n_jc = blk // halo
    assert seq_len % blk == 0 and blk % halo == 0 and (d * n_jc) % DIL_UNIT == 0
    hp_tiles = HD // LANES
    qkv3 = qkv.reshape(batch, seq_len, qkv.shape[1])
    width = 3 * N_BACK
    step = 2 * N_BACK - jnp.arange(width)
    seg = jnp.where(((step >= 0) & (step <= N_BACK))[:, None], bias_k[jnp.clip(step, 0, N_BACK)].astype(F32), NEG)
    seg = jnp.transpose(seg)[:, None, :]

    def col(which):
        return (which * N_GROUPS + gi) * hp_tiles

    cur = lambda which: pl.BlockSpec((None, blk, LANES), lambda hp, b, i: (b, i, col(which) + hp))
    prev = lambda which: pl.BlockSpec((None, halo, LANES),
                                      lambda hp, b, i: (b, jnp.maximum(i * n_jc - 1, 0), col(which) + hp))
    out_spec = pl.BlockSpec((None, blk, LANES), lambda hp, b, i: (b, i, hp))
    o, lse = pl.pallas_call(
        functools.partial(_dil_prompt_kernel, d=d, n_jc=n_jc),
        out_shape=(jax.ShapeDtypeStruct((batch, seq_len, HD), F32),) * 2,
        grid=(hp_tiles, batch, seq_len // blk),
        in_specs=[cur(0), cur(1), prev(1), cur(2), prev(2),
                  pl.BlockSpec((HEADS_PER_LANE_TILE, 1, width), lambda hp, b, i: (hp, 0, 0))],
        out_specs=(out_spec, out_spec),
        scratch_shapes=[pltpu.VMEM((HEADS_PER_LANE_TILE, N_BACK, 2 * N_BACK), F32),
                        pltpu.VMEM((halo + blk, LANES), F32), pltpu.VMEM((halo + blk, LANES), F32)],
        compiler_params=_params(("parallel", "arbitrary", "arbitrary")),
        name=f"dil_prompt_g{gi}",
    )(qkv3, qkv3, qkv3, qkv3, qkv3, seg)
    return o.reshape(batch * seq_len, HD), lse.reshape(batch * seq_len, HD)


def _dil_sample_kernel(ct_ref, q_ref, newt_ref, bias_c_ref, bias_n_ref, ot_ref, o_ref, lse_ref, *, hb, t_len, L):
    lane = lax.broadcasted_iota(jnp.int32, (HEAD_DIM, LANES), 1)
    nt = (((1,), (1,)), ((), ()))
    for hh in range(hb):
        q = (q_ref[hh] * SCALE).astype(BF16)
        kt, vt = ct_ref[0, hh], ct_ref[1, hh]
        knt, vnt = newt_ref[0, hh], newt_ref[1, hh]
        s_c = jnp.dot(q, kt.astype(BF16), preferred_element_type=F32) + bias_c_ref[hh]
        s_n = jnp.dot(q, knt.astype(BF16), preferred_element_type=F32) + bias_n_ref[hh]
        m = jnp.maximum(jnp.max(s_c, axis=1, keepdims=True), jnp.max(s_n, axis=1, keepdims=True))
        e_c = jnp.exp(s_c - m)
        e_n = jnp.exp(s_n - m)
        l = jnp.sum(e_c, axis=1, keepdims=True) + jnp.sum(e_n, axis=1, keepdims=True)
        pv = (lax.dot_general(e_c.astype(BF16), vt.astype(BF16), nt, preferred_element_type=F32)
              + lax.dot_general(e_n.astype(BF16), vnt.astype(BF16), nt, preferred_element_type=F32))
        o_ref[hh] = pv / l
        lse_ref[hh] = jnp.broadcast_to(m + jnp.log(l), (t_len, HEAD_DIM))
        for kv, (old, new) in enumerate(((kt, knt), (vt, vnt))):
            moved = pltpu.roll(old, L - t_len, 1)
            if L > LANES:
                ot_ref[kv, hh, :, :L - LANES] = moved[:, :L - LANES]
            ot_ref[kv, hh, :, L - LANES:] = jnp.where(lane >= LANES - t_len, new, moved[:, L - LANES:])


def dil_sample(cache, new_kv, q_s, dec_batch, t_len, gi, bias_k):
    win, d = DIL_GROUPS[gi]
    L = cache.shape[1]
    assert L == win and win // d == N_BACK and L % LANES == 0 and t_len <= LANES
    hb = min(N_HEADS, max(1, (4 << 20) // (2 * HEAD_DIM * L * 4)))
    assert N_HEADS % hb == 0
    ct = jnp.transpose(cache, (0, 2, 3, 4, 1))
    newt = jnp.pad(jnp.transpose(new_kv, (0, 2, 3, 4, 1)), ((0, 0),) * 4 + ((LANES - t_len, 0),))
    qh = jnp.transpose(q_s, (0, 2, 1, 3))
    tq = jnp.arange(t_len)[:, None]
    back = L + tq - jnp.arange(L)[None, :]
    ok_c = (back % d == 0) & (back // d <= N_BACK)
    bias_c = jnp.where(ok_c[:, :, None], bias_k[jnp.clip(back // d, 0, N_BACK)].astype(F32), NEG)
    t2 = jnp.arange(LANES)[None, :] - (LANES - t_len)
    ok_n = (t2 >= 0) & (t2 <= tq) & ((tq - t2) % d == 0)
    bias_n = jnp.where(ok_n[:, :, None], bias_k[jnp.clip((tq - t2) // d, 0, N_BACK)].astype(F32), NEG)
    bias_c = jnp.transpose(bias_c, (2, 0, 1))
    bias_n = jnp.transpose(bias_n, (2, 0, 1))
    cache_spec = pl.BlockSpec((None, 2, hb, HEAD_DIM, L), lambda b, j: (b, 0, j, 0, 0))
    row_spec = pl.BlockSpec((None, hb, t_len, HEAD_DIM), lambda b, j: (b, j, 0, 0))
    ot, o, lse = pl.pallas_call(
        functools.partial(_dil_sample_kernel, hb=hb, t_len=t_len, L=L),
        out_shape=(jax.ShapeDtypeStruct(ct.shape, F32),
                   jax.ShapeDtypeStruct(qh.shape, F32), jax.ShapeDtypeStruct(qh.shape, F32)),
        grid=(dec_batch, N_HEADS // hb),
        in_specs=[
            cache_spec,
            row_spec,
            pl.BlockSpec((None, 2, hb, HEAD_DIM, LANES), lambda b, j: (b, 0, j, 0, 0)),
            pl.BlockSpec((hb, t_len, L), lambda b, j: (j, 0, 0)),
            pl.BlockSpec((hb, t_len, LANES), lambda b, j: (j, 0, 0)),
        ],
        out_specs=(cache_spec, row_spec, row_spec),
        compiler_params=_params(("parallel", "parallel")),
        name=f"dil_sample_g{gi}",
    )(ct, qh, newt, bias_c, bias_n)
    to_rows = lambda a: jnp.transpose(a, (0, 2, 1, 3)).reshape(dec_batch * t_len, HD)
    return to_rows(o), to_rows(lse), jnp.transpose(ot, (0, 4, 1, 2, 3))


def _block_mean_kernel(k_ref, o_ref):
    o_ref[...] = jnp.broadcast_to(jnp.sum(k_ref[...], axis=0, keepdims=True) * (1.0 / MOBA_BLOCK), o_ref.shape)


def block_means(kvb, n_blocks_total):
    out = pl.pallas_call(
        _block_mean_kernel,
        out_shape=jax.ShapeDtypeStruct((n_blocks_total, 8, HD), F32),
        grid=(n_blocks_total,),
        in_specs=[pl.BlockSpec((MOBA_BLOCK, HD), lambda i: (i, 0))],
        out_specs=pl.BlockSpec((None, 8, HD), lambda i: (i, 0, 0)),
        compiler_params=_params(("parallel",)),
        name="block_means",
    )(kvb)
    return out[:, 0]


def _top_blocks(sc, blk, n_valid, n_sel, axis):
    sel = jnp.zeros(sc.shape, F32)
    big = float(sc.shape[axis])
    for it in range(n_sel):
        mx = jnp.max(sc, axis=axis, keepdims=True)
        idx = jnp.min(jnp.where(sc == mx, blk, big), axis=axis, keepdims=True)
        hit = blk == idx
        sel = jnp.maximum(sel, jnp.where(hit, jnp.where(it < n_valid, 1.0, 0.0), 0.0))
        sc = jnp.where(hit, -jnp.inf, sc)
    return sel


def _moba_prompt_kernel(q_ref, k_ref, vt_ref, km_ref, seg_ref, o_ref,
                        m_scr, l_scr, acc_scr, selb_scr, bias_scr, *, nb, n_sel, kb):
    b = pl.program_id(1)
    i = pl.program_id(2)
    tb = MOBA_BLOCK

    def head_mask(shape, axis, h):
        idx = lax.broadcasted_iota(jnp.int32, shape, axis)
        return (idx >= h * HEAD_DIM) & (idx < (h + 1) * HEAD_DIM)

    @pl.when((b == 0) & (i == 0))
    def _():
        row = lax.broadcasted_iota(jnp.int32, (tb, tb), 0)
        colm = lax.broadcasted_iota(jnp.int32, (tb, tb), 1)
        for h in range(HEADS_PER_LANE_TILE):
            def fill(delta, carry, h=h):
                bias_scr[h, delta] = _toeplitz(seg_ref[h, pl.ds(delta, 1), :], tb, tb, tb) * LOG2E
                return carry
            lax.fori_loop(0, nb, fill, 0)
            bias_scr[h, 0] = jnp.where(colm >= row, bias_scr[h, 0], NEG)

    heads = range(HEADS_PER_LANE_TILE)
    m_scr[...] = jnp.full_like(m_scr, NEG)
    l_scr[...] = jnp.zeros_like(l_scr)
    acc_scr[...] = jnp.zeros_like(acc_scr)
    q_raw = q_ref[...]
    km = km_ref[...]
    blk = lax.broadcasted_iota(jnp.int32, (nb, tb), 0).astype(F32)
    i_f = i.astype(F32)
    for h in heads:
        kmh = jnp.where(head_mask(km.shape, 1, h), km, 0.0)
        sc = lax.dot_general(kmh, q_raw, (((1,), (1,)), ((), ())), precision=HIGHEST, preferred_element_type=F32)
        sc = jnp.where(blk < i_f, sc, -jnp.inf)
        sel = _top_blocks(sc, blk, i, n_sel, axis=0) + jnp.where(blk == i_f, 1.0, 0.0)
        selb_scr[h] = jnp.where(sel > 0.5, 0.0, NEG)

    q = q_raw * (SCALE * LOG2E)
    qhs = [jnp.where(head_mask(q.shape, 1, h), q, 0.0).astype(BF16) for h in heads]

    def key_tile(kt):
        k0 = pl.multiple_of(kt * (kb * tb), kb * tb)
        k = k_ref[pl.ds(k0, kb * tb), :]
        vt = vt_ref[:, pl.ds(k0, kb * tb)]
        raw = [[lax.dot_general(k[c * tb:(c + 1) * tb], qhs[h], (((1,), (1,)), ((), ())),
                                preferred_element_type=F32) for c in range(kb)] for h in heads]
        parts, shifts, alphas = [], [], []
        for h in heads:
            ps, sbs, cms = [], [], []
            for c in range(kb):
                jj = kt * kb + c
                p = raw[h][c] + bias_scr[h, jnp.maximum(i - jj, 0)]
                sb = selb_scr[h, pl.ds(jj, 1), :]
                ps.append(p)
                sbs.append(sb)
                cms.append(jnp.max(p, axis=0, keepdims=True) + sb)
            m_old = m_scr[h]
            m_new = jnp.maximum(m_old, functools.reduce(jnp.maximum, cms))
            parts.append(ps)
            shifts.append([m_new - sb for sb in sbs])
            alphas.append(jnp.exp2(m_old - m_new))
            m_scr[h] = m_new
        for h in heads:
            acc = alphas[h] * acc_scr[h]
            l_new = alphas[h] * l_scr[h]
            for c in range(kb):
                e = jnp.exp2(parts[h][c] - shifts[h][c])
                l_new = l_new + jnp.sum(e, axis=0, keepdims=True)
                acc = acc + jnp.dot(vt[h * HEAD_DIM:(h + 1) * HEAD_DIM, c * tb:(c + 1) * tb], e.astype(BF16),
                                    preferred_element_type=F32)
            l_scr[h] = l_new
            acc_scr[h] = acc

    own_tile = i // kb
    key_tile(own_tile)

    def past_tile(kt, carry):
        key_tile(kt)
        return carry

    lax.fori_loop(0, own_tile, past_tile, 0)
    o_t = jnp.concatenate([acc_scr[h] / l_scr[h] for h in heads], axis=0)
    o_ref[...] = o_t.T


def moba_prompt(q, kvb, k_means, tab, batch, seq_len):
    tb = MOBA_BLOCK
    kb = 4
    assert seq_len % (kb * tb) == 0
    nb = seq_len // tb
    n_sel = min(MOBA_TOPK, nb)
    hp_tiles = HD // LANES
    dist = jnp.arange(nb)[:, None] * tb + jnp.arange(2 * tb)[None, :] - tb
    seg = tab.T.astype(F32)[:, jnp.clip(dist, 0, tab.shape[0] - 1)]
    km3 = k_means.reshape(batch, nb, HD)
    kv16 = kvb.astype(BF16)
    v_t = kv16[:, HD:].T
    return pl.pallas_call(
        functools.partial(_moba_prompt_kernel, nb=nb, n_sel=n_sel, kb=kb),
        out_shape=jax.ShapeDtypeStruct(q.shape, F32),
        grid=(hp_tiles, batch, nb),
        in_specs=[
            pl.BlockSpec((tb, LANES), lambda hp, b, i: (b * nb + i, hp)),
            pl.BlockSpec((seq_len, LANES), lambda hp, b, i: (b, hp)),
            pl.BlockSpec((LANES, seq_len), lambda hp, b, i: (hp, b)),
            pl.BlockSpec((None, nb, LANES), lambda hp, b, i: (b, 0, hp)),
            pl.BlockSpec((HEADS_PER_LANE_TILE, nb, 2 * tb), lambda hp, b, i: (hp, 0, 0)),
        ],
        out_specs=pl.BlockSpec((tb, LANES), lambda hp, b, i: (b * nb + i, hp)),
        scratch_shapes=[pltpu.VMEM((HEADS_PER_LANE_TILE, 1, tb), F32),
                        pltpu.VMEM((HEADS_PER_LANE_TILE, 1, tb), F32),
                        pltpu.VMEM((HEADS_PER_LANE_TILE, HEAD_DIM, tb), F32),
                        pltpu.VMEM((HEADS_PER_LANE_TILE, nb, tb), F32),
                        pltpu.VMEM((HEADS_PER_LANE_TILE, nb, tb, tb), F32)],
        compiler_params=_params(("parallel", "arbitrary", "arbitrary")),
        name="moba_prompt",
    )(q, kv16, v_t, km3, seg)


def _moba_page_kernel(pt_ref, *rest, t_len, pp):
    del pt_ref
    pool_refs, (qbd_ref, bias_ref, dmask_ref, ksum_ref, m_ref, l_ref, acc_ref) = rest[:pp], rest[pp:]
    j = pl.program_id(1)
    qbd = qbd_ref[...].astype(BF16)
    dmask = dmask_ref[...]
    kts = [ref[0] for ref in pool_refs]
    logits = [jnp.dot(qbd, kt.astype(BF16), preferred_element_type=F32) for kt in kts]
    ms, ls, es = [], [], []
    for u, s in enumerate(logits):
        s = s + bias_ref[u]
        m = jnp.max(s, axis=1, keepdims=True)
        e = jnp.exp(s - m)
        ms.append(m)
        ls.append(jnp.sum(e, axis=1, keepdims=True))
        es.append(e.astype(BF16))
    for u in range(pp):
        pv = lax.dot_general(es[u], pool_refs[u][1].astype(BF16), (((1,), (1,)), ((), ())),
                             preferred_element_type=F32)
        rows = [jnp.sum(pv[t * N_HEADS:(t + 1) * N_HEADS, :] * dmask, axis=0, keepdims=True) for t in range(t_len)]
        acc_ref[u * t_len:(u + 1) * t_len, :] = jnp.concatenate(rows, axis=0)

    @pl.when(j == 0)
    def _():
        ksum_ref[...] = jnp.zeros_like(ksum_ref)
        m_ref[...] = jnp.zeros_like(m_ref)
        l_ref[...] = jnp.zeros_like(l_ref)

    def put(ref, cols):
        lane = lax.broadcasted_iota(jnp.int32, ref.shape, 1)
        val = ref[...]
        for u, col in enumerate(cols):
            val = jnp.where(lane == j * pp + u, col, val)
        ref[...] = val

    put(ksum_ref, [jnp.sum(kt, axis=1, keepdims=True) for kt in kts])
    put(m_ref, ms)
    put(l_ref, ls)


def _moba_combine_kernel(ks_ref, m_ref, l_ref, acc_ref, qbd_ref, knt_ref, vnew_ref, bias_ref, pair_ref, dmask_ref,
                         o_ref, *, n_pages, n_blocks, n_sel, t_len):
    qbd = qbd_ref[...]
    pair = pair_ref[...]
    sc_pages = jnp.dot(qbd, ks_ref[...], precision=HIGHEST, preferred_element_type=F32)
    sc = jnp.dot(sc_pages, pair, precision=HIGHEST, preferred_element_type=F32) * (1.0 / MOBA_BLOCK)
    lane = lax.broadcasted_iota(jnp.int32, sc.shape, 1).astype(F32)
    sc = jnp.where(lane < n_blocks, sc, -jnp.inf)
    sel = _top_blocks(sc, lane, n_sel, n_sel, axis=1)
    picked = lax.dot_general(sel, pair, (((1,), (1,)), ((), ())), preferred_element_type=F32) > 0.5
    m_p = m_ref[...]
    s_own = jnp.dot(qbd.astype(BF16), knt_ref[...].astype(BF16), preferred_element_type=F32) + bias_ref[...]
    m_tot = jnp.maximum(jnp.max(jnp.where(picked, m_p, NEG), axis=1, keepdims=True),
                        jnp.max(s_own, axis=1, keepdims=True))
    w = jnp.where(picked, jnp.exp(m_p - m_tot), 0.0)
    e_own = jnp.exp(s_own - m_tot)
    l_tot = jnp.sum(w * l_ref[...], axis=1, keepdims=True) + jnp.sum(e_own, axis=1, keepdims=True)
    dmask = dmask_ref[...]
    vnew = vnew_ref[...].astype(BF16)
    rows = []
    for t in range(t_len):
        hs = slice(t * N_HEADS, (t + 1) * N_HEADS)
        a = jnp.dot(w[hs, :n_pages], acc_ref[t], precision=HIGHEST, preferred_element_type=F32)
        a = a + jnp.dot(e_own[hs].astype(BF16), vnew, preferred_element_type=F32)
        num = jnp.sum(a * dmask, axis=0, keepdims=True)
        den = jnp.sum(l_tot[hs] * dmask, axis=0, keepdims=True)
        rows.append(num / den)
    o_ref[...] = jnp.concatenate(rows, axis=0)


def moba_sample(q_s, kvb_s, pool, page_table, tab, dec_batch, t_len):
    n_pool, page = pool.shape[0], pool.shape[1]
    n_pages = page_table.shape[1]
    past = n_pages * page
    assert MOBA_BLOCK % page == 0 and past % MOBA_BLOCK == 0 and t_len <= MOBA_BLOCK and t_len == 8
    ppb = MOBA_BLOCK // page
    n_blocks = past // MOBA_BLOCK
    n_sel = min(MOBA_TOPK, n_blocks + 1)
    assert n_blocks >= n_sel, "fewer cached blocks than top-k picks is not supported"
    n_col = t_len * N_HEADS
    assert n_col == LANES and page == LANES and n_pages <= LANES
    pool_t = jnp.transpose(pool, (0, 2, 3, 4, 1)).reshape(n_pool, 2, HD, page)
    q4 = q_s.reshape(dec_batch, t_len, N_HEADS, HEAD_DIM) * SCALE
    qbd = jnp.einsum('bthx,hg->bthgx', q4, jnp.eye(N_HEADS, dtype=F32)).reshape(dec_batch, n_col, HD)
    rev = tab.T.astype(F32)[:, past + t_len - 1 - jnp.arange(past + t_len)]
    by_t = jnp.stack([rev[:, t_len - 1 - t:t_len - 1 - t + past] for t in range(t_len)])
    bias_pages = jnp.transpose(by_t.reshape(t_len, N_HEADS, n_pages, page), (2, 0, 1, 3)).reshape(n_pages, n_col, page)
    d_own = jnp.arange(t_len)[:, None] - jnp.arange(LANES)[None, :]
    ok_own = (d_own >= 0) & (jnp.arange(LANES)[None, :] < t_len)
    bias_own = jnp.where(ok_own[:, :, None], tab.astype(F32)[jnp.clip(d_own, 0, t_len)], NEG)
    bias_own = jnp.swapaxes(bias_own, 1, 2).reshape(n_col, LANES)
    dmask = (jnp.arange(N_HEADS)[:, None] == (jnp.arange(HD) // HEAD_DIM)[None, :]).astype(F32)
    pair = ((jnp.arange(LANES)[:, None] // ppb == jnp.arange(LANES)[None, :])
            & (jnp.arange(LANES)[:, None] < n_pages)).astype(F32)
    new3 = kvb_s.reshape(dec_batch, t_len, 2 * HD)
    knt = jnp.pad(jnp.swapaxes(new3[:, :, :HD], 1, 2), ((0, 0), (0, 0), (0, LANES - t_len)))
    vnew = jnp.pad(new3[:, :, HD:], ((0, 0), (0, LANES - t_len), (0, 0)))

    pp = 4 if n_pages % 4 == 0 else 1
    page_spec = lambda u: pl.BlockSpec((None, 2, HD, page), lambda b, j, pt: (pt[b, j * pp + u], 0, 0, 0))
    stat = lambda rows: pl.BlockSpec((None, rows, LANES), lambda b, j, pt: (b, 0, 0))
    ksum, m_p, l_p, acc = pl.pallas_call(
        functools.partial(_moba_page_kernel, t_len=t_len, pp=pp),
        out_shape=(jax.ShapeDtypeStruct((dec_batch, HD, LANES), F32),
                   jax.ShapeDtypeStruct((dec_batch, n_col, LANES), F32),
                   jax.ShapeDtypeStruct((dec_batch, n_col, LANES), F32),
                   jax.ShapeDtypeStruct((dec_batch, n_pages * t_len, HD), F32)),
        grid_spec=pltpu.PrefetchScalarGridSpec(
            num_scalar_prefetch=1,
            grid=(dec_batch, n_pages // pp),
            in_specs=[page_spec(u) for u in range(pp)] + [
                pl.BlockSpec((None, n_col, HD), lambda b, j, pt: (b, 0, 0)),
                pl.BlockSpec((pp, n_col, page), lambda b, j, pt: (j, 0, 0)),
                pl.BlockSpec((N_HEADS, HD), lambda b, j, pt: (0, 0)),
            ],
            out_specs=(stat(HD), stat(n_col), stat(n_col),
                       pl.BlockSpec((None, pp * t_len, HD), lambda b, j, pt: (b, j, 0))),
        ),
        compiler_params=_params(("parallel", "arbitrary")),
        name="moba_page",
    )(page_table, *([pool_t] * pp), qbd, bias_pages, dmask)

    acc = jnp.swapaxes(acc.reshape(dec_batch, n_pages, t_len, HD), 1, 2)
    per_b = lambda shape: pl.BlockSpec((None,) + shape, lambda b: (b,) + (0,) * len(shape))
    const = lambda shape: pl.BlockSpec(shape, lambda b: (0,) * len(shape))
    out = pl.pallas_call(
        functools.partial(_moba_combine_kernel, n_pages=n_pages, n_blocks=n_blocks, n_sel=n_sel, t_len=t_len),
        out_shape=jax.ShapeDtypeStruct((dec_batch, t_len, HD), F32),
        grid=(dec_batch,),
        in_specs=[per_b((HD, LANES)), per_b((n_col, LANES)), per_b((n_col, LANES)),
                  per_b((t_len, n_pages, HD)), per_b((n_col, HD)), per_b((HD, LANES)), per_b((LANES, HD)),
                  const((n_col, LANES)), const((LANES, LANES)), const((N_HEADS, HD))],
        out_specs=per_b((t_len, HD)),
        compiler_params=_params(("parallel",)),
        name="moba_combine",
    )(ksum, m_p, l_p, acc, qbd, knt, vnew, bias_own, pair, dmask)
    return out.reshape(dec_batch * t_len, HD)


def kernel(x_prompt, x_sample, cache_a_kv_g0, cache_a_kv_g1, cache_a_kv_g2, cache_b_kv_pool, page_table, state_ffn_conv, rel_bias, norm_attn, norm_ffn, w_qkv_a, q_norm_a, k_norm_a, w_o_a, norm_kv, w_kv_b, k_norm_b, w_q_b, q_norm_b, w_o_b, w_up, conv_w, conv_b, w_down):
    B, S, D = x_prompt.shape
    DB, T, _ = x_sample.shape
    depth = norm_attn.shape[0]
    n_self = w_qkv_a.shape[0]
    caches = (cache_a_kv_g0, cache_a_kv_g1, cache_a_kv_g2)
    past = page_table.shape[1] * cache_b_kv_pool.shape[1]
    assert D == HD

    group_bias = [rel_bias[_rel_bucket(dil * jnp.arange(win // dil + 1))] for (win, dil) in DIL_GROUPS]
    tab = rel_bias[_rel_bucket(jnp.arange(max(S, past + T)))]

    ones =jnp.ones((HD,), F32)
    tile_h = lambda gvec, n: jnp.tile(gvec, n * N_HEADS)

    def self_layer(x, l, prompt):
        hg = jnp.concatenate([tile_h(q_norm_a[l], N_GROUPS), tile_h(k_norm_a[l], N_GROUPS), jnp.tile(ones, N_GROUPS)])
        qkv = norm_matmul(x, norm_attn[l], w_qkv_a[l].astype(BF16), hg, 2 * N_GROUPS * HD)
        outs, lses, bufs = [], [], []
        seq = S if prompt else T
        qkv3 = qkv.reshape(-1, seq, qkv.shape[1])
        for gi, (win, dil) in enumerate(DIL_GROUPS):
            keep = min(win, seq)
            part = lambda which: qkv3[:, seq - keep:, (which * N_GROUPS + gi) * HD:(which * N_GROUPS + gi + 1) * HD
                                      ].reshape(-1, keep, N_HEADS, HEAD_DIM)
            new_kv = jnp.stack([part(1), part(2)], axis=2)
            if prompt:
                o, lse = dil_prompt(qkv, B, S, gi, group_bias[gi])
                bufs.append(new_kv)
            else:
                o, lse, buf = dil_sample(caches[gi][l], new_kv, part(0), DB, T, gi, group_bias[gi])
                bufs.append(buf)
            outs.append(o)
            lses.append(lse)
        return merge_proj(outs, lses, w_o_a[l].astype(BF16), x), bufs

    def ffn(x, l, prompt):
        wup, wdn = w_up[l].astype(BF16), w_down[l].astype(BF16)
        if prompt:
            tm = 256
            y, ut = ffn_prompt(x, S, norm_ffn[l], wup, conv_w[l], conv_b[l], wdn, tm=tm)
            cs = ut.reshape(B, S // tm, 8, ut.shape[1])[:, -1, 8 - (CONV_W - 1):]
        else:
            y, ua, ub = ffn_sample(x, T, state_ffn_conv[l], norm_ffn[l], wup, conv_w[l], conv_b[l], wdn)
            u = jnp.concatenate([ua, ub], axis=1)
            cs = u.reshape(DB, T, u.shape[1])[:, T - (CONV_W - 1):]
        return y, cs

    def trunk(x, prompt):
        nb_rows = B if prompt else DB
        a_new = [[] for _ in range(N_GROUPS)]
        conv_new = []
        kvb = None
        k_means = None
        for l in range(depth):
            if l < n_self:
                x, bufs = self_layer(x, l, prompt)
                for gi in range(N_GROUPS):
                    a_new[gi].append(bufs[gi])
            else:
                if l == n_self:
                    hg = jnp.concatenate([tile_h(k_norm_b, 1), ones])
                    kvb = norm_matmul(x, norm_kv, w_kv_b.astype(BF16), hg, HD)
                    if prompt:
                        k_means = block_means(kvb, B * S // MOBA_BLOCK)
                lb = l - n_self
                q = norm_matmul(x, norm_attn[l], w_q_b[lb].astype(BF16), tile_h(q_norm_b[lb], 1), HD)
                if prompt:
                    o = moba_prompt(q, kvb, k_means, tab, B, S)
                else:
                    o = moba_sample(q, kvb, cache_b_kv_pool, page_table, tab, DB, T)
                x = merge_proj([o], None, w_o_b[lb].astype(BF16), x)
            x, cs = ffn(x, l, prompt)
            conv_new.append(cs)
        seq = S if prompt else T
        kv_new = kvb.reshape(nb_rows, seq, 2, N_HEADS, HEAD_DIM)
        return x.reshape(nb_rows, seq, D), [jnp.stack(a, 0) for a in a_new], kv_new, jnp.stack(conv_new, 0)

    y_p, a_p, kv_p, conv_p = trunk(x_prompt.reshape(B * S, D), True)
    y_s, a_s, kv_s, conv_s = trunk(x_sample.reshape(DB * T, D), False)
    return (y_p, y_s, a_p[0], a_p[1], a_p[2], a_s[0], a_s[1], a_s[2], kv_p, kv_s, conv_p, conv_s)
```

```python
import functools
import math

import jax
import jax.numpy as jnp
import numpy as np
from jax import lax
from jax.experimental import pallas as pl
from jax.experimental.pallas import tpu as pltpu

HEAD_DIM = 64
N_HEADS = 16
HD = N_HEADS * HEAD_DIM
DIL_GROUPS = ((128, 1), (512, 4), (2048, 16))
N_GROUPS = len(DIL_GROUPS)
N_BACK = 128
MOBA_BLOCK = 256
MOBA_TOPK = 3
NUM_BUCKETS = 32
REL_MAX_DIST = 4096
CONV_W = 3
EPS = 1e-6
SCALE = HEAD_DIM ** -0.5
NEG = -1e30
LOG2E = 1.4426950408889634
LANES = 128
MXU_TILE = 256
HEADS_PER_LANE_TILE = LANES // HEAD_DIM
VMEM_LIMIT = 56 * 1024 * 1024

F32 = jnp.float32
BF16 = jnp.bfloat16
HIGHEST = lax.Precision.HIGHEST


def _params(sem, vmem=VMEM_LIMIT):
    return pltpu.CompilerParams(dimension_semantics=sem, vmem_limit_bytes=vmem)


def _const_spec(shape):
    nd = len(shape)
    return pl.BlockSpec(shape, lambda *_: (0,) * nd, pipeline_mode=pl.Buffered(1))


def _rel_bucket(dist):
    n = jnp.maximum(dist, 0)
    exact = NUM_BUCKETS // 2
    nf = jnp.maximum(n, exact).astype(F32)
    large = exact + (jnp.log(nf / exact) / math.log(REL_MAX_DIST / exact) * (NUM_BUCKETS - exact)).astype(jnp.int32)
    return jnp.where(n < exact, n, jnp.minimum(large, NUM_BUCKETS - 1))


def _norm_matmul_kernel(x_ref, g_ref, w_ref, hg_ref, bd_ref, o_ref, h_scr, *, n_norm_tiles):
    j = pl.program_id(1)

    @pl.when(j == 0)
    def _():
        x = x_ref[...]
        r = lax.rsqrt(jnp.mean(x * x, axis=-1, keepdims=True) + EPS)
        h_scr[...] = (x * r * g_ref[...]).astype(BF16)

    y = jnp.dot(h_scr[...], w_ref[...], preferred_element_type=F32)

    @pl.when(j < n_norm_tiles)
    def _():
        y2 = (y * y).astype(BF16)
        ss = jnp.concatenate([jnp.dot(y2[:, c:c + MXU_TILE], bd_ref[...], preferred_element_type=F32)
                              for c in range(0, y.shape[1], MXU_TILE)], axis=1)
        o_ref[...] = y * lax.rsqrt(ss * (1.0 / HEAD_DIM) + EPS) * hg_ref[...]

    @pl.when(j >= n_norm_tiles)
    def _():
        o_ref[...] = y


def norm_matmul(x, g, w_bf16, head_gain, n_norm_cols, *, tn=512):
    M, K = x.shape
    N = w_bf16.shape[1]
    tm = min(1024, M)
    assert M % tm == 0 and N % tn == 0 and n_norm_cols % tn == 0 and tn % MXU_TILE == 0
    hid = np.arange(MXU_TILE) // HEAD_DIM
    bd = jnp.asarray(hid[:, None] == hid[None, :], BF16)
    return pl.pallas_call(
        functools.partial(_norm_matmul_kernel, n_norm_tiles=n_norm_cols // tn),
        out_shape=jax.ShapeDtypeStruct((M, N), F32),
        grid=(M // tm, N // tn),
        in_specs=[
            pl.BlockSpec((tm, K), lambda i, j: (i, 0)),
            pl.BlockSpec((1, K), lambda i, j: (0, 0)),
            pl.BlockSpec((K, tn), lambda i, j: (0, j)),
            pl.BlockSpec((1, tn), lambda i, j: (0, j)),
            pl.BlockSpec((MXU_TILE, MXU_TILE), lambda i, j: (0, 0)),
        ],
        out_specs=pl.BlockSpec((tm, tn), lambda i, j: (i, j)),
        scratch_shapes=[pltpu.VMEM((tm, K), BF16)],
        compiler_params=_params(("parallel", "arbitrary")),
        name="norm_matmul",
    )(x, g.reshape(1, K), w_bf16, head_gain.reshape(1, N), bd)


def _merge_proj_kernel(*refs, n_g):
    o_refs = refs[:n_g]
    lse_refs = refs[n_g:2 * n_g] if n_g > 1 else ()
    w_ref, x_ref, out_ref = refs[-3:]
    if n_g == 1:
        a = o_refs[0][...]
    else:
        ls = [r[...] for r in lse_refs]
        mx = functools.reduce(jnp.maximum, ls)
        ws = [jnp.exp(l - mx) for l in ls]
        den = functools.reduce(lambda p, q: p + q, ws)
        num = functools.reduce(lambda p, q: p + q, [w * r[...] for w, r in zip(ws, o_refs)])
        a = num / den
    out_ref[...] = x_ref[...] + jnp.dot(a.astype(BF16), w_ref[...], preferred_element_type=F32)


def merge_proj(os_, lses, w_bf16, x):
    M, D = x.shape
    n_g = len(os_)
    tm = min(256, M)
    assert M % tm == 0
    row = pl.BlockSpec((tm, D), lambda i: (i, 0))
    ins = list(os_) + (list(lses) if n_g > 1 else [])
    return pl.pallas_call(
        functools.partial(_merge_proj_kernel, n_g=n_g),
        out_shape=jax.ShapeDtypeStruct((M, D), F32),
        grid=(M // tm,),
        in_specs=[row] * len(ins) + [_const_spec(w_bf16.shape), row],
        out_specs=row,
        compiler_params=_params(("parallel",)),
        name="merge_proj",
    )(*ins, w_bf16, x)


def _silu_gate(ca, cb):
    return (ca / (1.0 + jnp.exp(-ca))) * cb


def _ffn_prompt_kernel(x_ref, xp_ref, g_ref, wup_ref, cw_ref, cb_ref, wdn_ref, y_ref, ut_ref,
                       ua_scr, ub_scr, gate_scr, *, tm, ck, d_ff, tiles_per_seq):
    i = pl.program_id(0)
    x = x_ref[...]
    keep = jnp.where(i % tiles_per_seq == 0, 0.0, 1.0)
    xc = jnp.concatenate([xp_ref[...] * keep, x], axis=0)
    r = lax.rsqrt(jnp.mean(xc * xc, axis=-1, keepdims=True) + EPS)
    h = (xc * r * g_ref[...]).astype(BF16)
    for c in range(d_ff // ck):
        a0, b0 = c * ck, d_ff + c * ck
        ua_scr[...] = jnp.dot(h, wup_ref[:, a0:a0 + ck], preferred_element_type=F32)
        ub_scr[...] = jnp.dot(h, wup_ref[:, b0:b0 + ck], preferred_element_type=F32)
        ut_ref[:, a0:a0 + ck] = ua_scr[tm:tm + 8, :]
        ut_ref[:, b0:b0 + ck] = ub_scr[tm:tm + 8, :]
        ca = cb_ref[:, a0:a0 + ck]
        cb = cb_ref[:, b0:b0 + ck]
        for j in range(CONV_W):
            ca = ca + cw_ref[j:j + 1, a0:a0 + ck] * ua_scr[6 + j:6 + j + tm, :]
            cb = cb + cw_ref[j:j + 1, b0:b0 + ck] * ub_scr[6 + j:6 + j + tm, :]
        gate_scr[:, a0:a0 + ck] = _silu_gate(ca, cb).astype(BF16)
    y_ref[...] = x + jnp.dot(gate_scr[...], wdn_ref[...], preferred_element_type=F32)


def ffn_prompt(x, seq_len, g, wup_bf16, cw, cb, wdn_bf16, *, tm=256, ck=256):
    M, D = x.shape
    d_ff = wdn_bf16.shape[0]
    assert seq_len % tm == 0 and M % seq_len == 0 and d_ff % ck == 0 and CONV_W - 1 <= 8
    n_tiles = M // tm
    y, ut = pl.pallas_call(
        functools.partial(_ffn_prompt_kernel, tm=tm, ck=ck, d_ff=d_ff, tiles_per_seq=seq_len // tm),
        out_shape=(jax.ShapeDtypeStruct((M, D), F32), jax.ShapeDtypeStruct((n_tiles * 8, 2 * d_ff), F32)),
        grid=(n_tiles,),
        in_specs=[
            pl.BlockSpec((tm, D), lambda i: (i, 0)),
            pl.BlockSpec((8, D), lambda i: (jnp.maximum(i * (tm // 8) - 1, 0), 0)),
            _const_spec((1, D)),
            _const_spec(wup_bf16.shape),
            _const_spec(cw.shape),
            _const_spec((1, 2 * d_ff)),
            _const_spec(wdn_bf16.shape),
        ],
        out_specs=(pl.BlockSpec((tm, D), lambda i: (i, 0)), pl.BlockSpec((8, 2 * d_ff), lambda i: (i, 0))),
        scratch_shapes=[pltpu.VMEM((tm + 8, ck), F32), pltpu.VMEM((tm + 8, ck), F32), pltpu.VMEM((tm, d_ff), BF16)],
        compiler_params=_params(("parallel",)),
        name="ffn_prompt",
    )(x, x, g.reshape(1, D), wup_bf16, cw, cb.reshape(1, 2 * d_ff), wdn_bf16)
    return y, ut


def _ffn_sample_kernel(x_ref, g_ref, wa_ref, wb_ref, cwa_ref, cwb_ref, cba_ref, cbb_ref,
                       e1a_ref, e1b_ref, e0a_ref, e0b_ref, wdn_ref, y_ref, ua_ref, ub_ref,
                       h_scr, ua_scr, ub_scr, *, m, t_len):
    c = pl.program_id(0)

    @pl.when(c == 0)
    def _():
        x = x_ref[...]
        r = lax.rsqrt(jnp.mean(x * x, axis=-1, keepdims=True) + EPS)
        h_scr[...] = (x * r * g_ref[...]).astype(BF16)
        y_ref[...] = x
        ua_scr[0:8, :] = jnp.zeros((8, ua_scr.shape[1]), F32)
        ub_scr[0:8, :] = jnp.zeros((8, ub_scr.shape[1]), F32)

    h = h_scr[...]
    ua = jnp.dot(h, wa_ref[...], preferred_element_type=F32)
    ub = jnp.dot(h, wb_ref[...], preferred_element_type=F32)
    ua_ref[...] = ua
    ub_ref[...] = ub
    ua_scr[8:8 + m, :] = ua
    ub_scr[8:8 + m, :] = ub
    t = lax.broadcasted_iota(jnp.int32, ua.shape, 0) % t_len

    def conv(u, u_scr, cw_ref, cb_ref, e1_ref, e0_ref):
        um1 = jnp.where(t >= 1, u_scr[7:7 + m, :], e1_ref[...])
        um2 = jnp.where(t >= 2, u_scr[6:6 + m, :], jnp.where(t == 1, e1_ref[...], e0_ref[...]))
        return cb_ref[...] + cw_ref[0:1, :] * um2 + cw_ref[1:2, :] * um1 + cw_ref[2:3, :] * u

    ca = conv(ua, ua_scr, cwa_ref, cba_ref, e1a_ref, e0a_ref)
    cb = conv(ub, ub_scr, cwb_ref, cbb_ref, e1b_ref, e0b_ref)
    gt = _silu_gate(ca, cb).astype(BF16)
    y_ref[...] += jnp.dot(gt, wdn_ref[...], preferred_element_type=F32)


def ffn_sample(x, t_len, prev, g, wup_bf16, cw, cb, wdn_bf16, *, ck=256):
    M, D = x.shape
    d_ff = wdn_bf16.shape[0]
    assert d_ff % ck == 0 and CONV_W == 3 and t_len >= 2
    nc = d_ff // ck
    e1 = jnp.repeat(prev[:, 1], t_len, axis=0)
    e0 = jnp.repeat(prev[:, 0], t_len, axis=0)
    cb2 = cb.reshape(1, 2 * d_ff)
    a_col = lambda c: (0, c)
    b_col = lambda c: (0, nc + c)
    return pl.pallas_call(
        functools.partial(_ffn_sample_kernel, m=M, t_len=t_len),
        out_shape=(jax.ShapeDtypeStruct((M, D), F32), jax.ShapeDtypeStruct((M, d_ff), F32),
                   jax.ShapeDtypeStruct((M, d_ff), F32)),
        grid=(nc,),
        in_specs=[
            pl.BlockSpec((M, D), lambda c: (0, 0)),
            pl.BlockSpec((1, D), lambda c: (0, 0)),
            pl.BlockSpec((D, ck), a_col), pl.BlockSpec((D, ck), b_col),
            pl.BlockSpec((CONV_W, ck), a_col), pl.BlockSpec((CONV_W, ck), b_col),
            pl.BlockSpec((1, ck), a_col), pl.BlockSpec((1, ck), b_col),
            pl.BlockSpec((M, ck), a_col), pl.BlockSpec((M, ck), b_col),
            pl.BlockSpec((M, ck), a_col), pl.BlockSpec((M, ck), b_col),
            pl.BlockSpec((ck, D), lambda c: (c, 0)),
        ],
        out_specs=(pl.BlockSpec((M, D), lambda c: (0, 0)),
                   pl.BlockSpec((M, ck), a_col), pl.BlockSpec((M, ck), a_col)),
        scratch_shapes=[pltpu.VMEM((M, D), BF16), pltpu.VMEM((M + 8, ck), F32), pltpu.VMEM((M + 8, ck), F32)],
        compiler_params=_params(("arbitrary",)),
        name="ffn_sample",
    )(x, g.reshape(1, D), wup_bf16, wup_bf16, cw, cw, cb2, cb2, e1, e1, e0, e0, wdn_bf16)


def _toeplitz(seg_row, rows, lo, width):
    x = jnp.broadcast_to(seg_row, (rows, seg_row.shape[1]))
    return pltpu.roll(x, 0, 1, stride=1, stride_axis=0)[:, lo:lo + width]


DIL_UNIT = 4


def _dil_prompt_kernel(q_ref, kc_ref, kp_ref, vc_ref, vp_ref, seg_ref, o_ref, lse_ref, bias_scr, kcat_scr, vcat_scr,
                       *, d, n_jc):
    b = pl.program_id(1)
    i = pl.program_id(2)
    halo = N_BACK * d
    tq, tk = N_BACK, 2 * N_BACK

    @pl.when((b == 0) & (i == 0))
    def _():
        for h in range(HEADS_PER_LANE_TILE):
            bias_scr[h] = _toeplitz(seg_ref[h], tq, tq, tk)

    kcat_scr[0:halo, :] = kp_ref[...]
    kcat_scr[halo:, :] = kc_ref[...]
    vcat_scr[0:halo, :] = vp_ref[...]
    vcat_scr[halo:, :] = vc_ref[...]

    def rows(ref, start, n):
        return ref[pl.ds(start, n), :] if d == 1 else ref[pl.ds(start, n, stride=d), :]

    lane = lax.broadcasted_iota(jnp.int32, (tq, LANES), 1)
    col = lax.broadcasted_iota(jnp.int32, (tq, tk), 1)
    first = lane < HEAD_DIM
    heads = range(HEADS_PER_LANE_TILE)

    def unit(u, carry):
        starts, deads, logits, vals = [], [], [], []
        for s in range(DIL_UNIT):
            pidx = u * DIL_UNIT + s
            r, jc = pidx // n_jc, pidx % n_jc
            start = r + d * (jc * N_BACK)
            starts.append(start)
            deads.append(jnp.where((i == 0) & (jc == 0), N_BACK, 0))
            q = rows(q_ref, start, tq) * SCALE
            k = rows(kcat_scr, start, tk).astype(BF16)
            vals.append(rows(vcat_scr, start, tk).astype(BF16))
            for h in heads:
                hm = (lane >= h * HEAD_DIM) & (lane < (h + 1) * HEAD_DIM)
                qh = jnp.where(hm, q, 0.0).astype(BF16)
                logits.append(lax.dot_general(qh, k, (((1,), (1,)), ((), ())), preferred_element_type=F32))
        probs, stats = [], []
        for n, sc in enumerate(logits):
            sc = jnp.where(col < deads[n // HEADS_PER_LANE_TILE], NEG, sc + bias_scr[n % HEADS_PER_LANE_TILE])
            m = jnp.max(sc, axis=-1, keepdims=True)
            e = jnp.exp(sc - m)
            probs.append(e.astype(BF16))
            stats.append((m, jnp.sum(e, axis=-1, keepdims=True)))
        for s in range(DIL_UNIT):
            outs, lses = [], []
            for h in heads:
                n = s * HEADS_PER_LANE_TILE + h
                m, l = stats[n]
                outs.append(jnp.dot(probs[n], vals[s], preferred_element_type=F32) / l)
                lses.append(m + jnp.log(l))
            o_val = jnp.where(first, outs[0], outs[1])
            lse_val = jnp.where(first, lses[0], lses[1])
            if d == 1:
                o_ref[pl.ds(starts[s], tq), :] = o_val
                lse_ref[pl.ds(starts[s], tq), :] = lse_val
            else:
                o_ref[pl.ds(starts[s], tq, stride=d), :] = o_val
                lse_ref[pl.ds(starts[s], tq, stride=d), :] = lse_val
        return carry

    lax.fori_loop(0, d * n_jc // DIL_UNIT, unit, 0)


def dil_prompt(qkv, batch, seq_len, gi, bias_k):
    win, d = DIL_GROUPS[gi]
    assert win // d == N_BACK and seq_len % (d * N_BACK) == 0
    halo = N_BACK * d
    blk = max(halo, min(2048, seq_len))
    n_jc = blk // halo
    assert seq_len % blk == 0 and blk % halo == 0 and (d * n_jc) % DIL_UNIT == 0
    hp_tiles = HD // LANES
    qkv3 = qkv.reshape(batch, seq_len, qkv.shape[1])
    width = 3 * N_BACK
    step = 2 * N_BACK - np.arange(width)
    seg = jnp.where(((step >= 0) & (step <= N_BACK))[:, None], bias_k[np.clip(step, 0, N_BACK)].astype(F32), NEG)
    seg = jnp.transpose(seg)[:, None, :]

    def col(which):
        return (which * N_GROUPS + gi) * hp_tiles

    cur = lambda which: pl.BlockSpec((None, blk, LANES), lambda hp, b, i: (b, i, col(which) + hp))
    prev = lambda which: pl.BlockSpec((None, halo, LANES),
                                      lambda hp, b, i: (b, jnp.maximum(i * n_jc - 1, 0), col(which) + hp))
    out_spec = pl.BlockSpec((None, blk, LANES), lambda hp, b, i: (b, i, hp))
    o, lse = pl.pallas_call(
        functools.partial(_dil_prompt_kernel, d=d, n_jc=n_jc),
        out_shape=(jax.ShapeDtypeStruct((batch, seq_len, HD), F32),) * 2,
        grid=(hp_tiles, batch, seq_len // blk),
        in_specs=[cur(0), cur(1), prev(1), cur(2), prev(2),
                  pl.BlockSpec((HEADS_PER_LANE_TILE, 1, width), lambda hp, b, i: (hp, 0, 0))],
        out_specs=(out_spec, out_spec),
        scratch_shapes=[pltpu.VMEM((HEADS_PER_LANE_TILE, N_BACK, 2 * N_BACK), F32),
                        pltpu.VMEM((halo + blk, LANES), F32), pltpu.VMEM((halo + blk, LANES), F32)],
        compiler_params=_params(("parallel", "arbitrary", "arbitrary")),
        name=f"dil_prompt_g{gi}",
    )(qkv3, qkv3, qkv3, qkv3, qkv3, seg)
    return o.reshape(batch * seq_len, HD), lse.reshape(batch * seq_len, HD)


def _dil_sample_kernel(ct_ref, q_ref, newt_ref, bias_c_ref, bias_n_ref, ot_ref, o_ref, lse_ref, *, hb, t_len, L):
    lane = lax.broadcasted_iota(jnp.int32, (HEAD_DIM, LANES), 1)
    nt = (((1,), (1,)), ((), ()))
    for hh in range(hb):
        q = (q_ref[hh] * SCALE).astype(BF16)
        kt, vt = ct_ref[0, hh], ct_ref[1, hh]
        knt, vnt = newt_ref[0, hh], newt_ref[1, hh]
        s_c = jnp.dot(q, kt.astype(BF16), preferred_element_type=F32) + bias_c_ref[hh]
        s_n = jnp.dot(q, knt.astype(BF16), preferred_element_type=F32) + bias_n_ref[hh]
        m = jnp.maximum(jnp.max(s_c, axis=1, keepdims=True), jnp.max(s_n, axis=1, keepdims=True))
        e_c = jnp.exp(s_c - m)
        e_n = jnp.exp(s_n - m)
        l = jnp.sum(e_c, axis=1, keepdims=True) + jnp.sum(e_n, axis=1, keepdims=True)
        pv = (lax.dot_general(e_c.astype(BF16), vt.astype(BF16), nt, preferred_element_type=F32)
              + lax.dot_general(e_n.astype(BF16), vnt.astype(BF16), nt, preferred_element_type=F32))
        o_ref[hh] = pv / l
        lse_ref[hh] = jnp.broadcast_to(m + jnp.log(l), (t_len, HEAD_DIM))
        for kv, (old, new) in enumerate(((kt, knt), (vt, vnt))):
            moved = pltpu.roll(old, L - t_len, 1)
            if L > LANES:
                ot_ref[kv, hh, :, :L - LANES] = moved[:, :L - LANES]
            ot_ref[kv, hh, :, L - LANES:] = jnp.where(lane >= LANES - t_len, new, moved[:, L - LANES:])


def dil_sample(cache, new_kv, q_s, dec_batch, t_len, gi, bias_k):
    win, d = DIL_GROUPS[gi]
    L = cache.shape[1]
    assert L == win and win // d == N_BACK and L % LANES == 0 and t_len <= LANES
    hb = min(N_HEADS, max(1, (4 << 20) // (2 * HEAD_DIM * L * 4)))
    assert N_HEADS % hb == 0
    ct = jnp.transpose(cache, (0, 2, 3, 4, 1))
    newt = jnp.pad(jnp.transpose(new_kv, (0, 2, 3, 4, 1)), ((0, 0),) * 4 + ((LANES - t_len, 0),))
    qh = jnp.transpose(q_s, (0, 2, 1, 3))
    tq = np.arange(t_len)[:, None]
    back = L + tq - np.arange(L)[None, :]
    ok_c = (back % d == 0) & (back // d <= N_BACK)
    bias_kt = bias_k.T.astype(F32)
    bias_c = jnp.where(ok_c[None], bias_kt[:, np.clip(back // d, 0, N_BACK)], NEG)
    t2 = np.arange(LANES)[None, :] - (LANES - t_len)
    ok_n = (t2 >= 0) & (t2 <= tq) & ((tq - t2) % d == 0)
    bias_n = jnp.where(ok_n[None], bias_kt[:, np.clip((tq - t2) // d, 0, N_BACK)], NEG)
    cache_spec = pl.BlockSpec((None, 2, hb, HEAD_DIM, L), lambda b, j: (b, 0, j, 0, 0))
    row_spec = pl.BlockSpec((None, hb, t_len, HEAD_DIM), lambda b, j: (b, j, 0, 0))
    ot, o, lse = pl.pallas_call(
        functools.partial(_dil_sample_kernel, hb=hb, t_len=t_len, L=L),
        out_shape=(jax.ShapeDtypeStruct(ct.shape, F32),
                   jax.ShapeDtypeStruct(qh.shape, F32), jax.ShapeDtypeStruct(qh.shape, F32)),
        grid=(dec_batch, N_HEADS // hb),
        in_specs=[
            cache_spec,
            row_spec,
            pl.BlockSpec((None, 2, hb, HEAD_DIM, LANES), lambda b, j: (b, 0, j, 0, 0)),
            pl.BlockSpec((hb, t_len, L), lambda b, j: (j, 0, 0)),
            pl.BlockSpec((hb, t_len, LANES), lambda b, j: (j, 0, 0)),
        ],
        out_specs=(cache_spec, row_spec, row_spec),
        compiler_params=_params(("parallel", "parallel")),
        name=f"dil_sample_g{gi}",
    )(ct, qh, newt, bias_c, bias_n)
    to_rows = lambda a: jnp.transpose(a, (0, 2, 1, 3)).reshape(dec_batch * t_len, HD)
    return to_rows(o), to_rows(lse), jnp.transpose(ot, (0, 4, 1, 2, 3))


def _kv_prep_kernel(kv_ref, mean_ref, k16_ref, vt16_ref):
    k = kv_ref[:, :HD]
    mean_ref[...] = jnp.broadcast_to(jnp.sum(k, axis=0, keepdims=True) * (1.0 / MOBA_BLOCK), mean_ref.shape)
    for hp in range(HD // LANES):
        k16_ref[hp] = k[:, hp * LANES:(hp + 1) * LANES].astype(BF16)
    vt16_ref[...] = kv_ref[:, HD:].T.astype(BF16)


def kv_prep(kvb):
    rows = kvb.shape[0]
    n_blocks = rows // MOBA_BLOCK
    hp_tiles = HD // LANES
    means, k16, vt16 = pl.pallas_call(
        _kv_prep_kernel,
        out_shape=(jax.ShapeDtypeStruct((n_blocks, 8, HD), F32),
                   jax.ShapeDtypeStruct((hp_tiles, rows, LANES), BF16),
                   jax.ShapeDtypeStruct((HD, rows), BF16)),
        grid=(n_blocks,),
        in_specs=[pl.BlockSpec((MOBA_BLOCK, 2 * HD), lambda i: (i, 0))],
        out_specs=(pl.BlockSpec((None, 8, HD), lambda i: (i, 0, 0)),
                   pl.BlockSpec((hp_tiles, MOBA_BLOCK, LANES), lambda i: (0, i, 0)),
                   pl.BlockSpec((HD, MOBA_BLOCK), lambda i: (0, i))),
        compiler_params=_params(("parallel",)),
        name="kv_prep",
    )(kvb)
    return means[:, 0], k16, vt16


def _top_blocks(sc, blk, n_valid, n_sel, axis):
    sel = jnp.zeros(sc.shape, F32)
    big = float(sc.shape[axis])
    for it in range(n_sel):
        mx = jnp.max(sc, axis=axis, keepdims=True)
        idx = jnp.min(jnp.where(sc == mx, blk, big), axis=axis, keepdims=True)
        hit = blk == idx
        sel = jnp.maximum(sel, jnp.where(hit, jnp.where(it < n_valid, 1.0, 0.0), 0.0))
        sc = jnp.where(hit, -jnp.inf, sc)
    return sel


def _moba_prompt_kernel(q_ref, k_ref, vt_ref, km_ref, seg_ref, o_ref,
                        m_scr, l_scr, acc_scr, selb_scr, bias_scr, *, nb, n_sel, kb):
    b = pl.program_id(1)
    i = pl.program_id(2)
    tb = MOBA_BLOCK

    def head_mask(shape, axis, h):
        idx = lax.broadcasted_iota(jnp.int32, shape, axis)
        return (idx >= h * HEAD_DIM) & (idx < (h + 1) * HEAD_DIM)

    @pl.when((b == 0) & (i == 0))
    def _():
        row = lax.broadcasted_iota(jnp.int32, (tb, tb), 0)
        colm = lax.broadcasted_iota(jnp.int32, (tb, tb), 1)
        for h in range(HEADS_PER_LANE_TILE):
            def fill(delta, carry, h=h):
                bias_scr[h, delta] = _toeplitz(seg_ref[h, pl.ds(delta, 1), :], tb, tb, tb) * LOG2E
                return carry
            lax.fori_loop(0, nb, fill, 0)
            bias_scr[h, 0] = jnp.where(colm >= row, bias_scr[h, 0], NEG)

    heads = range(HEADS_PER_LANE_TILE)
    m_scr[...] = jnp.full_like(m_scr, NEG)
    l_scr[...] = jnp.zeros_like(l_scr)
    acc_scr[...] = jnp.zeros_like(acc_scr)
    q_raw = q_ref[...]
    km = km_ref[...]
    blk = lax.broadcasted_iota(jnp.int32, (nb, tb), 0).astype(F32)
    i_f = i.astype(F32)
    for h in heads:
        kmh = jnp.where(head_mask(km.shape, 1, h), km, 0.0)
        sc = lax.dot_general(kmh, q_raw, (((1,), (1,)), ((), ())), precision=HIGHEST, preferred_element_type=F32)
        sc = jnp.where(blk < i_f, sc, -jnp.inf)
        sel = _top_blocks(sc, blk, i, n_sel, axis=0) + jnp.where(blk == i_f, 1.0, 0.0)
        selb_scr[h] = jnp.where(sel > 0.5, 0.0, NEG)

    q = q_raw * (SCALE * LOG2E)
    qhs = [jnp.where(head_mask(q.shape, 1, h), q, 0.0).astype(BF16) for h in heads]

    def key_tile(kt):
        k0 = pl.multiple_of(kt * (kb * tb), kb * tb)
        k = k_ref[pl.ds(k0, kb * tb), :]
        vt = vt_ref[:, pl.ds(k0, kb * tb)]
        raw = [[lax.dot_general(k[c * tb:(c + 1) * tb], qhs[h], (((1,), (1,)), ((), ())),
                                preferred_element_type=F32) for c in range(kb)] for h in heads]
        parts, shifts, alphas = [], [], []
        for h in heads:
            ps, sbs, cms = [], [], []
            for c in range(kb):
                jj = kt * kb + c
                p = raw[h][c] + bias_scr[h, jnp.maximum(i - jj, 0)]
                sb = selb_scr[h, pl.ds(jj, 1), :]
                ps.append(p)
                sbs.append(sb)
                cms.append(jnp.max(p, axis=0, keepdims=True) + sb)
            m_old = m_scr[h]
            m_new = jnp.maximum(m_old, functools.reduce(jnp.maximum, cms))
            parts.append(ps)
            shifts.append([m_new - sb for sb in sbs])
            alphas.append(jnp.exp2(m_old - m_new))
            m_scr[h] = m_new
        for h in heads:
            acc = alphas[h] * acc_scr[h]
            l_new = alphas[h] * l_scr[h]
            for c in range(kb):
                e = jnp.exp2(parts[h][c] - shifts[h][c])
                l_new = l_new + jnp.sum(e, axis=0, keepdims=True)
                acc = acc + jnp.dot(vt[h * HEAD_DIM:(h + 1) * HEAD_DIM, c * tb:(c + 1) * tb], e.astype(BF16),
                                    preferred_element_type=F32)
            l_scr[h] = l_new
            acc_scr[h] = acc

    own_tile = i // kb
    key_tile(own_tile)

    def past_tile(kt, carry):
        key_tile(kt)
        return carry

    lax.fori_loop(0, own_tile, past_tile, 0)
    o_t = jnp.concatenate([acc_scr[h] / l_scr[h] for h in heads], axis=0)
    o_ref[...] = o_t.T


def moba_prompt(q, k16, vt16, k_means, tab, batch, seq_len):
    tb = MOBA_BLOCK
    kb = 4
    assert seq_len % (kb * tb) == 0
    nb = seq_len // tb
    n_sel = min(MOBA_TOPK, nb)
    hp_tiles = HD // LANES
    dist = np.arange(nb)[:, None] * tb + np.arange(2 * tb)[None, :] - tb
    seg = tab.T.astype(F32)[:, np.clip(dist, 0, tab.shape[0] - 1)]
    km3 = k_means.reshape(batch, nb, HD)
    return pl.pallas_call(
        functools.partial(_moba_prompt_kernel, nb=nb, n_sel=n_sel, kb=kb),
        out_shape=jax.ShapeDtypeStruct(q.shape, F32),
        grid=(hp_tiles, batch, nb),
        in_specs=[
            pl.BlockSpec((tb, LANES), lambda hp, b, i: (b * nb + i, hp)),
            pl.BlockSpec((None, seq_len, LANES), lambda hp, b, i: (hp, b, 0)),
            pl.BlockSpec((LANES, seq_len), lambda hp, b, i: (hp, b)),
            pl.BlockSpec((None, nb, LANES), lambda hp, b, i: (b, 0, hp)),
            pl.BlockSpec((HEADS_PER_LANE_TILE, nb, 2 * tb), lambda hp, b, i: (hp, 0, 0)),
        ],
        out_specs=pl.BlockSpec((tb, LANES), lambda hp, b, i: (b * nb + i, hp)),
        scratch_shapes=[pltpu.VMEM((HEADS_PER_LANE_TILE, 1, tb), F32),
                        pltpu.VMEM((HEADS_PER_LANE_TILE, 1, tb), F32),
                        pltpu.VMEM((HEADS_PER_LANE_TILE, HEAD_DIM, tb), F32),
                        pltpu.VMEM((HEADS_PER_LANE_TILE, nb, tb), F32),
                        pltpu.VMEM((HEADS_PER_LANE_TILE, nb, tb, tb), F32)],
        compiler_params=_params(("parallel", "arbitrary", "arbitrary")),
        name="moba_prompt",
    )(q, k16, vt16, km3, seg)


def _moba_page_kernel(pt_ref, *rest, t_len, pp):
    del pt_ref
    pool_refs, (qbd_ref, bias_ref, ksum_ref, m_ref, l_ref, acc_ref) = rest[:pp], rest[pp:]
    j = pl.program_id(1)
    qbd = qbd_ref[...].astype(BF16)
    hpg = MXU_TILE // HEAD_DIM
    n_grp = N_HEADS // hpg
    rpg = hpg * t_len
    nt = (((1,), (1,)), ((), ()))
    kts = [ref[0] for ref in pool_refs]
    logits = []
    for kt in kts:
        k16 = kt.astype(BF16)
        logits.append(jnp.concatenate(
            [jnp.dot(qbd[g * rpg:(g + 1) * rpg, g * MXU_TILE:(g + 1) * MXU_TILE], k16[g * MXU_TILE:(g + 1) * MXU_TILE],
                     preferred_element_type=F32) for g in range(n_grp)], axis=0))
    ms, ls, es = [], [], []
    for u, s in enumerate(logits):
        s = s + bias_ref[u]
        m = jnp.max(s, axis=1, keepdims=True)
        e = jnp.exp(s - m)
        ms.append(m)
        ls.append(jnp.sum(e, axis=1, keepdims=True))
        es.append(e.astype(BF16))
    lane = lax.broadcasted_iota(jnp.int32, (t_len, MXU_TILE), 1)
    for u in range(pp):
        v16 = pool_refs[u][1].astype(BF16)
        outs = []
        for g in range(n_grp):
            pv = lax.dot_general(es[u][g * rpg:(g + 1) * rpg], v16[g * MXU_TILE:(g + 1) * MXU_TILE], nt,
                                 preferred_element_type=F32)
            out = pv[0:t_len]
            for hl in range(1, hpg):
                out = jnp.where(lane >= hl * HEAD_DIM, pv[hl * t_len:(hl + 1) * t_len], out)
            outs.append(out)
        acc_ref[u * t_len:(u + 1) * t_len, :] = jnp.concatenate(outs, axis=1)

    @pl.when(j == 0)
    def _():
        ksum_ref[...] = jnp.zeros_like(ksum_ref)
        m_ref[...] = jnp.zeros_like(m_ref)
        l_ref[...] = jnp.zeros_like(l_ref)

    def put(ref, cols):
        lane = lax.broadcasted_iota(jnp.int32, ref.shape, 1)
        val = ref[...]
        for u, col in enumerate(cols):
            val = jnp.where(lane == j * pp + u, col, val)
        ref[...] = val

    put(ksum_ref, [jnp.sum(kt, axis=1, keepdims=True) for kt in kts])
    put(m_ref, ms)
    put(l_ref, ls)


def _moba_combine_kernel(ks_ref, m_ref, l_ref, acc_ref, qbd_ref, knt_ref, vnew_ref, bias_ref, pair_ref, dmask_ref,
                         perm_ref, o_ref, *, n_pages, n_blocks, n_sel, t_len):
    qbd = qbd_ref[...]
    pair = pair_ref[...]
    sc_pages = jnp.dot(qbd, ks_ref[...], precision=HIGHEST, preferred_element_type=F32)
    sc = jnp.dot(sc_pages, pair, precision=HIGHEST, preferred_element_type=F32) * (1.0 / MOBA_BLOCK)
    lane = lax.broadcasted_iota(jnp.int32, sc.shape, 1).astype(F32)
    sc = jnp.where(lane < n_blocks, sc, -jnp.inf)
    sel = _top_blocks(sc, lane, n_sel, n_sel, axis=1)
    picked = lax.dot_general(sel, pair, (((1,), (1,)), ((), ())), preferred_element_type=F32) > 0.5
    m_p = m_ref[...]
    s_own = jnp.dot(qbd.astype(BF16), knt_ref[...].astype(BF16), preferred_element_type=F32) + bias_ref[...]
    m_tot = jnp.maximum(jnp.max(jnp.where(picked, m_p, NEG), axis=1, keepdims=True),
                        jnp.max(s_own, axis=1, keepdims=True))
    w = jnp.where(picked, jnp.exp(m_p - m_tot), 0.0)
    e_own = jnp.exp(s_own - m_tot)
    l_tot = jnp.sum(w * l_ref[...], axis=1, keepdims=True) + jnp.sum(e_own, axis=1, keepdims=True)
    perm = perm_ref[...]
    to_query_major = lambda a: jnp.dot(perm, a, precision=HIGHEST, preferred_element_type=F32)
    w = to_query_major(w)
    e_own = to_query_major(e_own)
    l_tot = to_query_major(jnp.broadcast_to(l_tot, w.shape))[:, 0:1]
    dmask = dmask_ref[...]
    vnew = vnew_ref[...].astype(BF16)
    rows = []
    for t in range(t_len):
        hs = slice(t * N_HEADS, (t + 1) * N_HEADS)
        a = jnp.dot(w[hs, :n_pages], acc_ref[t], precision=HIGHEST, preferred_element_type=F32)
        a = a + jnp.dot(e_own[hs].astype(BF16), vnew, preferred_element_type=F32)
        num = jnp.sum(a * dmask, axis=0, keepdims=True)
        den = jnp.sum(l_tot[hs] * dmask, axis=0, keepdims=True)
        rows.append(num / den)
    o_ref[...] = jnp.concatenate(rows, axis=0)


def moba_sample(q_s, kvb_s, pool, page_table, tab, dec_batch, t_len):
    n_pool, page = pool.shape[0], pool.shape[1]
    n_pages = page_table.shape[1]
    past = n_pages * page
    assert MOBA_BLOCK % page == 0 and past % MOBA_BLOCK == 0 and t_len <= MOBA_BLOCK and t_len == 8
    ppb = MOBA_BLOCK // page
    n_blocks = past // MOBA_BLOCK
    n_sel = min(MOBA_TOPK, n_blocks + 1)
    assert n_blocks >= n_sel, "fewer cached blocks than top-k picks is not supported"
    n_col = t_len * N_HEADS
    assert n_col == LANES and page == LANES and n_pages <= LANES
    pool_t = jnp.transpose(pool, (0, 2, 3, 4, 1)).reshape(n_pool, 2, HD, page)
    q4 = q_s.reshape(dec_batch, t_len, N_HEADS, HEAD_DIM) * SCALE
    qbd = jnp.einsum('bthx,hg->bhtgx', q4, np.eye(N_HEADS, dtype=np.float32)).reshape(dec_batch, n_col, HD)
    rev = tab.T.astype(F32)[:, past + t_len - 1 - np.arange(past + t_len)]
    by_t = jnp.stack([rev[:, t_len - 1 - t:t_len - 1 - t + past] for t in range(t_len)], axis=1)
    bias_pages = jnp.transpose(by_t.reshape(N_HEADS, t_len, n_pages, page), (2, 0, 1, 3)).reshape(n_pages, n_col, page)
    d_own = np.arange(t_len)[:, None] - np.arange(LANES)[None, :]
    ok_own = (d_own >= 0) & (np.arange(LANES)[None, :] < t_len)
    bias_own = jnp.where(ok_own[None], tab.T.astype(F32)[:, np.clip(d_own, 0, t_len)], NEG)
    bias_own = bias_own.reshape(n_col, LANES)
    dmask = jnp.asarray(np.arange(N_HEADS)[:, None] == (np.arange(HD) // HEAD_DIM)[None, :], F32)
    r_qh = np.arange(n_col)
    perm = jnp.asarray((r_qh % N_HEADS)[:, None] * t_len + (r_qh // N_HEADS)[:, None] == np.arange(n_col)[None, :], F32)
    pair = jnp.asarray((np.arange(LANES)[:, None] // ppb == np.arange(LANES)[None, :])
                       & (np.arange(LANES)[:, None] < n_pages), F32)
    new3 = kvb_s.reshape(dec_batch, t_len, 2 * HD)
    knt = jnp.pad(jnp.swapaxes(new3[:, :, :HD], 1, 2), ((0, 0), (0, 0), (0, LANES - t_len)))
    vnew = jnp.pad(new3[:, :, HD:], ((0, 0), (0, LANES - t_len), (0, 0)))

    pp = 4 if n_pages % 4 == 0 else 1
    page_spec = lambda u: pl.BlockSpec((None, 2, HD, page), lambda b, j, pt: (pt[b, j * pp + u], 0, 0, 0))
    stat = lambda rows: pl.BlockSpec((None, rows, LANES), lambda b, j, pt: (b, 0, 0))
    ksum, m_p, l_p, acc = pl.pallas_call(
        functools.partial(_moba_page_kernel, t_len=t_len, pp=pp),
        out_shape=(jax.ShapeDtypeStruct((dec_batch, HD, LANES), F32),
                   jax.ShapeDtypeStruct((dec_batch, n_col, LANES), F32),
                   jax.ShapeDtypeStruct((dec_batch, n_col, LANES), F32),
                   jax.ShapeDtypeStruct((dec_batch, n_pages * t_len, HD), F32)),
        grid_spec=pltpu.PrefetchScalarGridSpec(
            num_scalar_prefetch=1,
            grid=(dec_batch, n_pages // pp),
            in_specs=[page_spec(u) for u in range(pp)] + [
                pl.BlockSpec((None, n_col, HD), lambda b, j, pt: (b, 0, 0)),
                pl.BlockSpec((pp, n_col, page), lambda b, j, pt: (j, 0, 0)),
            ],
            out_specs=(stat(HD), stat(n_col), stat(n_col),
                       pl.BlockSpec((None, pp * t_len, HD), lambda b, j, pt: (b, j, 0))),
        ),
        compiler_params=_params(("parallel", "arbitrary")),
        name="moba_page",
    )(page_table, *([pool_t] * pp), qbd, bias_pages)

    acc = jnp.swapaxes(acc.reshape(dec_batch, n_pages, t_len, HD), 1, 2)
    per_b = lambda shape: pl.BlockSpec((None,) + shape, lambda b: (b,) + (0,) * len(shape))
    const = lambda shape: pl.BlockSpec(shape, lambda b: (0,) * len(shape))
    out = pl.pallas_call(
        functools.partial(_moba_combine_kernel, n_pages=n_pages, n_blocks=n_blocks, n_sel=n_sel, t_len=t_len),
        out_shape=jax.ShapeDtypeStruct((dec_batch, t_len, HD), F32),
        grid=(dec_batch,),
        in_specs=[per_b((HD, LANES)), per_b((n_col, LANES)), per_b((n_col, LANES)),
                  per_b((t_len, n_pages, HD)), per_b((n_col, HD)), per_b((HD, LANES)), per_b((LANES, HD)),
                  const((n_col, LANES)), const((LANES, LANES)), const((N_HEADS, HD)), const((n_col, n_col))],
        out_specs=per_b((t_len, HD)),
        compiler_params=_params(("parallel",)),
        name="moba_combine",
    )(ksum, m_p, l_p, acc, qbd, knt, vnew, bias_own, pair, dmask, perm)
    return out.reshape(dec_batch * t_len, HD)


def kernel(x_prompt, x_sample, cache_a_kv_g0, cache_a_kv_g1, cache_a_kv_g2, cache_b_kv_pool, page_table, state_ffn_conv, rel_bias, norm_attn, norm_ffn, w_qkv_a, q_norm_a, k_norm_a, w_o_a, norm_kv, w_kv_b, k_norm_b, w_q_b, q_norm_b, w_o_b, w_up, conv_w, conv_b, w_down):
    B, S, D = x_prompt.shape
    DB, T, _ = x_sample.shape
    depth = norm_attn.shape[0]
    n_self = w_qkv_a.shape[0]
    caches = (cache_a_kv_g0, cache_a_kv_g1, cache_a_kv_g2)
    past = page_table.shape[1] * cache_b_kv_pool.shape[1]
    assert D == HD

    tab = rel_bias[_rel_bucket(jnp.arange(max(S, past + T, DIL_GROUPS[-1][0] + 1)))]
    group_bias = [tab[dil * np.arange(win // dil + 1)] for (win, dil) in DIL_GROUPS]

    ones =jnp.ones((HD,), F32)
    tile_h = lambda gvec, n: jnp.tile(gvec, n * N_HEADS)

    def self_layer(x, l, prompt):
        hg = jnp.concatenate([tile_h(q_norm_a[l], N_GROUPS), tile_h(k_norm_a[l], N_GROUPS), jnp.tile(ones, N_GROUPS)])
        qkv = norm_matmul(x, norm_attn[l], w_qkv_a[l].astype(BF16), hg, 2 * N_GROUPS * HD)
        outs, lses, bufs = [], [], []
        seq = S if prompt else T
        qkv3 = qkv.reshape(-1, seq, qkv.shape[1])
        for gi, (win, dil) in enumerate(DIL_GROUPS):
            keep = min(win, seq)
            part = lambda which: qkv3[:, seq - keep:, (which * N_GROUPS + gi) * HD:(which * N_GROUPS + gi + 1) * HD
                                      ].reshape(-1, keep, N_HEADS, HEAD_DIM)
            new_kv = jnp.stack([part(1), part(2)], axis=2)
            if prompt:
                o, lse = dil_prompt(qkv, B, S, gi, group_bias[gi])
                bufs.append(new_kv)
            else:
                o, lse, buf = dil_sample(caches[gi][l], new_kv, part(0), DB, T, gi, group_bias[gi])
                bufs.append(buf)
            outs.append(o)
            lses.append(lse)
        return merge_proj(outs, lses, w_o_a[l].astype(BF16), x), bufs

    def ffn(x, l, prompt):
        wup, wdn = w_up[l].astype(BF16), w_down[l].astype(BF16)
        if prompt:
            tm = 256
            y, ut = ffn_prompt(x, S, norm_ffn[l], wup, conv_w[l], conv_b[l], wdn, tm=tm)
            cs = ut.reshape(B, S // tm, 8, ut.shape[1])[:, -1, 8 - (CONV_W - 1):]
        else:
            y, ua, ub = ffn_sample(x, T, state_ffn_conv[l], norm_ffn[l], wup, conv_w[l], conv_b[l], wdn)
            u = jnp.concatenate([ua, ub], axis=1)
            cs = u.reshape(DB, T, u.shape[1])[:, T - (CONV_W - 1):]
        return y, cs

    def trunk(x, prompt):
        nb_rows = B if prompt else DB
        a_new = [[] for _ in range(N_GROUPS)]
        conv_new = []
        kvb = None
        prep = None
        for l in range(depth):
            if l < n_self:
                x, bufs = self_layer(x, l, prompt)
                for gi in range(N_GROUPS):
                    a_new[gi].append(bufs[gi])
            else:
                if l == n_self:
                    hg = jnp.concatenate([tile_h(k_norm_b, 1), ones])
                    kvb = norm_matmul(x, norm_kv, w_kv_b.astype(BF16), hg, HD)
                    if prompt:
                        prep = kv_prep(kvb)
                lb = l - n_self
                q = norm_matmul(x, norm_attn[l], w_q_b[lb].astype(BF16), tile_h(q_norm_b[lb], 1), HD)
                if prompt:
                    o = moba_prompt(q, prep[1], prep[2], prep[0], tab, B, S)
                else:
                    o = moba_sample(q, kvb, cache_b_kv_pool, page_table, tab, DB, T)
                x = merge_proj([o], None, w_o_b[lb].astype(BF16), x)
            x, cs = ffn(x, l, prompt)
            conv_new.append(cs)
        seq = S if prompt else T
        kv_new = kvb.reshape(nb_rows, seq, 2, N_HEADS, HEAD_DIM)
        return x.reshape(nb_rows, seq, D), [jnp.stack(a, 0) for a in a_new], kv_new, jnp.stack(conv_new, 0)

    y_p, a_p, kv_p, conv_p = trunk(x_prompt.reshape(B * S, D), True)
    y_s, a_s, kv_s, conv_s = trunk(x_sample.reshape(DB * T, D), False)
    return (y_p, y_s, a_p[0], a_p[1], a_p[2], a_s[0], a_s[1], a_s[2], kv_p, kv_s, conv_p, conv_s)
```

```python
import functools
import math

import jax
import jax.numpy as jnp
import numpy as np
from jax import lax
from jax.experimental import pallas as pl
from jax.experimental.pallas import tpu as pltpu

HEAD_DIM = 64
N_HEADS = 16
HD = N_HEADS * HEAD_DIM
DIL_GROUPS = ((128, 1), (512, 4), (2048, 16))
N_GROUPS = len(DIL_GROUPS)
N_BACK = 128
MOBA_BLOCK = 256
MOBA_TOPK = 3
NUM_BUCKETS = 32
REL_MAX_DIST = 4096
CONV_W = 3
EPS = 1e-6
SCALE = HEAD_DIM ** -0.5
NEG = -1e30
LOG2E = 1.4426950408889634
LANES = 128
MXU_TILE = 256
HEADS_PER_LANE_TILE = LANES // HEAD_DIM
VMEM_LIMIT = 56 * 1024 * 1024

F32 = jnp.float32
BF16 = jnp.bfloat16
HIGHEST = lax.Precision.HIGHEST


def _params(sem, vmem=VMEM_LIMIT):
    return pltpu.CompilerParams(dimension_semantics=sem, vmem_limit_bytes=vmem)


def _const_spec(shape):
    nd = len(shape)
    return pl.BlockSpec(shape, lambda *_: (0,) * nd, pipeline_mode=pl.Buffered(1))


def _rel_bucket(dist):
    n = jnp.maximum(dist, 0)
    exact = NUM_BUCKETS // 2
    nf = jnp.maximum(n, exact).astype(F32)
    large = exact + (jnp.log(nf / exact) / math.log(REL_MAX_DIST / exact) * (NUM_BUCKETS - exact)).astype(jnp.int32)
    return jnp.where(n < exact, n, jnp.minimum(large, NUM_BUCKETS - 1))


def _norm_matmul_kernel(x_ref, g_ref, w_ref, hg_ref, bd_ref, o_ref, h_scr, *, n_norm_tiles):
    j = pl.program_id(1)

    @pl.when(j == 0)
    def _():
        x = x_ref[...]
        r = lax.rsqrt(jnp.mean(x * x, axis=-1, keepdims=True) + EPS)
        h_scr[...] = (x * r * g_ref[...]).astype(BF16)

    y = jnp.dot(h_scr[...], w_ref[...], preferred_element_type=F32)

    @pl.when(j < n_norm_tiles)
    def _():
        y2 = (y * y).astype(BF16)
        ss = jnp.concatenate([jnp.dot(y2[:, c:c + MXU_TILE], bd_ref[...], preferred_element_type=F32)
                              for c in range(0, y.shape[1], MXU_TILE)], axis=1)
        o_ref[...] = y * lax.rsqrt(ss * (1.0 / HEAD_DIM) + EPS) * hg_ref[...]

    @pl.when(j >= n_norm_tiles)
    def _():
        o_ref[...] = y


def norm_matmul(x, g, w_bf16, head_gain, n_norm_cols, *, tn=512):
    M, K = x.shape
    N = w_bf16.shape[1]
    tm = min(1024, M)
    assert M % tm == 0 and N % tn == 0 and n_norm_cols % tn == 0 and tn % MXU_TILE == 0
    hid = np.arange(MXU_TILE) // HEAD_DIM
    bd = jnp.asarray(hid[:, None] == hid[None, :], BF16)
    return pl.pallas_call(
        functools.partial(_norm_matmul_kernel, n_norm_tiles=n_norm_cols // tn),
        out_shape=jax.ShapeDtypeStruct((M, N), F32),
        grid=(M // tm, N // tn),
        in_specs=[
            pl.BlockSpec((tm, K), lambda i, j: (i, 0)),
            pl.BlockSpec((1, K), lambda i, j: (0, 0)),
            pl.BlockSpec((K, tn), lambda i, j: (0, j)),
            pl.BlockSpec((1, tn), lambda i, j: (0, j)),
            pl.BlockSpec((MXU_TILE, MXU_TILE), lambda i, j: (0, 0)),
        ],
        out_specs=pl.BlockSpec((tm, tn), lambda i, j: (i, j)),
        scratch_shapes=[pltpu.VMEM((tm, K), BF16)],
        compiler_params=_params(("parallel", "arbitrary")),
        name="norm_matmul",
    )(x, g.reshape(1, K), w_bf16, head_gain.reshape(1, N), bd)


def _merge_proj_kernel(*refs, n_g):
    o_refs = refs[:n_g]
    lse_refs = refs[n_g:2 * n_g] if n_g > 1 else ()
    w_ref, x_ref, out_ref = refs[-3:]
    if n_g == 1:
        a = o_refs[0][...]
    else:
        ls = [r[...] for r in lse_refs]
        mx = functools.reduce(jnp.maximum, ls)
        ws = [jnp.exp(l - mx) for l in ls]
        den = functools.reduce(lambda p, q: p + q, ws)
        num = functools.reduce(lambda p, q: p + q, [w * r[...] for w, r in zip(ws, o_refs)])
        a = num / den
    out_ref[...] = x_ref[...] + jnp.dot(a.astype(BF16), w_ref[...], preferred_element_type=F32)


def merge_proj(os_, lses, w_bf16, x):
    M, D = x.shape
    n_g = len(os_)
    tm = min(256, M)
    assert M % tm == 0
    row = pl.BlockSpec((tm, D), lambda i: (i, 0))
    ins = list(os_) + (list(lses) if n_g > 1 else [])
    return pl.pallas_call(
        functools.partial(_merge_proj_kernel, n_g=n_g),
        out_shape=jax.ShapeDtypeStruct((M, D), F32),
        grid=(M // tm,),
        in_specs=[row] * len(ins) + [_const_spec(w_bf16.shape), row],
        out_specs=row,
        compiler_params=_params(("parallel",)),
        name="merge_proj",
    )(*ins, w_bf16, x)


def _silu_gate(ca, cb):
    return (ca / (1.0 + jnp.exp(-ca))) * cb


def _ffn_prompt_kernel(x_ref, xp_ref, g_ref, wup_ref, cw_ref, cb_ref, wdn_ref, y_ref, ut_ref,
                       ua_scr, ub_scr, gate_scr, *, tm, ck, d_ff, tiles_per_seq):
    i = pl.program_id(0)
    x = x_ref[...]
    keep = jnp.where(i % tiles_per_seq == 0, 0.0, 1.0)
    xc = jnp.concatenate([xp_ref[...] * keep, x], axis=0)
    r = lax.rsqrt(jnp.mean(xc * xc, axis=-1, keepdims=True) + EPS)
    h = (xc * r * g_ref[...]).astype(BF16)
    for c in range(d_ff // ck):
        a0, b0 = c * ck, d_ff + c * ck
        ua_scr[...] = jnp.dot(h, wup_ref[:, a0:a0 + ck], preferred_element_type=F32)
        ub_scr[...] = jnp.dot(h, wup_ref[:, b0:b0 + ck], preferred_element_type=F32)
        ut_ref[:, a0:a0 + ck] = ua_scr[tm:tm + 8, :]
        ut_ref[:, b0:b0 + ck] = ub_scr[tm:tm + 8, :]
        ca = cb_ref[:, a0:a0 + ck]
        cb = cb_ref[:, b0:b0 + ck]
        for j in range(CONV_W):
            ca = ca + cw_ref[j:j + 1, a0:a0 + ck] * ua_scr[6 + j:6 + j + tm, :]
            cb = cb + cw_ref[j:j + 1, b0:b0 + ck] * ub_scr[6 + j:6 + j + tm, :]
        gate_scr[:, a0:a0 + ck] = _silu_gate(ca, cb).astype(BF16)
    y_ref[...] = x + jnp.dot(gate_scr[...], wdn_ref[...], preferred_element_type=F32)


def ffn_prompt(x, seq_len, g, wup_bf16, cw, cb, wdn_bf16, *, tm=256, ck=256):
    M, D = x.shape
    d_ff = wdn_bf16.shape[0]
    assert seq_len % tm == 0 and M % seq_len == 0 and d_ff % ck == 0 and CONV_W - 1 <= 8
    n_tiles = M // tm
    y, ut = pl.pallas_call(
        functools.partial(_ffn_prompt_kernel, tm=tm, ck=ck, d_ff=d_ff, tiles_per_seq=seq_len // tm),
        out_shape=(jax.ShapeDtypeStruct((M, D), F32), jax.ShapeDtypeStruct((n_tiles * 8, 2 * d_ff), F32)),
        grid=(n_tiles,),
        in_specs=[
            pl.BlockSpec((tm, D), lambda i: (i, 0)),
            pl.BlockSpec((8, D), lambda i: (jnp.maximum(i * (tm // 8) - 1, 0), 0)),
            _const_spec((1, D)),
            _const_spec(wup_bf16.shape),
            _const_spec(cw.shape),
            _const_spec((1, 2 * d_ff)),
            _const_spec(wdn_bf16.shape),
        ],
        out_specs=(pl.BlockSpec((tm, D), lambda i: (i, 0)), pl.BlockSpec((8, 2 * d_ff), lambda i: (i, 0))),
        scratch_shapes=[pltpu.VMEM((tm + 8, ck), F32), pltpu.VMEM((tm + 8, ck), F32), pltpu.VMEM((tm, d_ff), BF16)],
        compiler_params=_params(("parallel",)),
        name="ffn_prompt",
    )(x, x, g.reshape(1, D), wup_bf16, cw, cb.reshape(1, 2 * d_ff), wdn_bf16)
    return y, ut


def _ffn_sample_kernel(x_ref, g_ref, wa_ref, wb_ref, cwa_ref, cwb_ref, cba_ref, cbb_ref,
                       e1a_ref, e1b_ref, e0a_ref, e0b_ref, wdn_ref, y_ref, ua_ref, ub_ref,
                       h_scr, ua_scr, ub_scr, *, m, t_len):
    c = pl.program_id(0)

    @pl.when(c == 0)
    def _():
        x = x_ref[...]
        r = lax.rsqrt(jnp.mean(x * x, axis=-1, keepdims=True) + EPS)
        h_scr[...] = (x * r * g_ref[...]).astype(BF16)
        y_ref[...] = x
        ua_scr[0:8, :] = jnp.zeros((8, ua_scr.shape[1]), F32)
        ub_scr[0:8, :] = jnp.zeros((8, ub_scr.shape[1]), F32)

    h = h_scr[...]
    ua = jnp.dot(h, wa_ref[...], preferred_element_type=F32)
    ub = jnp.dot(h, wb_ref[...], preferred_element_type=F32)
    ua_ref[...] = ua
    ub_ref[...] = ub
    ua_scr[8:8 + m, :] = ua
    ub_scr[8:8 + m, :] = ub
    t = lax.broadcasted_iota(jnp.int32, ua.shape, 0) % t_len

    def conv(u, u_scr, cw_ref, cb_ref, e1_ref, e0_ref):
        um1 = jnp.where(t >= 1, u_scr[7:7 + m, :], e1_ref[...])
        um2 = jnp.where(t >= 2, u_scr[6:6 + m, :], jnp.where(t == 1, e1_ref[...], e0_ref[...]))
        return cb_ref[...] + cw_ref[0:1, :] * um2 + cw_ref[1:2, :] * um1 + cw_ref[2:3, :] * u

    ca = conv(ua, ua_scr, cwa_ref, cba_ref, e1a_ref, e0a_ref)
    cb = conv(ub, ub_scr, cwb_ref, cbb_ref, e1b_ref, e0b_ref)
    gt = _silu_gate(ca, cb).astype(BF16)
    y_ref[...] += jnp.dot(gt, wdn_ref[...], preferred_element_type=F32)


def ffn_sample(x, t_len, prev, g, wup_bf16, cw, cb, wdn_bf16, *, ck=256):
    M, D = x.shape
    d_ff = wdn_bf16.shape[0]
    assert d_ff % ck == 0 and CONV_W == 3 and t_len >= 2
    nc = d_ff // ck
    e1 = jnp.repeat(prev[:, 1], t_len, axis=0)
    e0 = jnp.repeat(prev[:, 0], t_len, axis=0)
    cb2 = cb.reshape(1, 2 * d_ff)
    a_col = lambda c: (0, c)
    b_col = lambda c: (0, nc + c)
    return pl.pallas_call(
        functools.partial(_ffn_sample_kernel, m=M, t_len=t_len),
        out_shape=(jax.ShapeDtypeStruct((M, D), F32), jax.ShapeDtypeStruct((M, d_ff), F32),
                   jax.ShapeDtypeStruct((M, d_ff), F32)),
        grid=(nc,),
        in_specs=[
            pl.BlockSpec((M, D), lambda c: (0, 0)),
            pl.BlockSpec((1, D), lambda c: (0, 0)),
            pl.BlockSpec((D, ck), a_col), pl.BlockSpec((D, ck), b_col),
            pl.BlockSpec((CONV_W, ck), a_col), pl.BlockSpec((CONV_W, ck), b_col),
            pl.BlockSpec((1, ck), a_col), pl.BlockSpec((1, ck), b_col),
            pl.BlockSpec((M, ck), a_col), pl.BlockSpec((M, ck), b_col),
            pl.BlockSpec((M, ck), a_col), pl.BlockSpec((M, ck), b_col),
            pl.BlockSpec((ck, D), lambda c: (c, 0)),
        ],
        out_specs=(pl.BlockSpec((M, D), lambda c: (0, 0)),
                   pl.BlockSpec((M, ck), a_col), pl.BlockSpec((M, ck), a_col)),
        scratch_shapes=[pltpu.VMEM((M, D), BF16), pltpu.VMEM((M + 8, ck), F32), pltpu.VMEM((M + 8, ck), F32)],
        compiler_params=_params(("arbitrary",)),
        name="ffn_sample",
    )(x, g.reshape(1, D), wup_bf16, wup_bf16, cw, cw, cb2, cb2, e1, e1, e0, e0, wdn_bf16)


def _toeplitz(seg_row, rows, lo, width):
    x = jnp.broadcast_to(seg_row, (rows, seg_row.shape[1]))
    return pltpu.roll(x, 0, 1, stride=1, stride_axis=0)[:, lo:lo + width]


DIL_UNIT = 4


def _dil_prompt_kernel(q_ref, kc_ref, kp_ref, vc_ref, vp_ref, seg_ref, o_ref, lse_ref, bias_scr, kcat_scr, vcat_scr,
                       *, d, n_jc):
    b = pl.program_id(1)
    i = pl.program_id(2)
    halo = N_BACK * d
    tq, tk = N_BACK, 2 * N_BACK

    @pl.when((b == 0) & (i == 0))
    def _():
        for h in range(HEADS_PER_LANE_TILE):
            bias_scr[h] = _toeplitz(seg_ref[h], tq, tq, tk)

    kcat_scr[0:halo, :] = kp_ref[...]
    kcat_scr[halo:, :] = kc_ref[...]
    vcat_scr[0:halo, :] = vp_ref[...]
    vcat_scr[halo:, :] = vc_ref[...]

    def rows(ref, start, n):
        return ref[pl.ds(start, n), :] if d == 1 else ref[pl.ds(start, n, stride=d), :]

    lane = lax.broadcasted_iota(jnp.int32, (tq, LANES), 1)
    col = lax.broadcasted_iota(jnp.int32, (tq, tk), 1)
    first = lane < HEAD_DIM
    heads = range(HEADS_PER_LANE_TILE)

    def unit(u, carry):
        starts, deads, logits, vals = [], [], [], []
        for s in range(DIL_UNIT):
            pidx = u * DIL_UNIT + s
            r, jc = pidx // n_jc, pidx % n_jc
            start = r + d * (jc * N_BACK)
            starts.append(start)
            deads.append(jnp.where((i == 0) & (jc == 0), N_BACK, 0))
            q = rows(q_ref, start, tq) * SCALE
            k = rows(kcat_scr, start, tk).astype(BF16)
            vals.append(rows(vcat_scr, start, tk).astype(BF16))
            for h in heads:
                hm = (lane >= h * HEAD_DIM) & (lane < (h + 1) * HEAD_DIM)
                qh = jnp.where(hm, q, 0.0).astype(BF16)
                logits.append(lax.dot_general(qh, k, (((1,), (1,)), ((), ())), preferred_element_type=F32))
        probs, stats = [], []
        for n, sc in enumerate(logits):
            sc = jnp.where(col < deads[n // HEADS_PER_LANE_TILE], NEG, sc + bias_scr[n % HEADS_PER_LANE_TILE])
            m = jnp.max(sc, axis=-1, keepdims=True)
            e = jnp.exp(sc - m)
            probs.append(e.astype(BF16))
            stats.append((m, jnp.sum(e, axis=-1, keepdims=True)))
        for s in range(DIL_UNIT):
            outs, lses = [], []
            for h in heads:
                n = s * HEADS_PER_LANE_TILE + h
                m, l = stats[n]
                outs.append(jnp.dot(probs[n], vals[s], preferred_element_type=F32) / l)
                lses.append(m + jnp.log(l))
            o_val = jnp.where(first, outs[0], outs[1])
            lse_val = jnp.where(first, lses[0], lses[1])
            if d == 1:
                o_ref[pl.ds(starts[s], tq), :] = o_val
                lse_ref[pl.ds(starts[s], tq), :] = lse_val
            else:
                o_ref[pl.ds(starts[s], tq, stride=d), :] = o_val
                lse_ref[pl.ds(starts[s], tq, stride=d), :] = lse_val
        return carry

    lax.fori_loop(0, d * n_jc // DIL_UNIT, unit, 0)


def dil_prompt(qkv, batch, seq_len, gi, bias_k):
    win, d = DIL_GROUPS[gi]
    assert win // d == N_BACK and seq_len % (d * N_BACK) == 0
    halo = N_BACK * d
    blk = max(halo, min(2048, seq_len))
    n_jc = blk // halo
    assert seq_len % blk == 0 and blk % halo == 0 and (d * n_jc) % DIL_UNIT == 0
    hp_tiles = HD // LANES
    qkv3 = qkv.reshape(batch, seq_len, qkv.shape[1])
    width = 3 * N_BACK
    step = 2 * N_BACK - np.arange(width)
    seg = jnp.where(((step >= 0) & (step <= N_BACK))[:, None], bias_k[np.clip(step, 0, N_BACK)].astype(F32), NEG)
    seg = jnp.transpose(seg)[:, None, :]

    def col(which):
        return (which * N_GROUPS + gi) * hp_tiles

    cur = lambda which: pl.BlockSpec((None, blk, LANES), lambda hp, b, i: (b, i, col(which) + hp))
    prev = lambda which: pl.BlockSpec((None, halo, LANES),
                                      lambda hp, b, i: (b, jnp.maximum(i * n_jc - 1, 0), col(which) + hp))
    out_spec = pl.BlockSpec((None, blk, LANES), lambda hp, b, i: (b, i, hp))
    o, lse = pl.pallas_call(
        functools.partial(_dil_prompt_kernel, d=d, n_jc=n_jc),
        out_shape=(jax.ShapeDtypeStruct((batch, seq_len, HD), F32),) * 2,
        grid=(hp_tiles, batch, seq_len // blk),
        in_specs=[cur(0), cur(1), prev(1), cur(2), prev(2),
                  pl.BlockSpec((HEADS_PER_LANE_TILE, 1, width), lambda hp, b, i: (hp, 0, 0))],
        out_specs=(out_spec, out_spec),
        scratch_shapes=[pltpu.VMEM((HEADS_PER_LANE_TILE, N_BACK, 2 * N_BACK), F32),
                        pltpu.VMEM((halo + blk, LANES), F32), pltpu.VMEM((halo + blk, LANES), F32)],
        compiler_params=_params(("parallel", "arbitrary", "arbitrary")),
        name=f"dil_prompt_g{gi}",
    )(qkv3, qkv3, qkv3, qkv3, qkv3, seg)
    return o.reshape(batch * seq_len, HD), lse.reshape(batch * seq_len, HD)


def _dil_sample_kernel(ct_ref, q_ref, newt_ref, bias_c_ref, bias_n_ref, ot_ref, o_ref, lse_ref, *, hb, t_len, L):
    lane = lax.broadcasted_iota(jnp.int32, (HEAD_DIM, LANES), 1)
    nt = (((1,), (1,)), ((), ()))
    for hh in range(hb):
        q = (q_ref[hh] * SCALE).astype(BF16)
        kt, vt = ct_ref[0, hh], ct_ref[1, hh]
        knt, vnt = newt_ref[0, hh], newt_ref[1, hh]
        s_c = jnp.dot(q, kt.astype(BF16), preferred_element_type=F32) + bias_c_ref[hh]
        s_n = jnp.dot(q, knt.astype(BF16), preferred_element_type=F32) + bias_n_ref[hh]
        m = jnp.maximum(jnp.max(s_c, axis=1, keepdims=True), jnp.max(s_n, axis=1, keepdims=True))
        e_c = jnp.exp(s_c - m)
        e_n = jnp.exp(s_n - m)
        l = jnp.sum(e_c, axis=1, keepdims=True) + jnp.sum(e_n, axis=1, keepdims=True)
        pv = (lax.dot_general(e_c.astype(BF16), vt.astype(BF16), nt, preferred_element_type=F32)
              + lax.dot_general(e_n.astype(BF16), vnt.astype(BF16), nt, preferred_element_type=F32))
        o_ref[hh] = pv / l
        lse_ref[hh] = jnp.broadcast_to(m + jnp.log(l), (t_len, HEAD_DIM))
        for kv, (old, new) in enumerate(((kt, knt), (vt, vnt))):
            moved = pltpu.roll(old, L - t_len, 1)
            if L > LANES:
                ot_ref[kv, hh, :, :L - LANES] = moved[:, :L - LANES]
            ot_ref[kv, hh, :, L - LANES:] = jnp.where(lane >= LANES - t_len, new, moved[:, L - LANES:])


def dil_sample(cache, new_kv, q_s, dec_batch, t_len, gi, bias_k):
    win, d = DIL_GROUPS[gi]
    L = cache.shape[1]
    assert L == win and win // d == N_BACK and L % LANES == 0 and t_len <= LANES
    hb = min(N_HEADS, max(1, (4 << 20) // (2 * HEAD_DIM * L * 4)))
    assert N_HEADS % hb == 0
    ct = jnp.transpose(cache, (0, 2, 3, 4, 1))
    newt = jnp.pad(jnp.transpose(new_kv, (0, 2, 3, 4, 1)), ((0, 0),) * 4 + ((LANES - t_len, 0),))
    qh = jnp.transpose(q_s, (0, 2, 1, 3))
    tq = np.arange(t_len)[:, None]
    back = L + tq - np.arange(L)[None, :]
    ok_c = (back % d == 0) & (back // d <= N_BACK)
    bias_kt = bias_k.T.astype(F32)
    bias_c = jnp.where(ok_c[None], bias_kt[:, np.clip(back // d, 0, N_BACK)], NEG)
    t2 = np.arange(LANES)[None, :] - (LANES - t_len)
    ok_n = (t2 >= 0) & (t2 <= tq) & ((tq - t2) % d == 0)
    bias_n = jnp.where(ok_n[None], bias_kt[:, np.clip((tq - t2) // d, 0, N_BACK)], NEG)
    cache_spec = pl.BlockSpec((None, 2, hb, HEAD_DIM, L), lambda b, j: (b, 0, j, 0, 0))
    row_spec = pl.BlockSpec((None, hb, t_len, HEAD_DIM), lambda b, j: (b, j, 0, 0))
    ot, o, lse = pl.pallas_call(
        functools.partial(_dil_sample_kernel, hb=hb, t_len=t_len, L=L),
        out_shape=(jax.ShapeDtypeStruct(ct.shape, F32),
                   jax.ShapeDtypeStruct(qh.shape, F32), jax.ShapeDtypeStruct(qh.shape, F32)),
        grid=(dec_batch, N_HEADS // hb),
        in_specs=[
            cache_spec,
            row_spec,
            pl.BlockSpec((None, 2, hb, HEAD_DIM, LANES), lambda b, j: (b, 0, j, 0, 0)),
            pl.BlockSpec((hb, t_len, L), lambda b, j: (j, 0, 0)),
            pl.BlockSpec((hb, t_len, LANES), lambda b, j: (j, 0, 0)),
        ],
        out_specs=(cache_spec, row_spec, row_spec),
        compiler_params=_params(("parallel", "parallel")),
        name=f"dil_sample_g{gi}",
    )(ct, qh, newt, bias_c, bias_n)
    to_rows = lambda a: jnp.transpose(a, (0, 2, 1, 3)).reshape(dec_batch * t_len, HD)
    return to_rows(o), to_rows(lse), jnp.transpose(ot, (0, 4, 1, 2, 3))


def _kv_prep_kernel(kv_ref, mean_ref, k16_ref, vt16_ref, kvt_ref):
    k = kv_ref[:, :HD]
    mean_ref[...] = jnp.broadcast_to(jnp.sum(k, axis=0, keepdims=True) * (1.0 / MOBA_BLOCK), mean_ref.shape)
    for hp in range(HD // LANES):
        k16_ref[hp] = k[:, hp * LANES:(hp + 1) * LANES].astype(BF16)
    kvt = kv_ref[...].T
    kvt_ref[...] = kvt
    vt16_ref[...] = kvt[HD:].astype(BF16)


def kv_prep(kvb, batch):
    rows = kvb.shape[0]
    n_blocks = rows // MOBA_BLOCK
    per_seq = n_blocks // batch
    hp_tiles = HD // LANES
    means, k16, vt16, kvt = pl.pallas_call(
        _kv_prep_kernel,
        out_shape=(jax.ShapeDtypeStruct((n_blocks, 8, HD), F32),
                   jax.ShapeDtypeStruct((hp_tiles, rows, LANES), BF16),
                   jax.ShapeDtypeStruct((HD, rows), BF16),
                   jax.ShapeDtypeStruct((batch, 2 * HD, rows // batch), F32)),
        grid=(n_blocks,),
        in_specs=[pl.BlockSpec((MOBA_BLOCK, 2 * HD), lambda i: (i, 0))],
        out_specs=(pl.BlockSpec((None, 8, HD), lambda i: (i, 0, 0)),
                   pl.BlockSpec((hp_tiles, MOBA_BLOCK, LANES), lambda i: (0, i, 0)),
                   pl.BlockSpec((HD, MOBA_BLOCK), lambda i: (0, i)),
                   pl.BlockSpec((None, 2 * HD, MOBA_BLOCK), lambda i: (i // per_seq, 0, i % per_seq))),
        compiler_params=_params(("parallel",)),
        name="kv_prep",
    )(kvb)
    return means[:, 0], k16, vt16, kvt


def _top_blocks(sc, blk, n_valid, n_sel, axis):
    sel = jnp.zeros(sc.shape, F32)
    big = float(sc.shape[axis])
    for it in range(n_sel):
        mx = jnp.max(sc, axis=axis, keepdims=True)
        idx = jnp.min(jnp.where(sc == mx, blk, big), axis=axis, keepdims=True)
        hit = blk == idx
        sel = jnp.maximum(sel, jnp.where(hit, jnp.where(it < n_valid, 1.0, 0.0), 0.0))
        sc = jnp.where(hit, -jnp.inf, sc)
    return sel


def _moba_prompt_kernel(q_ref, k_ref, vt_ref, km_ref, seg_ref, o_ref,
                        m_scr, l_scr, acc_scr, selb_scr, bias_scr, *, nb, n_sel, kb):
    b = pl.program_id(1)
    i = pl.program_id(2)
    tb = MOBA_BLOCK

    def head_mask(shape, axis, h):
        idx = lax.broadcasted_iota(jnp.int32, shape, axis)
        return (idx >= h * HEAD_DIM) & (idx < (h + 1) * HEAD_DIM)

    @pl.when((b == 0) & (i == 0))
    def _():
        row = lax.broadcasted_iota(jnp.int32, (tb, tb), 0)
        colm = lax.broadcasted_iota(jnp.int32, (tb, tb), 1)
        for h in range(HEADS_PER_LANE_TILE):
            def fill(delta, carry, h=h):
                bias_scr[h, delta] = _toeplitz(seg_ref[h, pl.ds(delta, 1), :], tb, tb, tb) * LOG2E
                return carry
            lax.fori_loop(0, nb, fill, 0)
            bias_scr[h, 0] = jnp.where(colm >= row, bias_scr[h, 0], NEG)

    heads = range(HEADS_PER_LANE_TILE)
    m_scr[...] = jnp.full_like(m_scr, NEG)
    l_scr[...] = jnp.zeros_like(l_scr)
    acc_scr[...] = jnp.zeros_like(acc_scr)
    q_raw = q_ref[...]
    km = km_ref[...]
    blk = lax.broadcasted_iota(jnp.int32, (nb, tb), 0).astype(F32)
    i_f = i.astype(F32)
    for h in heads:
        kmh = jnp.where(head_mask(km.shape, 1, h), km, 0.0)
        sc = lax.dot_general(kmh, q_raw, (((1,), (1,)), ((), ())), precision=HIGHEST, preferred_element_type=F32)
        sc = jnp.where(blk < i_f, sc, -jnp.inf)
        sel = _top_blocks(sc, blk, i, n_sel, axis=0) + jnp.where(blk == i_f, 1.0, 0.0)
        selb_scr[h] = jnp.where(sel > 0.5, 0.0, NEG)

    q = q_raw * (SCALE * LOG2E)
    qhs = [jnp.where(head_mask(q.shape, 1, h), q, 0.0).astype(BF16) for h in heads]

    def key_tile(kt):
        k0 = pl.multiple_of(kt * (kb * tb), kb * tb)
        k = k_ref[pl.ds(k0, kb * tb), :]
        vt = vt_ref[:, pl.ds(k0, kb * tb)]
        raw = [[lax.dot_general(k[c * tb:(c + 1) * tb], qhs[h], (((1,), (1,)), ((), ())),
                                preferred_element_type=F32) for c in range(kb)] for h in heads]
        parts, shifts, alphas = [], [], []
        for h in heads:
            ps, sbs, cms = [], [], []
            for c in range(kb):
                jj = kt * kb + c
                p = raw[h][c] + bias_scr[h, jnp.maximum(i - jj, 0)]
                sb = selb_scr[h, pl.ds(jj, 1), :]
                ps.append(p)
                sbs.append(sb)
                cms.append(jnp.max(p, axis=0, keepdims=True) + sb)
            m_old = m_scr[h]
            m_new = jnp.maximum(m_old, functools.reduce(jnp.maximum, cms))
            parts.append(ps)
            shifts.append([m_new - sb for sb in sbs])
            alphas.append(jnp.exp2(m_old - m_new))
            m_scr[h] = m_new
        for h in heads:
            acc = alphas[h] * acc_scr[h]
            l_new = alphas[h] * l_scr[h]
            for c in range(kb):
                e = jnp.exp2(parts[h][c] - shifts[h][c])
                l_new = l_new + jnp.sum(e, axis=0, keepdims=True)
                acc = acc + jnp.dot(vt[h * HEAD_DIM:(h + 1) * HEAD_DIM, c * tb:(c + 1) * tb], e.astype(BF16),
                                    preferred_element_type=F32)
            l_scr[h] = l_new
            acc_scr[h] = acc

    own_tile = i // kb
    key_tile(own_tile)

    def past_tile(kt, carry):
        key_tile(kt)
        return carry

    lax.fori_loop(0, own_tile, past_tile, 0)
    o_t = jnp.concatenate([acc_scr[h] / l_scr[h] for h in heads], axis=0)
    o_ref[...] = o_t.T


def moba_prompt(q, k16, vt16, k_means, tab, batch, seq_len):
    tb = MOBA_BLOCK
    kb = 4
    assert seq_len % (kb * tb) == 0
    nb = seq_len // tb
    n_sel = min(MOBA_TOPK, nb)
    hp_tiles = HD // LANES
    dist = np.arange(nb)[:, None] * tb + np.arange(2 * tb)[None, :] - tb
    seg = tab.T.astype(F32)[:, np.clip(dist, 0, tab.shape[0] - 1)]
    km3 = k_means.reshape(batch, nb, HD)
    return pl.pallas_call(
        functools.partial(_moba_prompt_kernel, nb=nb, n_sel=n_sel, kb=kb),
        out_shape=jax.ShapeDtypeStruct(q.shape, F32),
        grid=(hp_tiles, batch, nb),
        in_specs=[
            pl.BlockSpec((tb, LANES), lambda hp, b, i: (b * nb + i, hp)),
            pl.BlockSpec((None, seq_len, LANES), lambda hp, b, i: (hp, b, 0)),
            pl.BlockSpec((LANES, seq_len), lambda hp, b, i: (hp, b)),
            pl.BlockSpec((None, nb, LANES), lambda hp, b, i: (b, 0, hp)),
            pl.BlockSpec((HEADS_PER_LANE_TILE, nb, 2 * tb), lambda hp, b, i: (hp, 0, 0)),
        ],
        out_specs=pl.BlockSpec((tb, LANES), lambda hp, b, i: (b * nb + i, hp)),
        scratch_shapes=[pltpu.VMEM((HEADS_PER_LANE_TILE, 1, tb), F32),
                        pltpu.VMEM((HEADS_PER_LANE_TILE, 1, tb), F32),
                        pltpu.VMEM((HEADS_PER_LANE_TILE, HEAD_DIM, tb), F32),
                        pltpu.VMEM((HEADS_PER_LANE_TILE, nb, tb), F32),
                        pltpu.VMEM((HEADS_PER_LANE_TILE, nb, tb, tb), F32)],
        compiler_params=_params(("parallel", "arbitrary", "arbitrary")),
        name="moba_prompt",
    )(q, k16, vt16, km3, seg)


def _moba_page_kernel(pt_ref, *rest, t_len, pp):
    del pt_ref
    pool_refs, (qbd_ref, bias_ref, ksum_ref, m_ref, l_ref, acc_ref) = rest[:pp], rest[pp:]
    j = pl.program_id(1)
    qbd = qbd_ref[...].astype(BF16)
    hpg = MXU_TILE // HEAD_DIM
    n_grp = N_HEADS // hpg
    rpg = hpg * t_len
    nt = (((1,), (1,)), ((), ()))
    kts = [ref[0] for ref in pool_refs]
    logits = []
    for kt in kts:
        k16 = kt.astype(BF16)
        logits.append(jnp.concatenate(
            [jnp.dot(qbd[g * rpg:(g + 1) * rpg, g * MXU_TILE:(g + 1) * MXU_TILE], k16[g * MXU_TILE:(g + 1) * MXU_TILE],
                     preferred_element_type=F32) for g in range(n_grp)], axis=0))
    ms, ls, es = [], [], []
    for u, s in enumerate(logits):
        s = s + bias_ref[u]
        m = jnp.max(s, axis=1, keepdims=True)
        e = jnp.exp(s - m)
        ms.append(m)
        ls.append(jnp.sum(e, axis=1, keepdims=True))
        es.append(e.astype(BF16))
    lane = lax.broadcasted_iota(jnp.int32, (t_len, MXU_TILE), 1)
    for u in range(pp):
        v16 = pool_refs[u][1].astype(BF16)
        outs = []
        for g in range(n_grp):
            pv = lax.dot_general(es[u][g * rpg:(g + 1) * rpg], v16[g * MXU_TILE:(g + 1) * MXU_TILE], nt,
                                 preferred_element_type=F32)
            out = pv[0:t_len]
            for hl in range(1, hpg):
                out = jnp.where(lane >= hl * HEAD_DIM, pv[hl * t_len:(hl + 1) * t_len], out)
            outs.append(out)
        acc_ref[u * t_len:(u + 1) * t_len, :] = jnp.concatenate(outs, axis=1)

    @pl.when(j == 0)
    def _():
        ksum_ref[...] = jnp.zeros_like(ksum_ref)
        m_ref[...] = jnp.zeros_like(m_ref)
        l_ref[...] = jnp.zeros_like(l_ref)

    def put(ref, cols):
        lane = lax.broadcasted_iota(jnp.int32, ref.shape, 1)
        val = ref[...]
        for u, col in enumerate(cols):
            val = jnp.where(lane == j * pp + u, col, val)
        ref[...] = val

    put(ksum_ref, [jnp.sum(kt, axis=1, keepdims=True) for kt in kts])
    put(m_ref, ms)
    put(l_ref, ls)


def _moba_combine_kernel(ks_ref, m_ref, l_ref, acc_ref, qbd_ref, knt_ref, vnew_ref, bias_ref, pair_ref, dmask_ref,
                         perm_ref, o_ref, *, n_pages, n_blocks, n_sel, t_len):
    qbd = qbd_ref[...]
    pair = pair_ref[...]
    sc_pages = jnp.dot(qbd, ks_ref[...], precision=HIGHEST, preferred_element_type=F32)
    sc = jnp.dot(sc_pages, pair, precision=HIGHEST, preferred_element_type=F32) * (1.0 / MOBA_BLOCK)
    lane = lax.broadcasted_iota(jnp.int32, sc.shape, 1).astype(F32)
    sc = jnp.where(lane < n_blocks, sc, -jnp.inf)
    sel = _top_blocks(sc, lane, n_sel, n_sel, axis=1)
    picked = lax.dot_general(sel, pair, (((1,), (1,)), ((), ())), preferred_element_type=F32) > 0.5
    m_p = m_ref[...]
    s_own = jnp.dot(qbd.astype(BF16), knt_ref[...].astype(BF16), preferred_element_type=F32) + bias_ref[...]
    m_tot = jnp.maximum(jnp.max(jnp.where(picked, m_p, NEG), axis=1, keepdims=True),
                        jnp.max(s_own, axis=1, keepdims=True))
    w = jnp.where(picked, jnp.exp(m_p - m_tot), 0.0)
    e_own = jnp.exp(s_own - m_tot)
    l_tot = jnp.sum(w * l_ref[...], axis=1, keepdims=True) + jnp.sum(e_own, axis=1, keepdims=True)
    perm = perm_ref[...]
    to_query_major = lambda a: jnp.dot(perm, a, precision=HIGHEST, preferred_element_type=F32)
    w = to_query_major(w)
    e_own = to_query_major(e_own)
    l_tot = to_query_major(jnp.broadcast_to(l_tot, w.shape))[:, 0:1]
    dmask = dmask_ref[...]
    vnew = vnew_ref[...].astype(BF16)
    rows = []
    for t in range(t_len):
        hs = slice(t * N_HEADS, (t + 1) * N_HEADS)
        a = jnp.dot(w[hs, :n_pages], acc_ref[t], precision=HIGHEST, preferred_element_type=F32)
        a = a + jnp.dot(e_own[hs].astype(BF16), vnew, preferred_element_type=F32)
        num = jnp.sum(a * dmask, axis=0, keepdims=True)
        den = jnp.sum(l_tot[hs] * dmask, axis=0, keepdims=True)
        rows.append(num / den)
    o_ref[...] = jnp.concatenate(rows, axis=0)


def moba_sample(q_s, kvb_s, pool, page_table, tab, dec_batch, t_len):
    n_pool, page = pool.shape[0], pool.shape[1]
    n_pages = page_table.shape[1]
    past = n_pages * page
    assert MOBA_BLOCK % page == 0 and past % MOBA_BLOCK == 0 and t_len <= MOBA_BLOCK and t_len == 8
    ppb = MOBA_BLOCK // page
    n_blocks = past // MOBA_BLOCK
    n_sel = min(MOBA_TOPK, n_blocks + 1)
    assert n_blocks >= n_sel, "fewer cached blocks than top-k picks is not supported"
    n_col = t_len * N_HEADS
    assert n_col == LANES and page == LANES and n_pages <= LANES
    pool_t = jnp.transpose(pool, (0, 2, 3, 4, 1)).reshape(n_pool, 2, HD, page)
    q4 = q_s.reshape(dec_batch, t_len, N_HEADS, HEAD_DIM) * SCALE
    qbd = jnp.einsum('bthx,hg->bhtgx', q4, np.eye(N_HEADS, dtype=np.float32)).reshape(dec_batch, n_col, HD)
    rev = tab.T.astype(F32)[:, past + t_len - 1 - np.arange(past + t_len)]
    by_t = jnp.stack([rev[:, t_len - 1 - t:t_len - 1 - t + past] for t in range(t_len)], axis=1)
    bias_pages = jnp.transpose(by_t.reshape(N_HEADS, t_len, n_pages, page), (2, 0, 1, 3)).reshape(n_pages, n_col, page)
    d_own = np.arange(t_len)[:, None] - np.arange(LANES)[None, :]
    ok_own = (d_own >= 0) & (np.arange(LANES)[None, :] < t_len)
    bias_own = jnp.where(ok_own[None], tab.T.astype(F32)[:, np.clip(d_own, 0, t_len)], NEG)
    bias_own = bias_own.reshape(n_col, LANES)
    dmask = jnp.asarray(np.arange(N_HEADS)[:, None] == (np.arange(HD) // HEAD_DIM)[None, :], F32)
    r_qh = np.arange(n_col)
    perm = jnp.asarray((r_qh % N_HEADS)[:, None] * t_len + (r_qh // N_HEADS)[:, None] == np.arange(n_col)[None, :], F32)
    pair = jnp.asarray((np.arange(LANES)[:, None] // ppb == np.arange(LANES)[None, :])
                       & (np.arange(LANES)[:, None] < n_pages), F32)
    new3 = kvb_s.reshape(dec_batch, t_len, 2 * HD)
    knt = jnp.pad(jnp.swapaxes(new3[:, :, :HD], 1, 2), ((0, 0), (0, 0), (0, LANES - t_len)))
    vnew = jnp.pad(new3[:, :, HD:], ((0, 0), (0, LANES - t_len), (0, 0)))

    pp = 4 if n_pages % 4 == 0 else 1
    page_spec = lambda u: pl.BlockSpec((None, 2, HD, page), lambda b, j, pt: (pt[b, j * pp + u], 0, 0, 0))
    stat = lambda rows: pl.BlockSpec((None, rows, LANES), lambda b, j, pt: (b, 0, 0))
    ksum, m_p, l_p, acc = pl.pallas_call(
        functools.partial(_moba_page_kernel, t_len=t_len, pp=pp),
        out_shape=(jax.ShapeDtypeStruct((dec_batch, HD, LANES), F32),
                   jax.ShapeDtypeStruct((dec_batch, n_col, LANES), F32),
                   jax.ShapeDtypeStruct((dec_batch, n_col, LANES), F32),
                   jax.ShapeDtypeStruct((dec_batch, n_pages * t_len, HD), F32)),
        grid_spec=pltpu.PrefetchScalarGridSpec(
            num_scalar_prefetch=1,
            grid=(dec_batch, n_pages // pp),
            in_specs=[page_spec(u) for u in range(pp)] + [
                pl.BlockSpec((None, n_col, HD), lambda b, j, pt: (b, 0, 0)),
                pl.BlockSpec((pp, n_col, page), lambda b, j, pt: (j, 0, 0)),
            ],
            out_specs=(stat(HD), stat(n_col), stat(n_col),
                       pl.BlockSpec((None, pp * t_len, HD), lambda b, j, pt: (b, j, 0))),
        ),
        compiler_params=_params(("parallel", "arbitrary")),
        name="moba_page",
    )(page_table, *([pool_t] * pp), qbd, bias_pages)

    acc = jnp.swapaxes(acc.reshape(dec_batch, n_pages, t_len, HD), 1, 2)
    per_b = lambda shape: pl.BlockSpec((None,) + shape, lambda b: (b,) + (0,) * len(shape))
    const = lambda shape: pl.BlockSpec(shape, lambda b: (0,) * len(shape))
    out = pl.pallas_call(
        functools.partial(_moba_combine_kernel, n_pages=n_pages, n_blocks=n_blocks, n_sel=n_sel, t_len=t_len),
        out_shape=jax.ShapeDtypeStruct((dec_batch, t_len, HD), F32),
        grid=(dec_batch,),
        in_specs=[per_b((HD, LANES)), per_b((n_col, LANES)), per_b((n_col, LANES)),
                  per_b((t_len, n_pages, HD)), per_b((n_col, HD)), per_b((HD, LANES)), per_b((LANES, HD)),
                  const((n_col, LANES)), const((LANES, LANES)), const((N_HEADS, HD)), const((n_col, n_col))],
        out_specs=per_b((t_len, HD)),
        compiler_params=_params(("parallel",)),
        name="moba_combine",
    )(ksum, m_p, l_p, acc, qbd, knt, vnew, bias_own, pair, dmask, perm)
    return out.reshape(dec_batch * t_len, HD)


def kernel(x_prompt, x_sample, cache_a_kv_g0, cache_a_kv_g1, cache_a_kv_g2, cache_b_kv_pool, page_table, state_ffn_conv, rel_bias, norm_attn, norm_ffn, w_qkv_a, q_norm_a, k_norm_a, w_o_a, norm_kv, w_kv_b, k_norm_b, w_q_b, q_norm_b, w_o_b, w_up, conv_w, conv_b, w_down):
    B, S, D = x_prompt.shape
    DB, T, _ = x_sample.shape
    depth = norm_attn.shape[0]
    n_self = w_qkv_a.shape[0]
    caches = (cache_a_kv_g0, cache_a_kv_g1, cache_a_kv_g2)
    past = page_table.shape[1] * cache_b_kv_pool.shape[1]
    assert D == HD

    tab = rel_bias[_rel_bucket(jnp.arange(max(S, past + T, DIL_GROUPS[-1][0] + 1)))]
    group_bias = [tab[dil * np.arange(win // dil + 1)] for (win, dil) in DIL_GROUPS]

    ones =jnp.ones((HD,), F32)
    tile_h = lambda gvec, n: jnp.tile(gvec, n * N_HEADS)

    def self_layer(x, l, prompt):
        hg = jnp.concatenate([tile_h(q_norm_a[l], N_GROUPS), tile_h(k_norm_a[l], N_GROUPS), jnp.tile(ones, N_GROUPS)])
        qkv = norm_matmul(x, norm_attn[l], w_qkv_a[l].astype(BF16), hg, 2 * N_GROUPS * HD)
        outs, lses, bufs = [], [], []
        seq = S if prompt else T
        qkv3 = qkv.reshape(-1, seq, qkv.shape[1])
        for gi, (win, dil) in enumerate(DIL_GROUPS):
            keep = min(win, seq)
            part = lambda which: qkv3[:, seq - keep:, (which * N_GROUPS + gi) * HD:(which * N_GROUPS + gi + 1) * HD
                                      ].reshape(-1, keep, N_HEADS, HEAD_DIM)
            new_kv = jnp.stack([part(1), part(2)], axis=2)
            if prompt:
                o, lse = dil_prompt(qkv, B, S, gi, group_bias[gi])
                bufs.append(new_kv)
            else:
                o, lse, buf = dil_sample(caches[gi][l], new_kv, part(0), DB, T, gi, group_bias[gi])
                bufs.append(buf)
            outs.append(o)
            lses.append(lse)
        return merge_proj(outs, lses, w_o_a[l].astype(BF16), x), bufs

    def ffn(x, l, prompt):
        wup, wdn = w_up[l].astype(BF16), w_down[l].astype(BF16)
        if prompt:
            tm = 256
            y, ut = ffn_prompt(x, S, norm_ffn[l], wup, conv_w[l], conv_b[l], wdn, tm=tm)
            cs = ut.reshape(B, S // tm, 8, ut.shape[1])[:, -1, 8 - (CONV_W - 1):]
        else:
            y, ua, ub = ffn_sample(x, T, state_ffn_conv[l], norm_ffn[l], wup, conv_w[l], conv_b[l], wdn)
            u = jnp.concatenate([ua, ub], axis=1)
            cs = u.reshape(DB, T, u.shape[1])[:, T - (CONV_W - 1):]
        return y, cs

    def trunk(x, prompt):
        nb_rows = B if prompt else DB
        a_new = [[] for _ in range(N_GROUPS)]
        conv_new = []
        kvb = None
        prep = None
        for l in range(depth):
            if l < n_self:
                x, bufs = self_layer(x, l, prompt)
                for gi in range(N_GROUPS):
                    a_new[gi].append(bufs[gi])
            else:
                if l == n_self:
                    hg = jnp.concatenate([tile_h(k_norm_b, 1), ones])
                    kvb = norm_matmul(x, norm_kv, w_kv_b.astype(BF16), hg, HD)
                    if prompt:
                        prep = kv_prep(kvb, B)
                lb = l - n_self
                q = norm_matmul(x, norm_attn[l], w_q_b[lb].astype(BF16), tile_h(q_norm_b[lb], 1), HD)
                if prompt:
                    o = moba_prompt(q, prep[1], prep[2], prep[0], tab, B, S)
                else:
                    o = moba_sample(q, kvb, cache_b_kv_pool, page_table, tab, DB, T)
                x = merge_proj([o], None, w_o_b[lb].astype(BF16), x)
            x, cs = ffn(x, l, prompt)
            conv_new.append(cs)
        seq = S if prompt else T
        if prompt:
            kv_new = jnp.transpose(prep[3].reshape(B, 2, N_HEADS, HEAD_DIM, S), (0, 4, 1, 2, 3))
        else:
            kv_new = kvb.reshape(nb_rows, seq, 2, N_HEADS, HEAD_DIM)
        return x.reshape(nb_rows, seq, D), [jnp.stack(a, 0) for a in a_new], kv_new, jnp.stack(conv_new, 0)

    y_p, a_p, kv_p, conv_p = trunk(x_prompt.reshape(B * S, D), True)
    y_s, a_s, kv_s, conv_s = trunk(x_sample.reshape(DB * T, D), False)
    return (y_p, y_s, a_p[0], a_p[1], a_p[2], a_s[0], a_s[1], a_s[2], kv_p, kv_s, conv_p, conv_s)
```

```python
import functools
import math

import jax
import jax.numpy as jnp
import numpy as np
from jax import lax
from jax.experimental import pallas as pl
from jax.experimental.pallas import tpu as pltpu

HEAD_DIM = 64
N_HEADS = 16
HD = N_HEADS * HEAD_DIM
DIL_GROUPS = ((128, 1), (512, 4), (2048, 16))
N_GROUPS = len(DIL_GROUPS)
N_BACK = 128
MOBA_BLOCK = 256
MOBA_TOPK = 3
NUM_BUCKETS = 32
REL_MAX_DIST = 4096
CONV_W = 3
EPS = 1e-6
SCALE = HEAD_DIM ** -0.5
NEG = -1e30
LOG2E = 1.4426950408889634
LANES = 128
MXU_TILE = 256
HEADS_PER_LANE_TILE = LANES // HEAD_DIM
VMEM_LIMIT = 56 * 1024 * 1024

F32 = jnp.float32
BF16 = jnp.bfloat16
HIGHEST = lax.Precision.HIGHEST


def _params(sem, vmem=VMEM_LIMIT):
    return pltpu.CompilerParams(dimension_semantics=sem, vmem_limit_bytes=vmem)


def _const_spec(shape):
    nd = len(shape)
    return pl.BlockSpec(shape, lambda *_: (0,) * nd, pipeline_mode=pl.Buffered(1))


def _rel_bucket(dist):
    n = jnp.maximum(dist, 0)
    exact = NUM_BUCKETS // 2
    nf = jnp.maximum(n, exact).astype(F32)
    large = exact + (jnp.log(nf / exact) / math.log(REL_MAX_DIST / exact) * (NUM_BUCKETS - exact)).astype(jnp.int32)
    return jnp.where(n < exact, n, jnp.minimum(large, NUM_BUCKETS - 1))


def _norm_matmul_kernel(x_ref, g_ref, w_ref, hg_ref, bd_ref, o_ref, h_scr, *, n_norm_tiles):
    j = pl.program_id(1)

    @pl.when(j == 0)
    def _():
        x = x_ref[...]
        r = lax.rsqrt(jnp.mean(x * x, axis=-1, keepdims=True) + EPS)
        h_scr[...] = (x * r * g_ref[...]).astype(BF16)

    y = jnp.dot(h_scr[...], w_ref[...], preferred_element_type=F32)

    @pl.when(j < n_norm_tiles)
    def _():
        y2 = (y * y).astype(BF16)
        ss = jnp.concatenate([jnp.dot(y2[:, c:c + MXU_TILE], bd_ref[...], preferred_element_type=F32)
                              for c in range(0, y.shape[1], MXU_TILE)], axis=1)
        o_ref[...] = y * lax.rsqrt(ss * (1.0 / HEAD_DIM) + EPS) * hg_ref[...]

    @pl.when(j >= n_norm_tiles)
    def _():
        o_ref[...] = y


def norm_matmul(x, g, w_bf16, head_gain, n_norm_cols, *, tn=512):
    M, K = x.shape
    N = w_bf16.shape[1]
    tm = min(1024, M)
    assert M % tm == 0 and N % tn == 0 and n_norm_cols % tn == 0 and tn % MXU_TILE == 0
    hid = np.arange(MXU_TILE) // HEAD_DIM
    bd = jnp.asarray(hid[:, None] == hid[None, :], BF16)
    return pl.pallas_call(
        functools.partial(_norm_matmul_kernel, n_norm_tiles=n_norm_cols // tn),
        out_shape=jax.ShapeDtypeStruct((M, N), F32),
        grid=(M // tm, N // tn),
        in_specs=[
            pl.BlockSpec((tm, K), lambda i, j: (i, 0)),
            pl.BlockSpec((1, K), lambda i, j: (0, 0)),
            pl.BlockSpec((K, tn), lambda i, j: (0, j)),
            pl.BlockSpec((1, tn), lambda i, j: (0, j)),
            pl.BlockSpec((MXU_TILE, MXU_TILE), lambda i, j: (0, 0)),
        ],
        out_specs=pl.BlockSpec((tm, tn), lambda i, j: (i, j)),
        scratch_shapes=[pltpu.VMEM((tm, K), BF16)],
        compiler_params=_params(("parallel", "arbitrary")),
        name="norm_matmul",
    )(x, g.reshape(1, K), w_bf16, head_gain.reshape(1, N), bd)


def _merge_proj_kernel(*refs, n_g):
    o_refs = refs[:n_g]
    lse_refs = refs[n_g:2 * n_g] if n_g > 1 else ()
    w_ref, x_ref, out_ref = refs[-3:]
    if n_g == 1:
        a = o_refs[0][...]
    else:
        ls = [r[...] for r in lse_refs]
        mx = functools.reduce(jnp.maximum, ls)
        ws = [jnp.exp(l - mx) for l in ls]
        den = functools.reduce(lambda p, q: p + q, ws)
        num = functools.reduce(lambda p, q: p + q, [w * r[...] for w, r in zip(ws, o_refs)])
        a = num / den
    out_ref[...] = x_ref[...] + jnp.dot(a.astype(BF16), w_ref[...], preferred_element_type=F32)


def merge_proj(os_, lses, w_bf16, x):
    M, D = x.shape
    n_g = len(os_)
    tm = min(256, M)
    assert M % tm == 0
    row = pl.BlockSpec((tm, D), lambda i: (i, 0))
    ins = list(os_) + (list(lses) if n_g > 1 else [])
    return pl.pallas_call(
        functools.partial(_merge_proj_kernel, n_g=n_g),
        out_shape=jax.ShapeDtypeStruct((M, D), F32),
        grid=(M // tm,),
        in_specs=[row] * len(ins) + [_const_spec(w_bf16.shape), row],
        out_specs=row,
        compiler_params=_params(("parallel",)),
        name="merge_proj",
    )(*ins, w_bf16, x)


def _silu_gate(ca, cb):
    return (ca / (1.0 + jnp.exp(-ca))) * cb


def _ffn_prompt_kernel(x_ref, xp_ref, g_ref, wup_ref, cw_ref, cb_ref, wdn_ref, y_ref, ut_ref,
                       ua_scr, ub_scr, gate_scr, *, tm, ck, d_ff, tiles_per_seq):
    i = pl.program_id(0)
    x = x_ref[...]
    keep = jnp.where(i % tiles_per_seq == 0, 0.0, 1.0)
    xc = jnp.concatenate([xp_ref[...] * keep, x], axis=0)
    r = lax.rsqrt(jnp.mean(xc * xc, axis=-1, keepdims=True) + EPS)
    h = (xc * r * g_ref[...]).astype(BF16)
    for c in range(d_ff // ck):
        a0, b0 = c * ck, d_ff + c * ck
        ua_scr[...] = jnp.dot(h, wup_ref[:, a0:a0 + ck], preferred_element_type=F32)
        ub_scr[...] = jnp.dot(h, wup_ref[:, b0:b0 + ck], preferred_element_type=F32)
        ut_ref[:, a0:a0 + ck] = ua_scr[tm:tm + 8, :]
        ut_ref[:, b0:b0 + ck] = ub_scr[tm:tm + 8, :]
        ca = cb_ref[:, a0:a0 + ck]
        cb = cb_ref[:, b0:b0 + ck]
        for j in range(CONV_W):
            ca = ca + cw_ref[j:j + 1, a0:a0 + ck] * ua_scr[6 + j:6 + j + tm, :]
            cb = cb + cw_ref[j:j + 1, b0:b0 + ck] * ub_scr[6 + j:6 + j + tm, :]
        gate_scr[:, a0:a0 + ck] = _silu_gate(ca, cb).astype(BF16)
    y_ref[...] = x + jnp.dot(gate_scr[...], wdn_ref[...], preferred_element_type=F32)


def ffn_prompt(x, seq_len, g, wup_bf16, cw, cb, wdn_bf16, *, tm=256, ck=256):
    M, D = x.shape
    d_ff = wdn_bf16.shape[0]
    assert seq_len % tm == 0 and M % seq_len == 0 and d_ff % ck == 0 and CONV_W - 1 <= 8
    n_tiles = M // tm
    y, ut = pl.pallas_call(
        functools.partial(_ffn_prompt_kernel, tm=tm, ck=ck, d_ff=d_ff, tiles_per_seq=seq_len // tm),
        out_shape=(jax.ShapeDtypeStruct((M, D), F32), jax.ShapeDtypeStruct((n_tiles * 8, 2 * d_ff), F32)),
        grid=(n_tiles,),
        in_specs=[
            pl.BlockSpec((tm, D), lambda i: (i, 0)),
            pl.BlockSpec((8, D), lambda i: (jnp.maximum(i * (tm // 8) - 1, 0), 0)),
            _const_spec((1, D)),
            _const_spec(wup_bf16.shape),
            _const_spec(cw.shape),
            _const_spec((1, 2 * d_ff)),
            _const_spec(wdn_bf16.shape),
        ],
        out_specs=(pl.BlockSpec((tm, D), lambda i: (i, 0)), pl.BlockSpec((8, 2 * d_ff), lambda i: (i, 0))),
        scratch_shapes=[pltpu.VMEM((tm + 8, ck), F32), pltpu.VMEM((tm + 8, ck), F32), pltpu.VMEM((tm, d_ff), BF16)],
        compiler_params=_params(("parallel",)),
        name="ffn_prompt",
    )(x, x, g.reshape(1, D), wup_bf16, cw, cb.reshape(1, 2 * d_ff), wdn_bf16)
    return y, ut


def _ffn_sample_kernel(x_ref, g_ref, wa_ref, wb_ref, cwa_ref, cwb_ref, cba_ref, cbb_ref,
                       e1a_ref, e1b_ref, e0a_ref, e0b_ref, wdn_ref, y_ref, ua_ref, ub_ref,
                       h_scr, ua_scr, ub_scr, *, m, t_len):
    c = pl.program_id(0)

    @pl.when(c == 0)
    def _():
        x = x_ref[...]
        r = lax.rsqrt(jnp.mean(x * x, axis=-1, keepdims=True) + EPS)
        h_scr[...] = (x * r * g_ref[...]).astype(BF16)
        y_ref[...] = x
        ua_scr[0:8, :] = jnp.zeros((8, ua_scr.shape[1]), F32)
        ub_scr[0:8, :] = jnp.zeros((8, ub_scr.shape[1]), F32)

    h = h_scr[...]
    ua = jnp.dot(h, wa_ref[...], preferred_element_type=F32)
    ub = jnp.dot(h, wb_ref[...], preferred_element_type=F32)
    ua_ref[...] = ua
    ub_ref[...] = ub
    ua_scr[8:8 + m, :] = ua
    ub_scr[8:8 + m, :] = ub
    t = lax.broadcasted_iota(jnp.int32, ua.shape, 0) % t_len

    def conv(u, u_scr, cw_ref, cb_ref, e1_ref, e0_ref):
        um1 = jnp.where(t >= 1, u_scr[7:7 + m, :], e1_ref[...])
        um2 = jnp.where(t >= 2, u_scr[6:6 + m, :], jnp.where(t == 1, e1_ref[...], e0_ref[...]))
        return cb_ref[...] + cw_ref[0:1, :] * um2 + cw_ref[1:2, :] * um1 + cw_ref[2:3, :] * u

    ca = conv(ua, ua_scr, cwa_ref, cba_ref, e1a_ref, e0a_ref)
    cb = conv(ub, ub_scr, cwb_ref, cbb_ref, e1b_ref, e0b_ref)
    gt = _silu_gate(ca, cb).astype(BF16)
    y_ref[...] += jnp.dot(gt, wdn_ref[...], preferred_element_type=F32)


def ffn_sample(x, t_len, prev, g, wup_bf16, cw, cb, wdn_bf16, *, ck=256):
    M, D = x.shape
    d_ff = wdn_bf16.shape[0]
    assert d_ff % ck == 0 and CONV_W == 3 and t_len >= 2
    nc = d_ff // ck
    e1 = jnp.repeat(prev[:, 1], t_len, axis=0)
    e0 = jnp.repeat(prev[:, 0], t_len, axis=0)
    cb2 = cb.reshape(1, 2 * d_ff)
    a_col = lambda c: (0, c)
    b_col = lambda c: (0, nc + c)
    return pl.pallas_call(
        functools.partial(_ffn_sample_kernel, m=M, t_len=t_len),
        out_shape=(jax.ShapeDtypeStruct((M, D), F32), jax.ShapeDtypeStruct((M, d_ff), F32),
                   jax.ShapeDtypeStruct((M, d_ff), F32)),
        grid=(nc,),
        in_specs=[
            pl.BlockSpec((M, D), lambda c: (0, 0)),
            pl.BlockSpec((1, D), lambda c: (0, 0)),
            pl.BlockSpec((D, ck), a_col), pl.BlockSpec((D, ck), b_col),
            pl.BlockSpec((CONV_W, ck), a_col), pl.BlockSpec((CONV_W, ck), b_col),
            pl.BlockSpec((1, ck), a_col), pl.BlockSpec((1, ck), b_col),
            pl.BlockSpec((M, ck), a_col), pl.BlockSpec((M, ck), b_col),
            pl.BlockSpec((M, ck), a_col), pl.BlockSpec((M, ck), b_col),
            pl.BlockSpec((ck, D), lambda c: (c, 0)),
        ],
        out_specs=(pl.BlockSpec((M, D), lambda c: (0, 0)),
                   pl.BlockSpec((M, ck), a_col), pl.BlockSpec((M, ck), a_col)),
        scratch_shapes=[pltpu.VMEM((M, D), BF16), pltpu.VMEM((M + 8, ck), F32), pltpu.VMEM((M + 8, ck), F32)],
        compiler_params=_params(("arbitrary",)),
        name="ffn_sample",
    )(x, g.reshape(1, D), wup_bf16, wup_bf16, cw, cw, cb2, cb2, e1, e1, e0, e0, wdn_bf16)


def _toeplitz(seg_row, rows, lo, width):
    x = jnp.broadcast_to(seg_row, (rows, seg_row.shape[1]))
    return pltpu.roll(x, 0, 1, stride=1, stride_axis=0)[:, lo:lo + width]


DIL_UNIT = 4


def _dil_prompt_kernel(q_ref, kc_ref, kp_ref, vc_ref, vp_ref, seg_ref, o_ref, lse_ref, bias_scr, kcat_scr, vcat_scr,
                       *, d, n_jc):
    b = pl.program_id(1)
    i = pl.program_id(2)
    halo = N_BACK * d
    tq, tk = N_BACK, 2 * N_BACK

    @pl.when((b == 0) & (i == 0))
    def _():
        for h in range(HEADS_PER_LANE_TILE):
            bias_scr[h] = _toeplitz(seg_ref[h], tq, tq, tk)

    kcat_scr[0:halo, :] = kp_ref[...]
    kcat_scr[halo:, :] = kc_ref[...]
    vcat_scr[0:halo, :] = vp_ref[...]
    vcat_scr[halo:, :] = vc_ref[...]

    def rows(ref, start, n):
        return ref[pl.ds(start, n), :] if d == 1 else ref[pl.ds(start, n, stride=d), :]

    lane = lax.broadcasted_iota(jnp.int32, (tq, LANES), 1)
    col = lax.broadcasted_iota(jnp.int32, (tq, tk), 1)
    first = lane < HEAD_DIM
    heads = range(HEADS_PER_LANE_TILE)

    def unit(u, carry):
        starts, deads, logits, vals = [], [], [], []
        for s in range(DIL_UNIT):
            pidx = u * DIL_UNIT + s
            r, jc = pidx // n_jc, pidx % n_jc
            start = r + d * (jc * N_BACK)
            starts.append(start)
            deads.append(jnp.where((i == 0) & (jc == 0), N_BACK, 0))
            q = rows(q_ref, start, tq) * SCALE
            k = rows(kcat_scr, start, tk).astype(BF16)
            vals.append(rows(vcat_scr, start, tk).astype(BF16))
            for h in heads:
                hm = (lane >= h * HEAD_DIM) & (lane < (h + 1) * HEAD_DIM)
                qh = jnp.where(hm, q, 0.0).astype(BF16)
                logits.append(lax.dot_general(qh, k, (((1,), (1,)), ((), ())), preferred_element_type=F32))
        probs, stats = [], []
        for n, sc in enumerate(logits):
            sc = jnp.where(col < deads[n // HEADS_PER_LANE_TILE], NEG, sc + bias_scr[n % HEADS_PER_LANE_TILE])
            m = jnp.max(sc, axis=-1, keepdims=True)
            e = jnp.exp(sc - m)
            probs.append(e.astype(BF16))
            stats.append((m, jnp.sum(e, axis=-1, keepdims=True)))
        for s in range(DIL_UNIT):
            outs, lses = [], []
            for h in heads:
                n = s * HEADS_PER_LANE_TILE + h
                m, l = stats[n]
                outs.append(jnp.dot(probs[n], vals[s], preferred_element_type=F32) / l)
                lses.append(m + jnp.log(l))
            o_val = jnp.where(first, outs[0], outs[1])
            lse_val = jnp.where(first, lses[0], lses[1])
            if d == 1:
                o_ref[pl.ds(starts[s], tq), :] = o_val
                lse_ref[pl.ds(starts[s], tq), :] = lse_val
            else:
                o_ref[pl.ds(starts[s], tq, stride=d), :] = o_val
                lse_ref[pl.ds(starts[s], tq, stride=d), :] = lse_val
        return carry

    lax.fori_loop(0, d * n_jc // DIL_UNIT, unit, 0)


def dil_prompt(qkv, batch, seq_len, gi, bias_k):
    win, d = DIL_GROUPS[gi]
    assert win // d == N_BACK and seq_len % (d * N_BACK) == 0
    halo = N_BACK * d
    blk = max(halo, min(2048, seq_len))
    n_jc = blk // halo
    assert seq_len % blk == 0 and blk % halo == 0 and (d * n_jc) % DIL_UNIT == 0
    hp_tiles = HD // LANES
    qkv3 = qkv.reshape(batch, seq_len, qkv.shape[1])
    width = 3 * N_BACK
    step = 2 * N_BACK - np.arange(width)
    seg = jnp.where(((step >= 0) & (step <= N_BACK))[:, None], bias_k[np.clip(step, 0, N_BACK)].astype(F32), NEG)
    seg = jnp.transpose(seg)[:, None, :]

    def col(which):
        return (which * N_GROUPS + gi) * hp_tiles

    cur = lambda which: pl.BlockSpec((None, blk, LANES), lambda hp, b, i: (b, i, col(which) + hp))
    prev = lambda which: pl.BlockSpec((None, halo, LANES),
                                      lambda hp, b, i: (b, jnp.maximum(i * n_jc - 1, 0), col(which) + hp))
    out_spec = pl.BlockSpec((None, blk, LANES), lambda hp, b, i: (b, i, hp))
    o, lse = pl.pallas_call(
        functools.partial(_dil_prompt_kernel, d=d, n_jc=n_jc),
        out_shape=(jax.ShapeDtypeStruct((batch, seq_len, HD), F32),) * 2,
        grid=(hp_tiles, batch, seq_len // blk),
        in_specs=[cur(0), cur(1), prev(1), cur(2), prev(2),
                  pl.BlockSpec((HEADS_PER_LANE_TILE, 1, width), lambda hp, b, i: (hp, 0, 0))],
        out_specs=(out_spec, out_spec),
        scratch_shapes=[pltpu.VMEM((HEADS_PER_LANE_TILE, N_BACK, 2 * N_BACK), F32),
                        pltpu.VMEM((halo + blk, LANES), F32), pltpu.VMEM((halo + blk, LANES), F32)],
        compiler_params=_params(("parallel", "arbitrary", "arbitrary")),
        name=f"dil_prompt_g{gi}",
    )(qkv3, qkv3, qkv3, qkv3, qkv3, seg)
    return o.reshape(batch * seq_len, HD), lse.reshape(batch * seq_len, HD)


def _dil_sample_kernel(ct_ref, q_ref, newt_ref, bias_c_ref, bias_n_ref, ot_ref, o_ref, lse_ref, *, hb, t_len, L):
    lane = lax.broadcasted_iota(jnp.int32, (HEAD_DIM, LANES), 1)
    nt = (((1,), (1,)), ((), ()))
    for hh in range(hb):
        q = (q_ref[hh] * SCALE).astype(BF16)
        kt, vt = ct_ref[0, hh], ct_ref[1, hh]
        knt, vnt = newt_ref[0, hh], newt_ref[1, hh]
        s_c = jnp.dot(q, kt.astype(BF16), preferred_element_type=F32) + bias_c_ref[hh]
        s_n = jnp.dot(q, knt.astype(BF16), preferred_element_type=F32) + bias_n_ref[hh]
        m = jnp.maximum(jnp.max(s_c, axis=1, keepdims=True), jnp.max(s_n, axis=1, keepdims=True))
        e_c = jnp.exp(s_c - m)
        e_n = jnp.exp(s_n - m)
        l = jnp.sum(e_c, axis=1, keepdims=True) + jnp.sum(e_n, axis=1, keepdims=True)
        pv = (lax.dot_general(e_c.astype(BF16), vt.astype(BF16), nt, preferred_element_type=F32)
              + lax.dot_general(e_n.astype(BF16), vnt.astype(BF16), nt, preferred_element_type=F32))
        o_ref[hh] = pv / l
        lse_ref[hh] = jnp.broadcast_to(m + jnp.log(l), (t_len, HEAD_DIM))
        for kv, (old, new) in enumerate(((kt, knt), (vt, vnt))):
            moved = pltpu.roll(old, L - t_len, 1)
            if L > LANES:
                ot_ref[kv, hh, :, :L - LANES] = moved[:, :L - LANES]
            ot_ref[kv, hh, :, L - LANES:] = jnp.where(lane >= LANES - t_len, new, moved[:, L - LANES:])


def dil_sample(cache, new_kv, q_s, dec_batch, t_len, gi, bias_k):
    win, d = DIL_GROUPS[gi]
    L = cache.shape[1]
    assert L == win and win // d == N_BACK and L % LANES == 0 and t_len <= LANES
    hb = min(N_HEADS, max(1, (4 << 20) // (2 * HEAD_DIM * L * 4)))
    assert N_HEADS % hb == 0
    ct = jnp.transpose(cache, (0, 2, 3, 4, 1))
    newt = jnp.pad(jnp.transpose(new_kv, (0, 2, 3, 4, 1)), ((0, 0),) * 4 + ((LANES - t_len, 0),))
    qh = jnp.transpose(q_s, (0, 2, 1, 3))
    tq = np.arange(t_len)[:, None]
    back = L + tq - np.arange(L)[None, :]
    ok_c = (back % d == 0) & (back // d <= N_BACK)
    bias_kt = bias_k.T.astype(F32)
    bias_c = jnp.where(ok_c[None], bias_kt[:, np.clip(back // d, 0, N_BACK)], NEG)
    t2 = np.arange(LANES)[None, :] - (LANES - t_len)
    ok_n = (t2 >= 0) & (t2 <= tq) & ((tq - t2) % d == 0)
    bias_n = jnp.where(ok_n[None], bias_kt[:, np.clip((tq - t2) // d, 0, N_BACK)], NEG)
    cache_spec = pl.BlockSpec((None, 2, hb, HEAD_DIM, L), lambda b, j: (b, 0, j, 0, 0))
    row_spec = pl.BlockSpec((None, hb, t_len, HEAD_DIM), lambda b, j: (b, j, 0, 0))
    ot, o, lse = pl.pallas_call(
        functools.partial(_dil_sample_kernel, hb=hb, t_len=t_len, L=L),
        out_shape=(jax.ShapeDtypeStruct(ct.shape, F32),
                   jax.ShapeDtypeStruct(qh.shape, F32), jax.ShapeDtypeStruct(qh.shape, F32)),
        grid=(dec_batch, N_HEADS // hb),
        in_specs=[
            cache_spec,
            row_spec,
            pl.BlockSpec((None, 2, hb, HEAD_DIM, LANES), lambda b, j: (b, 0, j, 0, 0)),
            pl.BlockSpec((hb, t_len, L), lambda b, j: (j, 0, 0)),
            pl.BlockSpec((hb, t_len, LANES), lambda b, j: (j, 0, 0)),
        ],
        out_specs=(cache_spec, row_spec, row_spec),
        compiler_params=_params(("parallel", "parallel")),
        name=f"dil_sample_g{gi}",
    )(ct, qh, newt, bias_c, bias_n)
    to_rows = lambda a: jnp.transpose(a, (0, 2, 1, 3)).reshape(dec_batch * t_len, HD)
    return to_rows(o), to_rows(lse), jnp.transpose(ot, (0, 4, 1, 2, 3))


def _kv_prep_kernel(kv_ref, mean_ref, k16_ref, vt16_ref, kvt_ref):
    k = kv_ref[:, :HD]
    mean_ref[...] = jnp.broadcast_to(jnp.sum(k, axis=0, keepdims=True) * (1.0 / MOBA_BLOCK), mean_ref.shape)
    for hp in range(HD // LANES):
        k16_ref[hp] = k[:, hp * LANES:(hp + 1) * LANES].astype(BF16)
    kvt = kv_ref[...].T
    kvt_ref[...] = kvt
    vt16_ref[...] = kvt[HD:].astype(BF16)


def kv_prep(kvb, batch):
    rows = kvb.shape[0]
    n_blocks = rows // MOBA_BLOCK
    per_seq = n_blocks // batch
    hp_tiles = HD // LANES
    means, k16, vt16, kvt = pl.pallas_call(
        _kv_prep_kernel,
        out_shape=(jax.ShapeDtypeStruct((n_blocks, 8, HD), F32),
                   jax.ShapeDtypeStruct((hp_tiles, rows, LANES), BF16),
                   jax.ShapeDtypeStruct((HD, rows), BF16),
                   jax.ShapeDtypeStruct((batch, 2 * HD, rows // batch), F32)),
        grid=(n_blocks,),
        in_specs=[pl.BlockSpec((MOBA_BLOCK, 2 * HD), lambda i: (i, 0))],
        out_specs=(pl.BlockSpec((None, 8, HD), lambda i: (i, 0, 0)),
                   pl.BlockSpec((hp_tiles, MOBA_BLOCK, LANES), lambda i: (0, i, 0)),
                   pl.BlockSpec((HD, MOBA_BLOCK), lambda i: (0, i)),
                   pl.BlockSpec((None, 2 * HD, MOBA_BLOCK), lambda i: (i // per_seq, 0, i % per_seq))),
        compiler_params=_params(("parallel",)),
        name="kv_prep",
    )(kvb)
    return means[:, 0], k16, vt16, kvt


def _top_blocks(sc, blk, n_valid, n_sel, axis):
    sel = jnp.zeros(sc.shape, F32)
    big = float(sc.shape[axis])
    for it in range(n_sel):
        mx = jnp.max(sc, axis=axis, keepdims=True)
        idx = jnp.min(jnp.where(sc == mx, blk, big), axis=axis, keepdims=True)
        hit = blk == idx
        sel = jnp.maximum(sel, jnp.where(hit, jnp.where(it < n_valid, 1.0, 0.0), 0.0))
        sc = jnp.where(hit, -jnp.inf, sc)
    return sel


def _moba_prompt_kernel(q_ref, k_ref, vt_ref, km_ref, seg_ref, o_ref,
                        m_scr, l_scr, acc_scr, selb_scr, bias_scr, pa_scr, pb_scr, sha_scr, shb_scr, ala_scr, alb_scr,
                        *, nb, n_sel, kb):
    b = pl.program_id(1)
    i = pl.program_id(2)
    tb = MOBA_BLOCK

    def head_mask(shape, axis, h):
        idx = lax.broadcasted_iota(jnp.int32, shape, axis)
        return (idx >= h * HEAD_DIM) & (idx < (h + 1) * HEAD_DIM)

    @pl.when((b == 0) & (i == 0))
    def _():
        row = lax.broadcasted_iota(jnp.int32, (tb, tb), 0)
        colm = lax.broadcasted_iota(jnp.int32, (tb, tb), 1)
        for h in range(HEADS_PER_LANE_TILE):
            def fill(delta, carry, h=h):
                bias_scr[h, delta] = _toeplitz(seg_ref[h, pl.ds(delta, 1), :], tb, tb, tb) * LOG2E
                return carry
            lax.fori_loop(0, nb, fill, 0)
            bias_scr[h, 0] = jnp.where(colm >= row, bias_scr[h, 0], NEG)

    heads = range(HEADS_PER_LANE_TILE)
    m_scr[...] = jnp.full_like(m_scr, NEG)
    l_scr[...] = jnp.zeros_like(l_scr)
    acc_scr[...] = jnp.zeros_like(acc_scr)
    q_raw = q_ref[...]
    km = km_ref[...]
    blk = lax.broadcasted_iota(jnp.int32, (nb, tb), 0).astype(F32)
    i_f = i.astype(F32)
    for h in heads:
        kmh = jnp.where(head_mask(km.shape, 1, h), km, 0.0)
        sc = lax.dot_general(kmh, q_raw, (((1,), (1,)), ((), ())), precision=HIGHEST, preferred_element_type=F32)
        sc = jnp.where(blk < i_f, sc, -jnp.inf)
        sel = _top_blocks(sc, blk, i, n_sel, axis=0) + jnp.where(blk == i_f, 1.0, 0.0)
        selb_scr[h] = jnp.where(sel > 0.5, 0.0, NEG)

    q = q_raw * (SCALE * LOG2E)
    qhs = [jnp.where(head_mask(q.shape, 1, h), q, 0.0).astype(BF16) for h in heads]

    def logits_pass(kt, p_scr, sh_scr, al_scr):
        k0 = pl.multiple_of(kt * (kb * tb), kb * tb)
        k = k_ref[pl.ds(k0, kb * tb), :]
        raw = [[lax.dot_general(k[c * tb:(c + 1) * tb], qhs[h], (((1,), (1,)), ((), ())),
                                preferred_element_type=F32) for c in range(kb)] for h in heads]
        for h in heads:
            sbs, cms = [], []
            for c in range(kb):
                jj = kt * kb + c
                p = raw[h][c] + bias_scr[h, jnp.maximum(i - jj, 0)]
                p_scr[h, c] = p
                sb = selb_scr[h, pl.ds(jj, 1), :]
                sbs.append(sb)
                cms.append(jnp.max(p, axis=0, keepdims=True) + sb)
            m_old = m_scr[h]
            m_new = jnp.maximum(m_old, functools.reduce(jnp.maximum, cms))
            for c in range(kb):
                sh_scr[h, c] = m_new - sbs[c]
            al_scr[h] = jnp.exp2(m_old - m_new)
            m_scr[h] = m_new

    def value_pass(kt, p_scr, sh_scr, al_scr):
        k0 = pl.multiple_of(kt * (kb * tb), kb * tb)
        vt = vt_ref[:, pl.ds(k0, kb * tb)]
        for h in heads:
            alpha = al_scr[h]
            acc = alpha * acc_scr[h]
            l_new = alpha * l_scr[h]
            for c in range(kb):
                e = jnp.exp2(p_scr[h, c] - sh_scr[h, c])
                l_new = l_new + jnp.sum(e, axis=0, keepdims=True)
                acc = acc + jnp.dot(vt[h * HEAD_DIM:(h + 1) * HEAD_DIM, c * tb:(c + 1) * tb], e.astype(BF16),
                                    preferred_element_type=F32)
            l_scr[h] = l_new
            acc_scr[h] = acc

    own_tile = i // kb
    buf_a = (pa_scr, sha_scr, ala_scr)
    buf_b = (pb_scr, shb_scr, alb_scr)
    seq_tile = lambda n: jnp.where(n == 0, own_tile, n - 1)
    logits_pass(own_tile, *buf_a)

    def pair(j, carry):
        logits_pass(seq_tile(2 * j + 1), *buf_b)
        value_pass(seq_tile(2 * j), *buf_a)
        logits_pass(seq_tile(2 * j + 2), *buf_a)
        value_pass(seq_tile(2 * j + 1), *buf_b)
        return carry

    n_past = own_tile
    lax.fori_loop(0, n_past // 2, pair, 0)
    last_even = seq_tile(n_past - n_past % 2)

    @pl.when(n_past % 2 == 1)
    def _():
        logits_pass(seq_tile(n_past), *buf_b)
        value_pass(last_even, *buf_a)
        value_pass(seq_tile(n_past), *buf_b)

    @pl.when(n_past % 2 == 0)
    def _():
        value_pass(last_even, *buf_a)

    o_t = jnp.concatenate([acc_scr[h] / l_scr[h] for h in heads], axis=0)
    o_ref[...] = o_t.T


def moba_prompt(q, k16, vt16, k_means, tab, batch, seq_len):
    tb = MOBA_BLOCK
    kb = 4
    assert seq_len % (kb * tb) == 0
    nb = seq_len // tb
    n_sel = min(MOBA_TOPK, nb)
    hp_tiles = HD // LANES
    dist = np.arange(nb)[:, None] * tb + np.arange(2 * tb)[None, :] - tb
    seg = tab.T.astype(F32)[:, np.clip(dist, 0, tab.shape[0] - 1)]
    km3 = k_means.reshape(batch, nb, HD)
    return pl.pallas_call(
        functools.partial(_moba_prompt_kernel, nb=nb, n_sel=n_sel, kb=kb),
        out_shape=jax.ShapeDtypeStruct(q.shape, F32),
        grid=(hp_tiles, batch, nb),
        in_specs=[
            pl.BlockSpec((tb, LANES), lambda hp, b, i: (b * nb + i, hp)),
            pl.BlockSpec((None, seq_len, LANES), lambda hp, b, i: (hp, b, 0)),
            pl.BlockSpec((LANES, seq_len), lambda hp, b, i: (hp, b)),
            pl.BlockSpec((None, nb, LANES), lambda hp, b, i: (b, 0, hp)),
            pl.BlockSpec((HEADS_PER_LANE_TILE, nb, 2 * tb), lambda hp, b, i: (hp, 0, 0)),
        ],
        out_specs=pl.BlockSpec((tb, LANES), lambda hp, b, i: (b * nb + i, hp)),
        scratch_shapes=[pltpu.VMEM((HEADS_PER_LANE_TILE, 1, tb), F32),
                        pltpu.VMEM((HEADS_PER_LANE_TILE, 1, tb), F32),
                        pltpu.VMEM((HEADS_PER_LANE_TILE, HEAD_DIM, tb), F32),
                        pltpu.VMEM((HEADS_PER_LANE_TILE, nb, tb), F32),
                        pltpu.VMEM((HEADS_PER_LANE_TILE, nb, tb, tb), F32),
                        pltpu.VMEM((HEADS_PER_LANE_TILE, kb, tb, tb), F32),
                        pltpu.VMEM((HEADS_PER_LANE_TILE, kb, tb, tb), F32),
                        pltpu.VMEM((HEADS_PER_LANE_TILE, kb, 1, tb), F32),
                        pltpu.VMEM((HEADS_PER_LANE_TILE, kb, 1, tb), F32),
                        pltpu.VMEM((HEADS_PER_LANE_TILE, 1, tb), F32),
                        pltpu.VMEM((HEADS_PER_LANE_TILE, 1, tb), F32)],
        compiler_params=_params(("parallel", "arbitrary", "arbitrary")),
        name="moba_prompt",
    )(q, k16, vt16, km3, seg)


def _moba_page_kernel(pt_ref, *rest, t_len, pp):
    del pt_ref
    pool_refs, (qbd_ref, bias_ref, ksum_ref, m_ref, l_ref, acc_ref) = rest[:pp], rest[pp:]
    j = pl.program_id(1)
    qbd = qbd_ref[...].astype(BF16)
    hpg = MXU_TILE // HEAD_DIM
    n_grp = N_HEADS // hpg
    rpg = hpg * t_len
    nt = (((1,), (1,)), ((), ()))
    kts = [ref[0] for ref in pool_refs]
    logits = []
    for kt in kts:
        k16 = kt.astype(BF16)
        logits.append(jnp.concatenate(
            [jnp.dot(qbd[g * rpg:(g + 1) * rpg, g * MXU_TILE:(g + 1) * MXU_TILE], k16[g * MXU_TILE:(g + 1) * MXU_TILE],
                     preferred_element_type=F32) for g in range(n_grp)], axis=0))
    ms, ls, es = [], [], []
    for u, s in enumerate(logits):
        s = s + bias_ref[u]
        m = jnp.max(s, axis=1, keepdims=True)
        e = jnp.exp(s - m)
        ms.append(m)
        ls.append(jnp.sum(e, axis=1, keepdims=True))
        es.append(e.astype(BF16))
    lane = lax.broadcasted_iota(jnp.int32, (t_len, MXU_TILE), 1)
    for u in range(pp):
        v16 = pool_refs[u][1].astype(BF16)
        outs = []
        for g in range(n_grp):
            pv = lax.dot_general(es[u][g * rpg:(g + 1) * rpg], v16[g * MXU_TILE:(g + 1) * MXU_TILE], nt,
                                 preferred_element_type=F32)
            out = pv[0:t_len]
            for hl in range(1, hpg):
                out = jnp.where(lane >= hl * HEAD_DIM, pv[hl * t_len:(hl + 1) * t_len], out)
            outs.append(out)
        acc_ref[u * t_len:(u + 1) * t_len, :] = jnp.concatenate(outs, axis=1)

    @pl.when(j == 0)
    def _():
        ksum_ref[...] = jnp.zeros_like(ksum_ref)
        m_ref[...] = jnp.zeros_like(m_ref)
        l_ref[...] = jnp.zeros_like(l_ref)

    def put(ref, cols):
        lane = lax.broadcasted_iota(jnp.int32, ref.shape, 1)
        val = ref[...]
        for u, col in enumerate(cols):
            val = jnp.where(lane == j * pp + u, col, val)
        ref[...] = val

    put(ksum_ref, [jnp.sum(kt, axis=1, keepdims=True) for kt in kts])
    put(m_ref, ms)
    put(l_ref, ls)


def _moba_combine_kernel(ks_ref, m_ref, l_ref, acc_ref, qbd_ref, knt_ref, vnew_ref, bias_ref, pair_ref, dmask_ref,
                         perm_ref, o_ref, *, n_pages, n_blocks, n_sel, t_len):
    qbd = qbd_ref[...]
    pair = pair_ref[...]
    sc_pages = jnp.dot(qbd, ks_ref[...], precision=HIGHEST, preferred_element_type=F32)
    sc = jnp.dot(sc_pages, pair, precision=HIGHEST, preferred_element_type=F32) * (1.0 / MOBA_BLOCK)
    lane = lax.broadcasted_iota(jnp.int32, sc.shape, 1).astype(F32)
    sc = jnp.where(lane < n_blocks, sc, -jnp.inf)
    sel = _top_blocks(sc, lane, n_sel, n_sel, axis=1)
    picked = lax.dot_general(sel, pair, (((1,), (1,)), ((), ())), preferred_element_type=F32) > 0.5
    m_p = m_ref[...]
    s_own = jnp.dot(qbd.astype(BF16), knt_ref[...].astype(BF16), preferred_element_type=F32) + bias_ref[...]
    m_tot = jnp.maximum(jnp.max(jnp.where(picked, m_p, NEG), axis=1, keepdims=True),
                        jnp.max(s_own, axis=1, keepdims=True))
    w = jnp.where(picked, jnp.exp(m_p - m_tot), 0.0)
    e_own = jnp.exp(s_own - m_tot)
    l_tot = jnp.sum(w * l_ref[...], axis=1, keepdims=True) + jnp.sum(e_own, axis=1, keepdims=True)
    perm = perm_ref[...]
    to_query_major = lambda a: jnp.dot(perm, a, precision=HIGHEST, preferred_element_type=F32)
    w = to_query_major(w)
    e_own = to_query_major(e_own)
    l_tot = to_query_major(jnp.broadcast_to(l_tot, w.shape))[:, 0:1]
    dmask = dmask_ref[...]
    vnew = vnew_ref[...].astype(BF16)
    rows = []
    for t in range(t_len):
        hs = slice(t * N_HEADS, (t + 1) * N_HEADS)
        a = jnp.dot(w[hs, :n_pages], acc_ref[t], precision=HIGHEST, preferred_element_type=F32)
        a = a + jnp.dot(e_own[hs].astype(BF16), vnew, preferred_element_type=F32)
        num = jnp.sum(a * dmask, axis=0, keepdims=True)
        den = jnp.sum(l_tot[hs] * dmask, axis=0, keepdims=True)
        rows.append(num / den)
    o_ref[...] = jnp.concatenate(rows, axis=0)


def moba_sample(q_s, kvb_s, pool, page_table, tab, dec_batch, t_len):
    n_pool, page = pool.shape[0], pool.shape[1]
    n_pages = page_table.shape[1]
    past = n_pages * page
    assert MOBA_BLOCK % page == 0 and past % MOBA_BLOCK == 0 and t_len <= MOBA_BLOCK and t_len == 8
    ppb = MOBA_BLOCK // page
    n_blocks = past // MOBA_BLOCK
    n_sel = min(MOBA_TOPK, n_blocks + 1)
    assert n_blocks >= n_sel, "fewer cached blocks than top-k picks is not supported"
    n_col = t_len * N_HEADS
    assert n_col == LANES and page == LANES and n_pages <= LANES
    pool_t = jnp.transpose(pool, (0, 2, 3, 4, 1)).reshape(n_pool, 2, HD, page)
    q4 = q_s.reshape(dec_batch, t_len, N_HEADS, HEAD_DIM) * SCALE
    qbd = jnp.einsum('bthx,hg->bhtgx', q4, np.eye(N_HEADS, dtype=np.float32)).reshape(dec_batch, n_col, HD)
    rev = tab.T.astype(F32)[:, past + t_len - 1 - np.arange(past + t_len)]
    by_t = jnp.stack([rev[:, t_len - 1 - t:t_len - 1 - t + past] for t in range(t_len)], axis=1)
    bias_pages = jnp.transpose(by_t.reshape(N_HEADS, t_len, n_pages, page), (2, 0, 1, 3)).reshape(n_pages, n_col, page)
    d_own = np.arange(t_len)[:, None] - np.arange(LANES)[None, :]
    ok_own = (d_own >= 0) & (np.arange(LANES)[None, :] < t_len)
    bias_own = jnp.where(ok_own[None], tab.T.astype(F32)[:, np.clip(d_own, 0, t_len)], NEG)
    bias_own = bias_own.reshape(n_col, LANES)
    dmask = jnp.asarray(np.arange(N_HEADS)[:, None] == (np.arange(HD) // HEAD_DIM)[None, :], F32)
    r_qh = np.arange(n_col)
    perm = jnp.asarray((r_qh % N_HEADS)[:, None] * t_len + (r_qh // N_HEADS)[:, None] == np.arange(n_col)[None, :], F32)
    pair = jnp.asarray((np.arange(LANES)[:, None] // ppb == np.arange(LANES)[None, :])
                       & (np.arange(LANES)[:, None] < n_pages), F32)
    new3 = kvb_s.reshape(dec_batch, t_len, 2 * HD)
    knt = jnp.pad(jnp.swapaxes(new3[:, :, :HD], 1, 2), ((0, 0), (0, 0), (0, LANES - t_len)))
    vnew = jnp.pad(new3[:, :, HD:], ((0, 0), (0, LANES - t_len), (0, 0)))

    pp = 4 if n_pages % 4 == 0 else 1
    page_spec = lambda u: pl.BlockSpec((None, 2, HD, page), lambda b, j, pt: (pt[b, j * pp + u], 0, 0, 0))
    stat = lambda rows: pl.BlockSpec((None, rows, LANES), lambda b, j, pt: (b, 0, 0))
    ksum, m_p, l_p, acc = pl.pallas_call(
        functools.partial(_moba_page_kernel, t_len=t_len, pp=pp),
        out_shape=(jax.ShapeDtypeStruct((dec_batch, HD, LANES), F32),
                   jax.ShapeDtypeStruct((dec_batch, n_col, LANES), F32),
                   jax.ShapeDtypeStruct((dec_batch, n_col, LANES), F32),
                   jax.ShapeDtypeStruct((dec_batch, n_pages * t_len, HD), F32)),
        grid_spec=pltpu.PrefetchScalarGridSpec(
            num_scalar_prefetch=1,
            grid=(dec_batch, n_pages // pp),
            in_specs=[page_spec(u) for u in range(pp)] + [
                pl.BlockSpec((None, n_col, HD), lambda b, j, pt: (b, 0, 0)),
                pl.BlockSpec((pp, n_col, page), lambda b, j, pt: (j, 0, 0)),
            ],
            out_specs=(stat(HD), stat(n_col), stat(n_col),
                       pl.BlockSpec((None, pp * t_len, HD), lambda b, j, pt: (b, j, 0))),
        ),
        compiler_params=_params(("parallel", "arbitrary")),
        name="moba_page",
    )(page_table, *([pool_t] * pp), qbd, bias_pages)

    acc = jnp.swapaxes(acc.reshape(dec_batch, n_pages, t_len, HD), 1, 2)
    per_b = lambda shape: pl.BlockSpec((None,) + shape, lambda b: (b,) + (0,) * len(shape))
    const = lambda shape: pl.BlockSpec(shape, lambda b: (0,) * len(shape))
    out = pl.pallas_call(
        functools.partial(_moba_combine_kernel, n_pages=n_pages, n_blocks=n_blocks, n_sel=n_sel, t_len=t_len),
        out_shape=jax.ShapeDtypeStruct((dec_batch, t_len, HD), F32),
        grid=(dec_batch,),
        in_specs=[per_b((HD, LANES)), per_b((n_col, LANES)), per_b((n_col, LANES)),
                  per_b((t_len, n_pages, HD)), per_b((n_col, HD)), per_b((HD, LANES)), per_b((LANES, HD)),
                  const((n_col, LANES)), const((LANES, LANES)), const((N_HEADS, HD)), const((n_col, n_col))],
        out_specs=per_b((t_len, HD)),
        compiler_params=_params(("parallel",)),
        name="moba_combine",
    )(ksum, m_p, l_p, acc, qbd, knt, vnew, bias_own, pair, dmask, perm)
    return out.reshape(dec_batch * t_len, HD)


def kernel(x_prompt, x_sample, cache_a_kv_g0, cache_a_kv_g1, cache_a_kv_g2, cache_b_kv_pool, page_table, state_ffn_conv, rel_bias, norm_attn, norm_ffn, w_qkv_a, q_norm_a, k_norm_a, w_o_a, norm_kv, w_kv_b, k_norm_b, w_q_b, q_norm_b, w_o_b, w_up, conv_w, conv_b, w_down):
    B, S, D = x_prompt.shape
    DB, T, _ = x_sample.shape
    depth = norm_attn.shape[0]
    n_self = w_qkv_a.shape[0]
    caches = (cache_a_kv_g0, cache_a_kv_g1, cache_a_kv_g2)
    past = page_table.shape[1] * cache_b_kv_pool.shape[1]
    assert D == HD

    tab = rel_bias[_rel_bucket(jnp.arange(max(S, past + T, DIL_GROUPS[-1][0] + 1)))]
    group_bias = [tab[dil * np.arange(win // dil + 1)] for (win, dil) in DIL_GROUPS]

    ones =jnp.ones((HD,), F32)
    tile_h = lambda gvec, n: jnp.tile(gvec, n * N_HEADS)

    def self_layer(x, l, prompt):
        hg = jnp.concatenate([tile_h(q_norm_a[l], N_GROUPS), tile_h(k_norm_a[l], N_GROUPS), jnp.tile(ones, N_GROUPS)])
        qkv = norm_matmul(x, norm_attn[l], w_qkv_a[l].astype(BF16), hg, 2 * N_GROUPS * HD)
        outs, lses, bufs = [], [], []
        seq = S if prompt else T
        qkv3 = qkv.reshape(-1, seq, qkv.shape[1])
        for gi, (win, dil) in enumerate(DIL_GROUPS):
            keep = min(win, seq)
            part = lambda which: qkv3[:, seq - keep:, (which * N_GROUPS + gi) * HD:(which * N_GROUPS + gi + 1) * HD
                                      ].reshape(-1, keep, N_HEADS, HEAD_DIM)
            new_kv = jnp.stack([part(1), part(2)], axis=2)
            if prompt:
                o, lse = dil_prompt(qkv, B, S, gi, group_bias[gi])
                bufs.append(new_kv)
            else:
                o, lse, buf = dil_sample(caches[gi][l], new_kv, part(0), DB, T, gi, group_bias[gi])
                bufs.append(buf)
            outs.append(o)
            lses.append(lse)
        return merge_proj(outs, lses, w_o_a[l].astype(BF16), x), bufs

    def ffn(x, l, prompt):
        wup, wdn = w_up[l].astype(BF16), w_down[l].astype(BF16)
        if prompt:
            tm = 256
            y, ut = ffn_prompt(x, S, norm_ffn[l], wup, conv_w[l], conv_b[l], wdn, tm=tm)
            cs = ut.reshape(B, S // tm, 8, ut.shape[1])[:, -1, 8 - (CONV_W - 1):]
        else:
            y, ua, ub = ffn_sample(x, T, state_ffn_conv[l], norm_ffn[l], wup, conv_w[l], conv_b[l], wdn)
            u = jnp.concatenate([ua, ub], axis=1)
            cs = u.reshape(DB, T, u.shape[1])[:, T - (CONV_W - 1):]
        return y, cs

    def trunk(x, prompt):
        nb_rows = B if prompt else DB
        a_new = [[] for _ in range(N_GROUPS)]
        conv_new = []
        kvb = None
        prep = None
        for l in range(depth):
            if l < n_self:
                x, bufs = self_layer(x, l, prompt)
                for gi in range(N_GROUPS):
                    a_new[gi].append(bufs[gi])
            else:
                if l == n_self:
                    hg = jnp.concatenate([tile_h(k_norm_b, 1), ones])
                    kvb = norm_matmul(x, norm_kv, w_kv_b.astype(BF16), hg, HD)
                    if prompt:
                        prep = kv_prep(kvb, B)
                lb = l - n_self
                q = norm_matmul(x, norm_attn[l], w_q_b[lb].astype(BF16), tile_h(q_norm_b[lb], 1), HD)
                if prompt:
                    o = moba_prompt(q, prep[1], prep[2], prep[0], tab, B, S)
                else:
                    o = moba_sample(q, kvb, cache_b_kv_pool, page_table, tab, DB, T)
                x = merge_proj([o], None, w_o_b[lb].astype(BF16), x)
            x, cs = ffn(x, l, prompt)
            conv_new.append(cs)
        seq = S if prompt else T
        if prompt:
            kv_new = jnp.transpose(prep[3].reshape(B, 2, N_HEADS, HEAD_DIM, S), (0, 4, 1, 2, 3))
        else:
            kv_new = kvb.reshape(nb_rows, seq, 2, N_HEADS, HEAD_DIM)
        return x.reshape(nb_rows, seq, D), [jnp.stack(a, 0) for a in a_new], kv_new, jnp.stack(conv_new, 0)

    y_p, a_p, kv_p, conv_p = trunk(x_prompt.reshape(B * S, D), True)
    y_s, a_s, kv_s, conv_s = trunk(x_sample.reshape(DB * T, D), False)
    return (y_p, y_s, a_p[0], a_p[1], a_p[2], a_s[0], a_s[1], a_s[2], kv_p, kv_s, conv_p, conv_s)
```

```python
import functools
import math

import jax
import jax.numpy as jnp
import numpy as np
from jax import lax
from jax.experimental import pallas as pl
from jax.experimental.pallas import tpu as pltpu

HEAD_DIM = 64
N_HEADS = 16
HD = N_HEADS * HEAD_DIM
DIL_GROUPS = ((128, 1), (512, 4), (2048, 16))
N_GROUPS = len(DIL_GROUPS)
N_BACK = 128
MOBA_BLOCK = 256
MOBA_TOPK = 3
NUM_BUCKETS = 32
REL_MAX_DIST = 4096
CONV_W = 3
EPS = 1e-6
SCALE = HEAD_DIM ** -0.5
NEG = -1e30
LOG2E = 1.4426950408889634
LANES = 128
MXU_TILE = 256
HEADS_PER_LANE_TILE = LANES // HEAD_DIM
VMEM_LIMIT = 56 * 1024 * 1024

F32 = jnp.float32
BF16 = jnp.bfloat16
HIGHEST = lax.Precision.HIGHEST


def _params(sem, vmem=VMEM_LIMIT):
    return pltpu.CompilerParams(dimension_semantics=sem, vmem_limit_bytes=vmem)


def _const_spec(shape):
    nd = len(shape)
    return pl.BlockSpec(shape, lambda *_: (0,) * nd, pipeline_mode=pl.Buffered(1))


def _rel_bucket(dist):
    n = jnp.maximum(dist, 0)
    exact = NUM_BUCKETS // 2
    nf = jnp.maximum(n, exact).astype(F32)
    large = exact + (jnp.log(nf / exact) / math.log(REL_MAX_DIST / exact) * (NUM_BUCKETS - exact)).astype(jnp.int32)
    return jnp.where(n < exact, n, jnp.minimum(large, NUM_BUCKETS - 1))


def _norm_matmul_kernel(x_ref, g_ref, w_ref, hg_ref, bd_ref, o_ref, h_scr, *, n_norm_tiles):
    j = pl.program_id(1)

    @pl.when(j == 0)
    def _():
        x = x_ref[...]
        r = lax.rsqrt(jnp.mean(x * x, axis=-1, keepdims=True) + EPS)
        h_scr[...] = (x * r * g_ref[...]).astype(BF16)

    y = jnp.dot(h_scr[...], w_ref[...], preferred_element_type=F32)

    @pl.when(j < n_norm_tiles)
    def _():
        y2 = (y * y).astype(BF16)
        ss = jnp.concatenate([jnp.dot(y2[:, c:c + MXU_TILE], bd_ref[...], preferred_element_type=F32)
                              for c in range(0, y.shape[1], MXU_TILE)], axis=1)
        o_ref[...] = y * lax.rsqrt(ss * (1.0 / HEAD_DIM) + EPS) * hg_ref[...]

    @pl.when(j >= n_norm_tiles)
    def _():
        o_ref[...] = y


def norm_matmul(x, g, w_bf16, head_gain, n_norm_cols, *, tn=512):
    M, K = x.shape
    N = w_bf16.shape[1]
    tm = min(1024, M)
    assert M % tm == 0 and N % tn == 0 and n_norm_cols % tn == 0 and tn % MXU_TILE == 0
    hid = np.arange(MXU_TILE) // HEAD_DIM
    bd = jnp.asarray(hid[:, None] == hid[None, :], BF16)
    return pl.pallas_call(
        functools.partial(_norm_matmul_kernel, n_norm_tiles=n_norm_cols // tn),
        out_shape=jax.ShapeDtypeStruct((M, N), F32),
        grid=(M // tm, N // tn),
        in_specs=[
            pl.BlockSpec((tm, K), lambda i, j: (i, 0)),
            pl.BlockSpec((1, K), lambda i, j: (0, 0)),
            pl.BlockSpec((K, tn), lambda i, j: (0, j)),
            pl.BlockSpec((1, tn), lambda i, j: (0, j)),
            pl.BlockSpec((MXU_TILE, MXU_TILE), lambda i, j: (0, 0)),
        ],
        out_specs=pl.BlockSpec((tm, tn), lambda i, j: (i, j)),
        scratch_shapes=[pltpu.VMEM((tm, K), BF16)],
        compiler_params=_params(("parallel", "arbitrary")),
        name="norm_matmul",
    )(x, g.reshape(1, K), w_bf16, head_gain.reshape(1, N), bd)


def _merge_proj_kernel(*refs, n_g):
    o_refs = refs[:n_g]
    lse_refs = refs[n_g:2 * n_g] if n_g > 1 else ()
    w_ref, x_ref, out_ref = refs[-3:]
    if n_g == 1:
        a = o_refs[0][...]
    else:
        ls = [r[...] for r in lse_refs]
        mx = functools.reduce(jnp.maximum, ls)
        ws = [jnp.exp(l - mx) for l in ls]
        den = functools.reduce(lambda p, q: p + q, ws)
        num = functools.reduce(lambda p, q: p + q, [w * r[...] for w, r in zip(ws, o_refs)])
        a = num / den
    out_ref[...] = x_ref[...] + jnp.dot(a.astype(BF16), w_ref[...], preferred_element_type=F32)


def merge_proj(os_, lses, w_bf16, x):
    M, D = x.shape
    n_g = len(os_)
    tm = min(256, M)
    assert M % tm == 0
    row = pl.BlockSpec((tm, D), lambda i: (i, 0))
    ins = list(os_) + (list(lses) if n_g > 1 else [])
    return pl.pallas_call(
        functools.partial(_merge_proj_kernel, n_g=n_g),
        out_shape=jax.ShapeDtypeStruct((M, D), F32),
        grid=(M // tm,),
        in_specs=[row] * len(ins) + [_const_spec(w_bf16.shape), row],
        out_specs=row,
        compiler_params=_params(("parallel",)),
        name="merge_proj",
    )(*ins, w_bf16, x)


def _silu_gate(ca, cb):
    return (ca / (1.0 + jnp.exp(-ca))) * cb


def _ffn_prompt_kernel(x_ref, xp_ref, g_ref, wup_ref, cw_ref, cb_ref, wdn_ref, y_ref, ut_ref,
                       ua_scr, ub_scr, gate_scr, *, tm, ck, d_ff, tiles_per_seq):
    i = pl.program_id(0)
    x = x_ref[...]
    keep = jnp.where(i % tiles_per_seq == 0, 0.0, 1.0)
    xc = jnp.concatenate([xp_ref[...] * keep, x], axis=0)
    r = lax.rsqrt(jnp.mean(xc * xc, axis=-1, keepdims=True) + EPS)
    h = (xc * r * g_ref[...]).astype(BF16)
    for c in range(d_ff // ck):
        a0, b0 = c * ck, d_ff + c * ck
        ua_scr[...] = jnp.dot(h, wup_ref[:, a0:a0 + ck], preferred_element_type=F32)
        ub_scr[...] = jnp.dot(h, wup_ref[:, b0:b0 + ck], preferred_element_type=F32)
        ut_ref[:, a0:a0 + ck] = ua_scr[tm:tm + 8, :]
        ut_ref[:, b0:b0 + ck] = ub_scr[tm:tm + 8, :]
        ca = cb_ref[:, a0:a0 + ck]
        cb = cb_ref[:, b0:b0 + ck]
        for j in range(CONV_W):
            ca = ca + cw_ref[j:j + 1, a0:a0 + ck] * ua_scr[6 + j:6 + j + tm, :]
            cb = cb + cw_ref[j:j + 1, b0:b0 + ck] * ub_scr[6 + j:6 + j + tm, :]
        gate_scr[:, a0:a0 + ck] = _silu_gate(ca, cb).astype(BF16)
    y_ref[...] = x + jnp.dot(gate_scr[...], wdn_ref[...], preferred_element_type=F32)


def ffn_prompt(x, seq_len, g, wup_bf16, cw, cb, wdn_bf16, *, tm=256, ck=256):
    M, D = x.shape
    d_ff = wdn_bf16.shape[0]
    assert seq_len % tm == 0 and M % seq_len == 0 and d_ff % ck == 0 and CONV_W - 1 <= 8
    n_tiles = M // tm
    y, ut = pl.pallas_call(
        functools.partial(_ffn_prompt_kernel, tm=tm, ck=ck, d_ff=d_ff, tiles_per_seq=seq_len // tm),
        out_shape=(jax.ShapeDtypeStruct((M, D), F32), jax.ShapeDtypeStruct((n_tiles * 8, 2 * d_ff), F32)),
        grid=(n_tiles,),
        in_specs=[
            pl.BlockSpec((tm, D), lambda i: (i, 0)),
            pl.BlockSpec((8, D), lambda i: (jnp.maximum(i * (tm // 8) - 1, 0), 0)),
            _const_spec((1, D)),
            _const_spec(wup_bf16.shape),
            _const_spec(cw.shape),
            _const_spec((1, 2 * d_ff)),
            _const_spec(wdn_bf16.shape),
        ],
        out_specs=(pl.BlockSpec((tm, D), lambda i: (i, 0)), pl.BlockSpec((8, 2 * d_ff), lambda i: (i, 0))),
        scratch_shapes=[pltpu.VMEM((tm + 8, ck), F32), pltpu.VMEM((tm + 8, ck), F32), pltpu.VMEM((tm, d_ff), BF16)],
        compiler_params=_params(("parallel",)),
        name="ffn_prompt",
    )(x, x, g.reshape(1, D), wup_bf16, cw, cb.reshape(1, 2 * d_ff), wdn_bf16)
    return y, ut


def _ffn_sample_kernel(x_ref, g_ref, wa_ref, wb_ref, cwa_ref, cwb_ref, cba_ref, cbb_ref,
                       e1a_ref, e1b_ref, e0a_ref, e0b_ref, wdn_ref, y_ref, ua_ref, ub_ref,
                       h_scr, ua_scr, ub_scr, *, m, t_len):
    c = pl.program_id(0)

    @pl.when(c == 0)
    def _():
        x = x_ref[...]
        r = lax.rsqrt(jnp.mean(x * x, axis=-1, keepdims=True) + EPS)
        h_scr[...] = (x * r * g_ref[...]).astype(BF16)
        y_ref[...] = x
        ua_scr[0:8, :] = jnp.zeros((8, ua_scr.shape[1]), F32)
        ub_scr[0:8, :] = jnp.zeros((8, ub_scr.shape[1]), F32)

    h = h_scr[...]
    ua = jnp.dot(h, wa_ref[...], preferred_element_type=F32)
    ub = jnp.dot(h, wb_ref[...], preferred_element_type=F32)
    ua_ref[...] = ua
    ub_ref[...] = ub
    ua_scr[8:8 + m, :] = ua
    ub_scr[8:8 + m, :] = ub
    t = lax.broadcasted_iota(jnp.int32, ua.shape, 0) % t_len

    def conv(u, u_scr, cw_ref, cb_ref, e1_ref, e0_ref):
        um1 = jnp.where(t >= 1, u_scr[7:7 + m, :], e1_ref[...])
        um2 = jnp.where(t >= 2, u_scr[6:6 + m, :], jnp.where(t == 1, e1_ref[...], e0_ref[...]))
        return cb_ref[...] + cw_ref[0:1, :] * um2 + cw_ref[1:2, :] * um1 + cw_ref[2:3, :] * u

    ca = conv(ua, ua_scr, cwa_ref, cba_ref, e1a_ref, e0a_ref)
    cb = conv(ub, ub_scr, cwb_ref, cbb_ref, e1b_ref, e0b_ref)
    gt = _silu_gate(ca, cb).astype(BF16)
    y_ref[...] += jnp.dot(gt, wdn_ref[...], preferred_element_type=F32)


def ffn_sample(x, t_len, prev, g, wup_bf16, cw, cb, wdn_bf16, *, ck=256):
    M, D = x.shape
    d_ff = wdn_bf16.shape[0]
    assert d_ff % ck == 0 and CONV_W == 3 and t_len >= 2
    nc = d_ff // ck
    e1 = jnp.repeat(prev[:, 1], t_len, axis=0)
    e0 = jnp.repeat(prev[:, 0], t_len, axis=0)
    cb2 = cb.reshape(1, 2 * d_ff)
    a_col = lambda c: (0, c)
    b_col = lambda c: (0, nc + c)
    return pl.pallas_call(
        functools.partial(_ffn_sample_kernel, m=M, t_len=t_len),
        out_shape=(jax.ShapeDtypeStruct((M, D), F32), jax.ShapeDtypeStruct((M, d_ff), F32),
                   jax.ShapeDtypeStruct((M, d_ff), F32)),
        grid=(nc,),
        in_specs=[
            pl.BlockSpec((M, D), lambda c: (0, 0)),
            pl.BlockSpec((1, D), lambda c: (0, 0)),
            pl.BlockSpec((D, ck), a_col), pl.BlockSpec((D, ck), b_col),
            pl.BlockSpec((CONV_W, ck), a_col), pl.BlockSpec((CONV_W, ck), b_col),
            pl.BlockSpec((1, ck), a_col), pl.BlockSpec((1, ck), b_col),
            pl.BlockSpec((M, ck), a_col), pl.BlockSpec((M, ck), b_col),
            pl.BlockSpec((M, ck), a_col), pl.BlockSpec((M, ck), b_col),
            pl.BlockSpec((ck, D), lambda c: (c, 0)),
        ],
        out_specs=(pl.BlockSpec((M, D), lambda c: (0, 0)),
                   pl.BlockSpec((M, ck), a_col), pl.BlockSpec((M, ck), a_col)),
        scratch_shapes=[pltpu.VMEM((M, D), BF16), pltpu.VMEM((M + 8, ck), F32), pltpu.VMEM((M + 8, ck), F32)],
        compiler_params=_params(("arbitrary",)),
        name="ffn_sample",
    )(x, g.reshape(1, D), wup_bf16, wup_bf16, cw, cw, cb2, cb2, e1, e1, e0, e0, wdn_bf16)


def _toeplitz(seg_row, rows, lo, width):
    x = jnp.broadcast_to(seg_row, (rows, seg_row.shape[1]))
    return pltpu.roll(x, 0, 1, stride=1, stride_axis=0)[:, lo:lo + width]


DIL_UNIT = 4


def _dil_prompt_kernel(q_ref, kc_ref, kp_ref, vc_ref, vp_ref, seg_ref, o_ref, lse_ref, bias_scr, kcat_scr, vcat_scr,
                       *, d, n_jc):
    b = pl.program_id(1)
    i = pl.program_id(2)
    halo = N_BACK * d
    tq, tk = N_BACK, 2 * N_BACK

    @pl.when((b == 0) & (i == 0))
    def _():
        for h in range(HEADS_PER_LANE_TILE):
            bias_scr[h] = _toeplitz(seg_ref[h], tq, tq, tk)

    kcat_scr[0:halo, :] = kp_ref[...]
    kcat_scr[halo:, :] = kc_ref[...]
    vcat_scr[0:halo, :] = vp_ref[...]
    vcat_scr[halo:, :] = vc_ref[...]

    def rows(ref, start, n):
        return ref[pl.ds(start, n), :] if d == 1 else ref[pl.ds(start, n, stride=d), :]

    lane = lax.broadcasted_iota(jnp.int32, (tq, LANES), 1)
    col = lax.broadcasted_iota(jnp.int32, (tq, tk), 1)
    first = lane < HEAD_DIM
    heads = range(HEADS_PER_LANE_TILE)

    def unit(u, carry):
        starts, deads, logits, vals = [], [], [], []
        for s in range(DIL_UNIT):
            pidx = u * DIL_UNIT + s
            r, jc = pidx // n_jc, pidx % n_jc
            start = r + d * (jc * N_BACK)
            starts.append(start)
            deads.append(jnp.where((i == 0) & (jc == 0), N_BACK, 0))
            q = rows(q_ref, start, tq) * SCALE
            k = rows(kcat_scr, start, tk).astype(BF16)
            vals.append(rows(vcat_scr, start, tk).astype(BF16))
            for h in heads:
                hm = (lane >= h * HEAD_DIM) & (lane < (h + 1) * HEAD_DIM)
                qh = jnp.where(hm, q, 0.0).astype(BF16)
                logits.append(lax.dot_general(qh, k, (((1,), (1,)), ((), ())), preferred_element_type=F32))
        probs, stats = [], []
        for n, sc in enumerate(logits):
            sc = jnp.where(col < deads[n // HEADS_PER_LANE_TILE], NEG, sc + bias_scr[n % HEADS_PER_LANE_TILE])
            m = jnp.max(sc, axis=-1, keepdims=True)
            e = jnp.exp(sc - m)
            probs.append(e.astype(BF16))
            stats.append((m, jnp.sum(e, axis=-1, keepdims=True)))
        for s in range(DIL_UNIT):
            outs, lses = [], []
            for h in heads:
                n = s * HEADS_PER_LANE_TILE + h
                m, l = stats[n]
                outs.append(jnp.dot(probs[n], vals[s], preferred_element_type=F32) / l)
                lses.append(m + jnp.log(l))
            o_val = jnp.where(first, outs[0], outs[1])
            lse_val = jnp.where(first, lses[0], lses[1])
            if d == 1:
                o_ref[pl.ds(starts[s], tq), :] = o_val
                lse_ref[pl.ds(starts[s], tq), :] = lse_val
            else:
                o_ref[pl.ds(starts[s], tq, stride=d), :] = o_val
                lse_ref[pl.ds(starts[s], tq, stride=d), :] = lse_val
        return carry

    lax.fori_loop(0, d * n_jc // DIL_UNIT, unit, 0)


def dil_prompt(qkv, batch, seq_len, gi, bias_k):
    win, d = DIL_GROUPS[gi]
    assert win // d == N_BACK and seq_len % (d * N_BACK) == 0
    halo = N_BACK * d
    blk = max(halo, min(2048, seq_len))
    n_jc = blk // halo
    assert seq_len % blk == 0 and blk % halo == 0 and (d * n_jc) % DIL_UNIT == 0
    hp_tiles = HD // LANES
    qkv3 = qkv.reshape(batch, seq_len, qkv.shape[1])
    width = 3 * N_BACK
    step = 2 * N_BACK - np.arange(width)
    seg = jnp.where(((step >= 0) & (step <= N_BACK))[:, None], bias_k[np.clip(step, 0, N_BACK)].astype(F32), NEG)
    seg = jnp.transpose(seg)[:, None, :]

    def col(which):
        return (which * N_GROUPS + gi) * hp_tiles

    cur = lambda which: pl.BlockSpec((None, blk, LANES), lambda hp, b, i: (b, i, col(which) + hp))
    prev = lambda which: pl.BlockSpec((None, halo, LANES),
                                      lambda hp, b, i: (b, jnp.maximum(i * n_jc - 1, 0), col(which) + hp))
    out_spec = pl.BlockSpec((None, blk, LANES), lambda hp, b, i: (b, i, hp))
    o, lse = pl.pallas_call(
        functools.partial(_dil_prompt_kernel, d=d, n_jc=n_jc),
        out_shape=(jax.ShapeDtypeStruct((batch, seq_len, HD), F32),) * 2,
        grid=(hp_tiles, batch, seq_len // blk),
        in_specs=[cur(0), cur(1), prev(1), cur(2), prev(2),
                  pl.BlockSpec((HEADS_PER_LANE_TILE, 1, width), lambda hp, b, i: (hp, 0, 0))],
        out_specs=(out_spec, out_spec),
        scratch_shapes=[pltpu.VMEM((HEADS_PER_LANE_TILE, N_BACK, 2 * N_BACK), F32),
                        pltpu.VMEM((halo + blk, LANES), F32), pltpu.VMEM((halo + blk, LANES), F32)],
        compiler_params=_params(("parallel", "arbitrary", "arbitrary")),
        name=f"dil_prompt_g{gi}",
    )(qkv3, qkv3, qkv3, qkv3, qkv3, seg)
    return o.reshape(batch * seq_len, HD), lse.reshape(batch * seq_len, HD)


def _dil_sample_kernel(ct_ref, q_ref, newt_ref, bias_c_ref, bias_n_ref, ot_ref, o_ref, lse_ref, *, hb, t_len, L):
    lane = lax.broadcasted_iota(jnp.int32, (HEAD_DIM, LANES), 1)
    nt = (((1,), (1,)), ((), ()))
    for hh in range(hb):
        q = (q_ref[hh] * SCALE).astype(BF16)
        kt, vt = ct_ref[0, hh], ct_ref[1, hh]
        knt, vnt = newt_ref[0, hh], newt_ref[1, hh]
        s_c = jnp.dot(q, kt.astype(BF16), preferred_element_type=F32) + bias_c_ref[hh]
        s_n = jnp.dot(q, knt.astype(BF16), preferred_element_type=F32) + bias_n_ref[hh]
        m = jnp.maximum(jnp.max(s_c, axis=1, keepdims=True), jnp.max(s_n, axis=1, keepdims=True))
        e_c = jnp.exp(s_c - m)
        e_n = jnp.exp(s_n - m)
        l = jnp.sum(e_c, axis=1, keepdims=True) + jnp.sum(e_n, axis=1, keepdims=True)
        pv = (lax.dot_general(e_c.astype(BF16), vt.astype(BF16), nt, preferred_element_type=F32)
              + lax.dot_general(e_n.astype(BF16), vnt.astype(BF16), nt, preferred_element_type=F32))
        o_ref[hh] = pv / l
        lse_ref[hh] = jnp.broadcast_to(m + jnp.log(l), (t_len, HEAD_DIM))
        for kv, (old, new) in enumerate(((kt, knt), (vt, vnt))):
            moved = pltpu.roll(old, L - t_len, 1)
            if L > LANES:
                ot_ref[kv, hh, :, :L - LANES] = moved[:, :L - LANES]
            ot_ref[kv, hh, :, L - LANES:] = jnp.where(lane >= LANES - t_len, new, moved[:, L - LANES:])


def dil_sample(cache, new_kv, q_s, dec_batch, t_len, gi, bias_k):
    win, d = DIL_GROUPS[gi]
    L = cache.shape[1]
    assert L == win and win // d == N_BACK and L % LANES == 0 and t_len <= LANES
    hb = min(N_HEADS, max(1, (4 << 20) // (2 * HEAD_DIM * L * 4)))
    assert N_HEADS % hb == 0
    ct = jnp.transpose(cache, (0, 2, 3, 4, 1))
    newt = jnp.pad(jnp.transpose(new_kv, (0, 2, 3, 4, 1)), ((0, 0),) * 4 + ((LANES - t_len, 0),))
    qh = jnp.transpose(q_s, (0, 2, 1, 3))
    tq = np.arange(t_len)[:, None]
    back = L + tq - np.arange(L)[None, :]
    ok_c = (back % d == 0) & (back // d <= N_BACK)
    bias_kt = bias_k.T.astype(F32)
    bias_c = jnp.where(ok_c[None], bias_kt[:, np.clip(back // d, 0, N_BACK)], NEG)
    t2 = np.arange(LANES)[None, :] - (LANES - t_len)
    ok_n = (t2 >= 0) & (t2 <= tq) & ((tq - t2) % d == 0)
    bias_n = jnp.where(ok_n[None], bias_kt[:, np.clip((tq - t2) // d, 0, N_BACK)], NEG)
    cache_spec = pl.BlockSpec((None, 2, hb, HEAD_DIM, L), lambda b, j: (b, 0, j, 0, 0))
    row_spec = pl.BlockSpec((None, hb, t_len, HEAD_DIM), lambda b, j: (b, j, 0, 0))
    ot, o, lse = pl.pallas_call(
        functools.partial(_dil_sample_kernel, hb=hb, t_len=t_len, L=L),
        out_shape=(jax.ShapeDtypeStruct(ct.shape, F32),
                   jax.ShapeDtypeStruct(qh.shape, F32), jax.ShapeDtypeStruct(qh.shape, F32)),
        grid=(dec_batch, N_HEADS // hb),
        in_specs=[
            cache_spec,
            row_spec,
            pl.BlockSpec((None, 2, hb, HEAD_DIM, LANES), lambda b, j: (b, 0, j, 0, 0)),
            pl.BlockSpec((hb, t_len, L), lambda b, j: (j, 0, 0)),
            pl.BlockSpec((hb, t_len, LANES), lambda b, j: (j, 0, 0)),
        ],
        out_specs=(cache_spec, row_spec, row_spec),
        compiler_params=_params(("parallel", "parallel")),
        name=f"dil_sample_g{gi}",
    )(ct, qh, newt, bias_c, bias_n)
    to_rows = lambda a: jnp.transpose(a, (0, 2, 1, 3)).reshape(dec_batch * t_len, HD)
    return to_rows(o), to_rows(lse), jnp.transpose(ot, (0, 4, 1, 2, 3))


def _kv_prep_kernel(kv_ref, mean_ref, k16_ref, vt16_ref, kvt_ref):
    k = kv_ref[:, :HD]
    mean_ref[...] = jnp.broadcast_to(jnp.sum(k, axis=0, keepdims=True) * (1.0 / MOBA_BLOCK), mean_ref.shape)
    for hp in range(HD // LANES):
        k16_ref[hp] = k[:, hp * LANES:(hp + 1) * LANES].astype(BF16)
    kvt = kv_ref[...].T
    kvt_ref[...] = kvt
    vt16_ref[...] = kvt[HD:].astype(BF16)


def kv_prep(kvb, batch):
    rows = kvb.shape[0]
    n_blocks = rows // MOBA_BLOCK
    per_seq = n_blocks // batch
    hp_tiles = HD // LANES
    means, k16, vt16, kvt = pl.pallas_call(
        _kv_prep_kernel,
        out_shape=(jax.ShapeDtypeStruct((n_blocks, 8, HD), F32),
                   jax.ShapeDtypeStruct((hp_tiles, rows, LANES), BF16),
                   jax.ShapeDtypeStruct((HD, rows), BF16),
                   jax.ShapeDtypeStruct((batch, 2 * HD, rows // batch), F32)),
        grid=(n_blocks,),
        in_specs=[pl.BlockSpec((MOBA_BLOCK, 2 * HD), lambda i: (i, 0))],
        out_specs=(pl.BlockSpec((None, 8, HD), lambda i: (i, 0, 0)),
                   pl.BlockSpec((hp_tiles, MOBA_BLOCK, LANES), lambda i: (0, i, 0)),
                   pl.BlockSpec((HD, MOBA_BLOCK), lambda i: (0, i)),
                   pl.BlockSpec((None, 2 * HD, MOBA_BLOCK), lambda i: (i // per_seq, 0, i % per_seq))),
        compiler_params=_params(("parallel",)),
        name="kv_prep",
    )(kvb)
    return means[:, 0], k16, vt16, kvt


def _top_blocks(sc, blk, n_valid, n_sel, axis):
    sel = jnp.zeros(sc.shape, F32)
    big = float(sc.shape[axis])
    for it in range(n_sel):
        mx = jnp.max(sc, axis=axis, keepdims=True)
        idx = jnp.min(jnp.where(sc == mx, blk, big), axis=axis, keepdims=True)
        hit = blk == idx
        sel = jnp.maximum(sel, jnp.where(hit, jnp.where(it < n_valid, 1.0, 0.0), 0.0))
        sc = jnp.where(hit, -jnp.inf, sc)
    return sel


def _moba_prompt_kernel(q_ref, k_ref, vt_ref, km_ref, seg_ref, o_ref,
                        m_scr, l_scr, acc_scr, selb_scr, bias_scr, pa_scr, pb_scr, sha_scr, shb_scr, ala_scr, alb_scr,
                        *, nb, n_sel, kb):
    b = pl.program_id(1)
    i = pl.program_id(2)
    tb = MOBA_BLOCK

    def head_mask(shape, axis, h):
        idx = lax.broadcasted_iota(jnp.int32, shape, axis)
        return (idx >= h * HEAD_DIM) & (idx < (h + 1) * HEAD_DIM)

    @pl.when((b == 0) & (i == 0))
    def _():
        row = lax.broadcasted_iota(jnp.int32, (tb, tb), 0)
        colm = lax.broadcasted_iota(jnp.int32, (tb, tb), 1)
        for h in range(HEADS_PER_LANE_TILE):
            def fill(delta, carry, h=h):
                bias_scr[h, delta] = _toeplitz(seg_ref[h, pl.ds(delta, 1), :], tb, tb, tb) * LOG2E
                return carry
            lax.fori_loop(0, nb, fill, 0)
            bias_scr[h, 0] = jnp.where(colm >= row, bias_scr[h, 0], NEG)

    heads = range(HEADS_PER_LANE_TILE)
    m_scr[...] = jnp.full_like(m_scr, NEG)
    l_scr[...] = jnp.zeros_like(l_scr)
    acc_scr[...] = jnp.zeros_like(acc_scr)
    q_raw = q_ref[...]
    km = km_ref[...]
    blk = lax.broadcasted_iota(jnp.int32, (nb, tb), 0).astype(F32)
    i_f = i.astype(F32)
    for h in heads:
        kmh = jnp.where(head_mask(km.shape, 1, h), km, 0.0)
        sc = lax.dot_general(kmh, q_raw, (((1,), (1,)), ((), ())), precision=HIGHEST, preferred_element_type=F32)
        sc = jnp.where(blk < i_f, sc, -jnp.inf)
        sel = _top_blocks(sc, blk, i, n_sel, axis=0) + jnp.where(blk == i_f, 1.0, 0.0)
        selb_scr[h] = jnp.where(sel > 0.5, 0.0, NEG)

    q = q_raw * (SCALE * LOG2E)
    qhs = [jnp.where(head_mask(q.shape, 1, h), q, 0.0).astype(BF16) for h in heads]

    def logits_pass(kt, p_scr, sh_scr, al_scr):
        k0 = pl.multiple_of(kt * (kb * tb), kb * tb)
        k = k_ref[pl.ds(k0, kb * tb), :]
        raw = [[lax.dot_general(k[c * tb:(c + 1) * tb], qhs[h], (((1,), (1,)), ((), ())),
                                preferred_element_type=F32) for c in range(kb)] for h in heads]
        for h in heads:
            sbs, cms = [], []
            for c in range(kb):
                jj = kt * kb + c
                p = raw[h][c] + bias_scr[h, jnp.maximum(i - jj, 0)]
                p_scr[h, c] = p
                sb = selb_scr[h, pl.ds(jj, 1), :]
                sbs.append(sb)
                cms.append(jnp.max(p, axis=0, keepdims=True) + sb)
            m_old = m_scr[h]
            m_new = jnp.maximum(m_old, functools.reduce(jnp.maximum, cms))
            for c in range(kb):
                sh_scr[h, c] = m_new - sbs[c]
            al_scr[h] = jnp.exp2(m_old - m_new)
            m_scr[h] = m_new

    def value_pass(kt, p_scr, sh_scr, al_scr):
        k0 = pl.multiple_of(kt * (kb * tb), kb * tb)
        vt = vt_ref[:, pl.ds(k0, kb * tb)]
        for h in heads:
            alpha = al_scr[h]
            acc = alpha * acc_scr[h]
            l_new = alpha * l_scr[h]
            for c in range(kb):
                e = jnp.exp2(p_scr[h, c] - sh_scr[h, c])
                l_new = l_new + jnp.sum(e, axis=0, keepdims=True)
                acc = acc + jnp.dot(vt[h * HEAD_DIM:(h + 1) * HEAD_DIM, c * tb:(c + 1) * tb], e.astype(BF16),
                                    preferred_element_type=F32)
            l_scr[h] = l_new
            acc_scr[h] = acc

    own_tile = i // kb
    buf_a = (pa_scr, sha_scr, ala_scr)
    buf_b = (pb_scr, shb_scr, alb_scr)
    seq_tile = lambda n: jnp.where(n == 0, own_tile, n - 1)
    logits_pass(own_tile, *buf_a)

    def pair(j, carry):
        logits_pass(seq_tile(2 * j + 1), *buf_b)
        value_pass(seq_tile(2 * j), *buf_a)
        logits_pass(seq_tile(2 * j + 2), *buf_a)
        value_pass(seq_tile(2 * j + 1), *buf_b)
        return carry

    n_past = own_tile
    lax.fori_loop(0, n_past // 2, pair, 0)
    last_even = seq_tile(n_past - n_past % 2)

    @pl.when(n_past % 2 == 1)
    def _():
        logits_pass(seq_tile(n_past), *buf_b)
        value_pass(last_even, *buf_a)
        value_pass(seq_tile(n_past), *buf_b)

    @pl.when(n_past % 2 == 0)
    def _():
        value_pass(last_even, *buf_a)

    o_t = jnp.concatenate([acc_scr[h] / l_scr[h] for h in heads], axis=0)
    o_ref[...] = o_t.T


def moba_prompt(q, k16, vt16, k_means, tab, batch, seq_len):
    tb = MOBA_BLOCK
    kb = 4
    assert seq_len % (kb * tb) == 0
    nb = seq_len // tb
    n_sel = min(MOBA_TOPK, nb)
    hp_tiles = HD // LANES
    dist = np.arange(nb)[:, None] * tb + np.arange(2 * tb)[None, :] - tb
    seg = tab.T.astype(F32)[:, np.clip(dist, 0, tab.shape[0] - 1)]
    km3 = k_means.reshape(batch, nb, HD)
    return pl.pallas_call(
        functools.partial(_moba_prompt_kernel, nb=nb, n_sel=n_sel, kb=kb),
        out_shape=jax.ShapeDtypeStruct(q.shape, F32),
        grid=(hp_tiles, batch, nb),
        in_specs=[
            pl.BlockSpec((tb, LANES), lambda hp, b, i: (b * nb + i, hp)),
            pl.BlockSpec((None, seq_len, LANES), lambda hp, b, i: (hp, b, 0)),
            pl.BlockSpec((LANES, seq_len), lambda hp, b, i: (hp, b)),
            pl.BlockSpec((None, nb, LANES), lambda hp, b, i: (b, 0, hp)),
            pl.BlockSpec((HEADS_PER_LANE_TILE, nb, 2 * tb), lambda hp, b, i: (hp, 0, 0)),
        ],
        out_specs=pl.BlockSpec((tb, LANES), lambda hp, b, i: (b * nb + i, hp)),
        scratch_shapes=[pltpu.VMEM((HEADS_PER_LANE_TILE, 1, tb), F32),
                        pltpu.VMEM((HEADS_PER_LANE_TILE, 1, tb), F32),
                        pltpu.VMEM((HEADS_PER_LANE_TILE, HEAD_DIM, tb), F32),
                        pltpu.VMEM((HEADS_PER_LANE_TILE, nb, tb), F32),
                        pltpu.VMEM((HEADS_PER_LANE_TILE, nb, tb, tb), F32),
                        pltpu.VMEM((HEADS_PER_LANE_TILE, kb, tb, tb), F32),
                        pltpu.VMEM((HEADS_PER_LANE_TILE, kb, tb, tb), F32),
                        pltpu.VMEM((HEADS_PER_LANE_TILE, kb, 1, tb), F32),
                        pltpu.VMEM((HEADS_PER_LANE_TILE, kb, 1, tb), F32),
                        pltpu.VMEM((HEADS_PER_LANE_TILE, 1, tb), F32),
                        pltpu.VMEM((HEADS_PER_LANE_TILE, 1, tb), F32)],
        compiler_params=_params(("parallel", "arbitrary", "arbitrary")),
        name="moba_prompt",
    )(q, k16, vt16, km3, seg)


def _moba_page_kernel(pt_ref, *rest, t_len, pp):
    del pt_ref
    pool_refs, (qbd_ref, bias_ref, ksum_ref, m_ref, l_ref, acc_ref) = rest[:pp], rest[pp:]
    j = pl.program_id(1)
    qbd = qbd_ref[...].astype(BF16)
    hpg = MXU_TILE // HEAD_DIM
    n_grp = N_HEADS // hpg
    rpg = hpg * t_len
    nt = (((1,), (1,)), ((), ()))
    kts = [ref[0] for ref in pool_refs]
    logits = []
    for kt in kts:
        k16 = kt.astype(BF16)
        logits.append(jnp.concatenate(
            [jnp.dot(qbd[g * rpg:(g + 1) * rpg, g * MXU_TILE:(g + 1) * MXU_TILE], k16[g * MXU_TILE:(g + 1) * MXU_TILE],
                     preferred_element_type=F32) for g in range(n_grp)], axis=0))
    ms, ls, es = [], [], []
    for u, s in enumerate(logits):
        s = s + bias_ref[u]
        m = jnp.max(s, axis=1, keepdims=True)
        e = jnp.exp(s - m)
        ms.append(m)
        ls.append(jnp.sum(e, axis=1, keepdims=True))
        es.append(e.astype(BF16))
    lane = lax.broadcasted_iota(jnp.int32, (t_len, MXU_TILE), 1)
    for u in range(pp):
        v16 = pool_refs[u][1].astype(BF16)
        outs = []
        for g in range(n_grp):
            pv = lax.dot_general(es[u][g * rpg:(g + 1) * rpg], v16[g * MXU_TILE:(g + 1) * MXU_TILE], nt,
                                 preferred_element_type=F32)
            out = pv[0:t_len]
            for hl in range(1, hpg):
                out = jnp.where(lane >= hl * HEAD_DIM, pv[hl * t_len:(hl + 1) * t_len], out)
            outs.append(out)
        acc_ref[u * t_len:(u + 1) * t_len, :] = jnp.concatenate(outs, axis=1)

    @pl.when(j == 0)
    def _():
        ksum_ref[...] = jnp.zeros_like(ksum_ref)
        m_ref[...] = jnp.zeros_like(m_ref)
        l_ref[...] = jnp.zeros_like(l_ref)

    def put(ref, cols):
        lane = lax.broadcasted_iota(jnp.int32, ref.shape, 1)
        val = ref[...]
        for u, col in enumerate(cols):
            val = jnp.where(lane == j * pp + u, col, val)
        ref[...] = val

    put(ksum_ref, [jnp.sum(kt, axis=1, keepdims=True) for kt in kts])
    put(m_ref, ms)
    put(l_ref, ls)


def _moba_combine_kernel(ks_ref, m_ref, l_ref, acc_ref, qbd_ref, knt_ref, vnew_ref, bias_ref, pair_ref, dmask_ref,
                         perm_ref, o_ref, *, n_pages, n_blocks, n_sel, t_len):
    qbd = qbd_ref[...]
    pair = pair_ref[...]
    sc_pages = jnp.dot(qbd, ks_ref[...], precision=HIGHEST, preferred_element_type=F32)
    sc = jnp.dot(sc_pages, pair, precision=HIGHEST, preferred_element_type=F32) * (1.0 / MOBA_BLOCK)
    lane = lax.broadcasted_iota(jnp.int32, sc.shape, 1).astype(F32)
    sc = jnp.where(lane < n_blocks, sc, -jnp.inf)
    sel = _top_blocks(sc, lane, n_sel, n_sel, axis=1)
    picked = lax.dot_general(sel, pair, (((1,), (1,)), ((), ())), preferred_element_type=F32) > 0.5
    m_p = m_ref[...]
    s_own = jnp.dot(qbd.astype(BF16), knt_ref[...].astype(BF16), preferred_element_type=F32) + bias_ref[...]
    m_tot = jnp.maximum(jnp.max(jnp.where(picked, m_p, NEG), axis=1, keepdims=True),
                        jnp.max(s_own, axis=1, keepdims=True))
    w = jnp.where(picked, jnp.exp(m_p - m_tot), 0.0)
    e_own = jnp.exp(s_own - m_tot)
    l_tot = jnp.sum(w * l_ref[...], axis=1, keepdims=True) + jnp.sum(e_own, axis=1, keepdims=True)
    perm = perm_ref[...]
    to_query_major = lambda a: jnp.dot(perm, a, precision=HIGHEST, preferred_element_type=F32)
    w = to_query_major(w)
    e_own = to_query_major(e_own)
    l_tot = to_query_major(jnp.broadcast_to(l_tot, w.shape))[:, 0:1]
    dmask = dmask_ref[...]
    vnew = vnew_ref[...].astype(BF16)
    rows = []
    for t in range(t_len):
        hs = slice(t * N_HEADS, (t + 1) * N_HEADS)
        a = jnp.dot(w[hs, :n_pages], acc_ref[t], precision=HIGHEST, preferred_element_type=F32)
        a = a + jnp.dot(e_own[hs].astype(BF16), vnew, preferred_element_type=F32)
        num = jnp.sum(a * dmask, axis=0, keepdims=True)
        den = jnp.sum(l_tot[hs] * dmask, axis=0, keepdims=True)
        rows.append(num / den)
    o_ref[...] = jnp.concatenate(rows, axis=0)


def moba_sample(q_s, kvb_s, pool, page_table, tab, dec_batch, t_len):
    n_pool, page = pool.shape[0], pool.shape[1]
    n_pages = page_table.shape[1]
    past = n_pages * page
    assert MOBA_BLOCK % page == 0 and past % MOBA_BLOCK == 0 and t_len <= MOBA_BLOCK and t_len == 8
    ppb = MOBA_BLOCK // page
    n_blocks = past // MOBA_BLOCK
    n_sel = min(MOBA_TOPK, n_blocks + 1)
    assert n_blocks >= n_sel, "fewer cached blocks than top-k picks is not supported"
    n_col = t_len * N_HEADS
    assert n_col == LANES and page == LANES and n_pages <= LANES
    pool_t = jnp.transpose(pool, (0, 2, 3, 4, 1)).reshape(n_pool, 2, HD, page)
    q4 = q_s.reshape(dec_batch, t_len, N_HEADS, HEAD_DIM) * SCALE
    qbd = jnp.einsum('bthx,hg->bhtgx', q4, np.eye(N_HEADS, dtype=np.float32)).reshape(dec_batch, n_col, HD)
    rev = tab.T.astype(F32)[:, past + t_len - 1 - np.arange(past + t_len)]
    by_t = jnp.stack([rev[:, t_len - 1 - t:t_len - 1 - t + past] for t in range(t_len)], axis=1)
    bias_pages = jnp.transpose(by_t.reshape(N_HEADS, t_len, n_pages, page), (2, 0, 1, 3)).reshape(n_pages, n_col, page)
    d_own = np.arange(t_len)[:, None] - np.arange(LANES)[None, :]
    ok_own = (d_own >= 0) & (np.arange(LANES)[None, :] < t_len)
    bias_own = jnp.where(ok_own[None], tab.T.astype(F32)[:, np.clip(d_own, 0, t_len)], NEG)
    bias_own = bias_own.reshape(n_col, LANES)
    dmask = jnp.asarray(np.arange(N_HEADS)[:, None] == (np.arange(HD) // HEAD_DIM)[None, :], F32)
    r_qh = np.arange(n_col)
    perm = jnp.asarray((r_qh % N_HEADS)[:, None] * t_len + (r_qh // N_HEADS)[:, None] == np.arange(n_col)[None, :], F32)
    pair = jnp.asarray((np.arange(LANES)[:, None] // ppb == np.arange(LANES)[None, :])
                       & (np.arange(LANES)[:, None] < n_pages), F32)
    new3 = kvb_s.reshape(dec_batch, t_len, 2 * HD)
    knt = jnp.pad(jnp.swapaxes(new3[:, :, :HD], 1, 2), ((0, 0), (0, 0), (0, LANES - t_len)))
    vnew = jnp.pad(new3[:, :, HD:], ((0, 0), (0, LANES - t_len), (0, 0)))

    pp = 8 if n_pages % 8 == 0 else 1
    page_spec = lambda u: pl.BlockSpec((None, 2, HD, page), lambda b, j, pt: (pt[b, j * pp + u], 0, 0, 0))
    stat = lambda rows: pl.BlockSpec((None, rows, LANES), lambda b, j, pt: (b, 0, 0))
    ksum, m_p, l_p, acc = pl.pallas_call(
        functools.partial(_moba_page_kernel, t_len=t_len, pp=pp),
        out_shape=(jax.ShapeDtypeStruct((dec_batch, HD, LANES), F32),
                   jax.ShapeDtypeStruct((dec_batch, n_col, LANES), F32),
                   jax.ShapeDtypeStruct((dec_batch, n_col, LANES), F32),
                   jax.ShapeDtypeStruct((dec_batch, n_pages * t_len, HD), F32)),
        grid_spec=pltpu.PrefetchScalarGridSpec(
            num_scalar_prefetch=1,
            grid=(dec_batch, n_pages // pp),
            in_specs=[page_spec(u) for u in range(pp)] + [
                pl.BlockSpec((None, n_col, HD), lambda b, j, pt: (b, 0, 0)),
                pl.BlockSpec((pp, n_col, page), lambda b, j, pt: (j, 0, 0)),
            ],
            out_specs=(stat(HD), stat(n_col), stat(n_col),
                       pl.BlockSpec((None, pp * t_len, HD), lambda b, j, pt: (b, j, 0))),
        ),
        compiler_params=_params(("parallel", "arbitrary")),
        name="moba_page",
    )(page_table, *([pool_t] * pp), qbd, bias_pages)

    acc = jnp.swapaxes(acc.reshape(dec_batch, n_pages, t_len, HD), 1, 2)
    per_b = lambda shape: pl.BlockSpec((None,) + shape, lambda b: (b,) + (0,) * len(shape))
    const = lambda shape: pl.BlockSpec(shape, lambda b: (0,) * len(shape))
    out = pl.pallas_call(
        functools.partial(_moba_combine_kernel, n_pages=n_pages, n_blocks=n_blocks, n_sel=n_sel, t_len=t_len),
        out_shape=jax.ShapeDtypeStruct((dec_batch, t_len, HD), F32),
        grid=(dec_batch,),
        in_specs=[per_b((HD, LANES)), per_b((n_col, LANES)), per_b((n_col, LANES)),
                  per_b((t_len, n_pages, HD)), per_b((n_col, HD)), per_b((HD, LANES)), per_b((LANES, HD)),
                  const((n_col, LANES)), const((LANES, LANES)), const((N_HEADS, HD)), const((n_col, n_col))],
        out_specs=per_b((t_len, HD)),
        compiler_params=_params(("parallel",)),
        name="moba_combine",
    )(ksum, m_p, l_p, acc, qbd, knt, vnew, bias_own, pair, dmask, perm)
    return out.reshape(dec_batch * t_len, HD)


def kernel(x_prompt, x_sample, cache_a_kv_g0, cache_a_kv_g1, cache_a_kv_g2, cache_b_kv_pool, page_table, state_ffn_conv, rel_bias, norm_attn, norm_ffn, w_qkv_a, q_norm_a, k_norm_a, w_o_a, norm_kv, w_kv_b, k_norm_b, w_q_b, q_norm_b, w_o_b, w_up, conv_w, conv_b, w_down):
    B, S, D = x_prompt.shape
    DB, T, _ = x_sample.shape
    depth = norm_attn.shape[0]
    n_self = w_qkv_a.shape[0]
    caches = (cache_a_kv_g0, cache_a_kv_g1, cache_a_kv_g2)
    past = page_table.shape[1] * cache_b_kv_pool.shape[1]
    assert D == HD

    tab = rel_bias[_rel_bucket(jnp.arange(max(S, past + T, DIL_GROUPS[-1][0] + 1)))]
    group_bias = [tab[dil * np.arange(win // dil + 1)] for (win, dil) in DIL_GROUPS]

    ones =jnp.ones((HD,), F32)
    tile_h = lambda gvec, n: jnp.tile(gvec, n * N_HEADS)

    def self_layer(x, l, prompt):
        hg = jnp.concatenate([tile_h(q_norm_a[l], N_GROUPS), tile_h(k_norm_a[l], N_GROUPS), jnp.tile(ones, N_GROUPS)])
        qkv = norm_matmul(x, norm_attn[l], w_qkv_a[l].astype(BF16), hg, 2 * N_GROUPS * HD)
        outs, lses, bufs = [], [], []
        seq = S if prompt else T
        qkv3 = qkv.reshape(-1, seq, qkv.shape[1])
        for gi, (win, dil) in enumerate(DIL_GROUPS):
            keep = min(win, seq)
            part = lambda which: qkv3[:, seq - keep:, (which * N_GROUPS + gi) * HD:(which * N_GROUPS + gi + 1) * HD
                                      ].reshape(-1, keep, N_HEADS, HEAD_DIM)
            new_kv = jnp.stack([part(1), part(2)], axis=2)
            if prompt:
                o, lse = dil_prompt(qkv, B, S, gi, group_bias[gi])
                bufs.append(new_kv)
            else:
                o, lse, buf = dil_sample(caches[gi][l], new_kv, part(0), DB, T, gi, group_bias[gi])
                bufs.append(buf)
            outs.append(o)
            lses.append(lse)
        return merge_proj(outs, lses, w_o_a[l].astype(BF16), x), bufs

    def ffn(x, l, prompt):
        wup, wdn = w_up[l].astype(BF16), w_down[l].astype(BF16)
        if prompt:
            tm = 256
            y, ut = ffn_prompt(x, S, norm_ffn[l], wup, conv_w[l], conv_b[l], wdn, tm=tm)
            cs = ut.reshape(B, S // tm, 8, ut.shape[1])[:, -1, 8 - (CONV_W - 1):]
        else:
            y, ua, ub = ffn_sample(x, T, state_ffn_conv[l], norm_ffn[l], wup, conv_w[l], conv_b[l], wdn)
            u = jnp.concatenate([ua, ub], axis=1)
            cs = u.reshape(DB, T, u.shape[1])[:, T - (CONV_W - 1):]
        return y, cs

    def trunk(x, prompt):
        nb_rows = B if prompt else DB
        a_new = [[] for _ in range(N_GROUPS)]
        conv_new = []
        kvb = None
        prep = None
        for l in range(depth):
            if l < n_self:
                x, bufs = self_layer(x, l, prompt)
                for gi in range(N_GROUPS):
                    a_new[gi].append(bufs[gi])
            else:
                if l == n_self:
                    hg = jnp.concatenate([tile_h(k_norm_b, 1), ones])
                    kvb = norm_matmul(x, norm_kv, w_kv_b.astype(BF16), hg, HD)
                    if prompt:
                        prep = kv_prep(kvb, B)
                lb = l - n_self
                q = norm_matmul(x, norm_attn[l], w_q_b[lb].astype(BF16), tile_h(q_norm_b[lb], 1), HD)
                if prompt:
                    o = moba_prompt(q, prep[1], prep[2], prep[0], tab, B, S)
                else:
                    o = moba_sample(q, kvb, cache_b_kv_pool, page_table, tab, DB, T)
                x = merge_proj([o], None, w_o_b[lb].astype(BF16), x)
            x, cs = ffn(x, l, prompt)
            conv_new.append(cs)
        seq = S if prompt else T
        if prompt:
            kv_new = jnp.transpose(prep[3].reshape(B, 2, N_HEADS, HEAD_DIM, S), (0, 4, 1, 2, 3))
        else:
            kv_new = kvb.reshape(nb_rows, seq, 2, N_HEADS, HEAD_DIM)
        return x.reshape(nb_rows, seq, D), [jnp.stack(a, 0) for a in a_new], kv_new, jnp.stack(conv_new, 0)

    y_p, a_p, kv_p, conv_p = trunk(x_prompt.reshape(B * S, D), True)
    y_s, a_s, kv_s, conv_s = trunk(x_sample.reshape(DB * T, D), False)
    return (y_p, y_s, a_p[0], a_p[1], a_p[2], a_s[0], a_s[1], a_s[2], kv_p, kv_s, conv_p, conv_s)
```
